```python
import math
import jax, jax.numpy as jnp
from jax import lax
import numpy as np

D_MODEL = 1024
BATCH = 8
SEQ = 2048
DEPTH = 2

PLE_DIM = 256
EPS = 1e-6
MIX_W = D_MODEL
GLA_W = MIX_W // 2
DIL_W = MIX_W - GLA_W
GLA_HEADS = 4
GLA_DV = GLA_W // GLA_HEADS
GLA_DK = GLA_DV // 2
GLA_QK = GLA_HEADS * GLA_DK
GLA_LOWRANK = 16
GLA_TAU = 16.0
GLA_CHUNK = 16
DIL_HD = 64
DIL_HEADS = DIL_W // DIL_HD
DIL_PATTERNS = ((128, 1), (512, 4), (2048, 16))
BAND = 128
BLK = 128
REL_BUCKETS = 32
REL_MAX_DIST = 2048
D_FF = 4 * D_MODEL
NEG = -1e30
IN_SIZES = (GLA_QK, GLA_QK, GLA_W, GLA_W, GLA_LOWRANK, DIL_W, DIL_W, DIL_W)
IN_W = sum(IN_SIZES)
IN_SPLITS = tuple(int(c) for c in np.cumsum(IN_SIZES)[:-1])

kernel_name = "hybrid_gla_dilated_sandwich_ple"


def rmsnorm(x, g):
    x32 = x.astype(jnp.float32)
    y = x32 * lax.rsqrt(jnp.mean(x32 * x32, axis=-1, keepdims=True) + EPS)
    return (y * g.astype(jnp.float32)).astype(x.dtype)


def t5_bucket(dist):
    max_exact = REL_BUCKETS // 2
    d = jnp.maximum(dist, 1).astype(jnp.float32)
    large = max_exact + (jnp.log(d / max_exact) / math.log(REL_MAX_DIST / max_exact)
                         * (REL_BUCKETS - max_exact)).astype(jnp.int32)
    large = jnp.minimum(large, REL_BUCKETS - 1)
    return jnp.where(dist < max_exact, dist, large)


def gla_branch(q, k, v, log_a):
    B, S, H, K = q.shape
    V = v.shape[-1]
    C = GLA_CHUNK
    N = S // C
    q = q * (K ** -0.5)
    q, k, log_a = (t.reshape(B, N, C, H, K) for t in (q, k, log_a))
    v = v.reshape(B, N, C, H, V)
    b = jnp.cumsum(log_a, axis=2)
    b_last = b[:, :, -1]
    causal = jnp.tril(jnp.ones((C, C), dtype=bool))
    diff = b[:, :, :, None] - b[:, :, None, :]
    decay = jnp.exp(jnp.where(causal[None, None, :, :, None, None], diff, NEG))
    att = jnp.einsum('bnihk,bnjhk,bnijhk->bnhij', q, k, decay)
    o_intra = jnp.einsum('bnhij,bnjhv->bnihv', att, v)
    q_g = q * jnp.exp(b)
    k_g = k * jnp.exp(b_last[:, :, None] - b)
    kv = jnp.einsum('bnjhk,bnjhv->nbhkv', k_g, v)
    a_last = jnp.exp(b_last).transpose(1, 0, 2, 3)

    def step(state, inp):
        dec, kv_n = inp
        return dec[..., None] * state + kv_n, state

    _, states = lax.scan(step, jnp.zeros((B, H, K, V), jnp.float32), (a_last, kv))
    o_inter = jnp.einsum('bnihk,nbhkv->bnihv', q_g, states)
    return (o_intra + o_inter).reshape(B, S, H, V)


def dilated_branch(q, k, v, rel_bias, dil):
    B, S, H, E = q.shape
    L = S // dil
    nb = -(-L // BLK)
    Lp = nb * BLK

    def sub(t):
        t = t.reshape(B, L, dil, H, E)
        return jnp.pad(t, ((0, 0), (0, Lp - L), (0, 0), (0, 0), (0, 0)))

    def band(t):
        tp = jnp.pad(t, ((0, 0), (BLK, 0), (0, 0), (0, 0), (0, 0)))
        prev = tp[:, :Lp].reshape(B, nb, BLK, dil, H, E)
        cur = tp[:, BLK:].reshape(B, nb, BLK, dil, H, E)
        return jnp.concatenate([prev, cur], axis=2)

    qs = sub(q).reshape(B, nb, BLK, dil, H, E)
    kb, vb = band(sub(k)), band(sub(v))
    logits = jnp.einsum('bnqrhe,bnkrhe->bnrhqk', qs, kb) * (E ** -0.5)
    qi = jnp.arange(BLK)[:, None]
    ki = jnp.arange(2 * BLK)[None, :]
    j = qi + BLK - ki
    bias = rel_bias[t5_bucket(jnp.maximum(j, 0) * dil)]
    bias = bias.transpose(2, 0, 1).astype(jnp.float32)
    key_pos = jnp.arange(nb)[:, None, None] * BLK + ki[None] - BLK
    valid = (j >= 0)[None] & (j <= BAND)[None] & (key_pos >= 0)
    logits = jnp.where(valid[None, :, None, None], logits + bias, NEG)
    m = jnp.max(logits, axis=-1, keepdims=True)
    e = jnp.exp(logits - m)
    s = jnp.sum(e, axis=-1)
    o = jnp.einsum('bnrhqk,bnkrhe->bnqrhe', e, vb)
    o = o / s.transpose(0, 1, 4, 2, 3)[..., None]
    lse = (m[..., 0] + jnp.log(s)).transpose(0, 1, 4, 2, 3)
    o = o.reshape(B, Lp, dil, H, E)[:, :L].reshape(B, S, H, E)
    lse = lse.reshape(B, Lp, dil, H)[:, :L].reshape(B, S, H)
    return o, lse


def setup_inputs(seed: int = 0) -> dict:
    key = jax.random.key(seed)
    ks = jax.random.split(key, 20)
    nrm = lambda k, shape, s: jax.random.normal(k, shape, jnp.float32) * s
    gain = lambda k, shape: 1.0 + 0.05 * jax.random.normal(k, shape, jnp.float32)
    return {
        "x": nrm(ks[0], (BATCH, SEQ, D_MODEL), 1.0),
        "p": nrm(ks[1], (DEPTH, BATCH, SEQ, PLE_DIM), 1.0),
        "w_in": nrm(ks[2], (DEPTH, D_MODEL, IN_W), D_MODEL ** -0.5),
        "w_gla_a2": nrm(ks[3], (DEPTH, GLA_LOWRANK, GLA_QK), GLA_LOWRANK ** -0.5),
        "b_gla_a": nrm(ks[4], (DEPTH, GLA_QK), 0.1),
        "gla_norm_g": gain(ks[5], (DEPTH, GLA_DV)),
        "w_out": nrm(ks[6], (DEPTH, MIX_W, D_MODEL), MIX_W ** -0.5),
        "rel_bias": nrm(ks[7], (REL_BUCKETS, DIL_HEADS), 0.1),
        "pre_mix_g": gain(ks[8], (DEPTH, D_MODEL)),
        "post_mix_g": gain(ks[9], (DEPTH, D_MODEL)),
        "pre_mlp_g": gain(ks[10], (DEPTH, D_MODEL)),
        "post_mlp_g": gain(ks[11], (DEPTH, D_MODEL)),
        "w_mlp_in": nrm(ks[12], (DEPTH, D_MODEL, D_FF), D_MODEL ** -0.5),
        "w_mlp_out": nrm(ks[13], (DEPTH, D_FF, D_MODEL), D_FF ** -0.5),
        "w_ple_gate": nrm(ks[14], (DEPTH, D_MODEL, D_MODEL), D_MODEL ** -0.5),
        "w_ple_proj": nrm(ks[15], (DEPTH, PLE_DIM, D_MODEL), PLE_DIM ** -0.5),
    }


def reference(x, p, w_in, w_gla_a2, b_gla_a, gla_norm_g, w_out, rel_bias,
              pre_mix_g, post_mix_g, pre_mlp_g, post_mlp_g,
              w_mlp_in, w_mlp_out, w_ple_gate, w_ple_proj):
    B, S, _ = x.shape
    f32 = jnp.float32
    h = x
    for i in range(DEPTH):
        xn = rmsnorm(h, pre_mix_g[i])
        proj = xn @ w_in[i]
        gq, gk, gv, gg, glr, dq, dk, dv = jnp.split(proj, IN_SPLITS, axis=-1)
        log_a = jax.nn.log_sigmoid((glr @ w_gla_a2[i] + b_gla_a[i]).astype(f32)) / GLA_TAU
        o_gla = gla_branch(gq.astype(f32).reshape(B, S, GLA_HEADS, GLA_DK),
                           gk.astype(f32).reshape(B, S, GLA_HEADS, GLA_DK),
                           gv.astype(f32).reshape(B, S, GLA_HEADS, GLA_DV),
                           log_a.reshape(B, S, GLA_HEADS, GLA_DK))
        o_gla = rmsnorm(o_gla, gla_norm_g[i]).reshape(B, S, GLA_W)
        o_gla = o_gla * jax.nn.silu(gg.astype(f32))
        qd = dq.astype(f32).reshape(B, S, DIL_HEADS, DIL_HD)
        kd = dk.astype(f32).reshape(B, S, DIL_HEADS, DIL_HD)
        vd = dv.astype(f32).reshape(B, S, DIL_HEADS, DIL_HD)
        outs, lses = [], []
        for _, dil in DIL_PATTERNS:
            o_g, lse_g = dilated_branch(qd, kd, vd, rel_bias, dil)
            outs.append(o_g)
            lses.append(lse_g)
        wts = jax.nn.softmax(jnp.stack(lses, axis=0), axis=0)
        o_dil = jnp.sum(wts[..., None] * jnp.stack(outs, axis=0), axis=0).reshape(B, S, DIL_W)
        mix = jnp.concatenate([o_gla, o_dil], axis=-1).astype(h.dtype) @ w_out[i]
        h = h + rmsnorm(mix, post_mix_g[i])
        xn = rmsnorm(h, pre_mlp_g[i])
        f = jnp.square(jax.nn.relu(xn @ w_mlp_in[i])) @ w_mlp_out[i]
        h = h + rmsnorm(f, post_mlp_g[i])
        h = h + jax.nn.sigmoid(h @ w_ple_gate[i]) * (p[i] @ w_ple_proj[i])
    return h
```

```python
import functools
import math

import numpy as np
import jax
import jax.numpy as jnp
from jax import lax
from jax.experimental import pallas as pl
from jax.experimental.pallas import tpu as pltpu

F32 = jnp.float32
BF16 = jnp.bfloat16

EPS = 1e-6
GLA_HEADS = 4
GLA_DK = 64
GLA_DV = 128
GLA_QK = GLA_HEADS * GLA_DK
GLA_W = GLA_HEADS * GLA_DV
GLA_LOWRANK = 16
GLA_TAU = 16.0
DIL_HEADS = 8
DIL_HD = 64
DIL_W = DIL_HEADS * DIL_HD
DILATIONS = (1, 4, 16)
BAND = 128
BLK = 128
REL_BUCKETS = 32
REL_MAX_DIST = 2048
NEG = -1e30

LANES = 128
MACRO = 128
GLA_LEVELS = 7
VMEM_LIMIT = 56 * 1024 * 1024

C_GQ, C_GK, C_GV, C_GG, C_DQ, C_LR, C_END = 0, 256, 512, 1024, 1536, 3072, 3200


def _rms(x, g):
    return x * lax.rsqrt(jnp.mean(x * x, axis=-1, keepdims=True) + EPS) * g


def _in_proj_kernel(h_ref, g_ref, w_ref, wa2_ref, ba_ref,
                    gq_ref, gk_ref, la_ref, gv_ref, gg_ref, a1_ref, a4_ref, a16_ref,
                    slab_ref, *, tm):
    xn = _rms(h_ref[...], g_ref[...]).astype(BF16)

    def proj(lo, hi):
        return jnp.dot(xn, w_ref[:, lo:hi], preferred_element_type=F32)

    gq_ref[...] = proj(C_GQ, C_GK) * (GLA_DK ** -0.5)
    gk_ref[...] = proj(C_GK, C_GV)
    gv_ref[...] = proj(C_GV, C_GG).astype(BF16)
    gg_ref[...] = proj(C_GG, C_DQ).astype(BF16)
    z = jnp.dot(proj(C_LR, C_END).astype(BF16), wa2_ref[...],
                preferred_element_type=F32) + ba_ref[...]
    la_ref[...] = (jnp.minimum(z, 0.0) - jnp.log1p(jnp.exp(-jnp.abs(z)))) * (1.0 / GLA_TAU)

    for c in range(3):
        y = proj(C_DQ + DIL_W * c, C_DQ + DIL_W * (c + 1))
        if c == 0:
            y = y * (DIL_HD ** -0.5)
        for s in range(DIL_W // LANES):
            col = DIL_W * c + LANES * s
            ys = y[:, LANES * s:LANES * (s + 1)]
            a1_ref[0, :, col:col + LANES] = ys.astype(BF16)
            idx = c * (DIL_W // LANES) + s
            slab_ref[idx] = ys
            for d, ref in ((4, a4_ref), (16, a16_ref)):
                for r in range(d):
                    ref[r, :, col:col + LANES] = slab_ref[idx, pl.ds(r, tm // d, stride=d), :].astype(BF16)


def _in_proj(h, g, w, wa2, ba, *, batch, seq, tm):
    t = batch * seq
    nt = seq // tm
    row = lambda width: pl.BlockSpec((tm, width), lambda b, i: (b * nt + i, 0))
    const = lambda shape: pl.BlockSpec(shape, lambda b, i: (0,) * len(shape))
    dil_spec = lambda d: pl.BlockSpec((None, d, tm // d, 3 * DIL_W), lambda b, i: (b, 0, i, 0))
    out_shape = (
        jax.ShapeDtypeStruct((t, GLA_QK), F32), jax.ShapeDtypeStruct((t, GLA_QK), F32),
        jax.ShapeDtypeStruct((t, GLA_QK), F32),
        jax.ShapeDtypeStruct((t, GLA_W), BF16), jax.ShapeDtypeStruct((t, GLA_W), BF16),
    ) + tuple(jax.ShapeDtypeStruct((batch, d, seq // d, 3 * DIL_W), BF16) for d in DILATIONS)
    return pl.pallas_call(
        functools.partial(_in_proj_kernel, tm=tm),
        grid=(batch, nt),
        in_specs=[row(h.shape[1]), const(g.shape), const(w.shape), const(wa2.shape), const(ba.shape)],
        out_specs=(row(GLA_QK), row(GLA_QK), row(GLA_QK), row(GLA_W), row(GLA_W))
        + tuple(dil_spec(d) for d in DILATIONS),
        out_shape=out_shape,
        scratch_shapes=[pltpu.VMEM((3 * DIL_W // LANES, tm, LANES), F32)],
        compiler_params=pltpu.CompilerParams(
            dimension_semantics=("arbitrary", "arbitrary"), vmem_limit_bytes=VMEM_LIMIT),
        name="in_proj",
    )(h, g, w, wa2, ba)


def _gla_constants():
    n = MACRO
    i = np.arange(n)[:, None]
    t = np.arange(n)[None, :]
    blocks = [t <= i, t > i]
    for l in range(GLA_LEVELS):
        ref = ((i >> (l + 1)) << (l + 1)) + (1 << l) - 1
        odd = ((i >> l) & 1) == 1
        blocks.append(np.where(odd, (t > ref) & (t <= i), (t > i) & (t <= ref)))
    p = np.concatenate(blocks, axis=0).astype(np.float32)
    j = np.arange(n)[None, :]
    lvl = np.where(j < i, np.floor(np.log2(np.maximum(i ^ j, 1))).astype(np.int32),
                   np.where(j == i, GLA_LEVELS, -1)).astype(np.int32)
    return p, np.concatenate([lvl, lvl], axis=0)


def _split_heads(x, lo_half):
    zero = jnp.zeros_like(x)
    return jnp.concatenate([jnp.where(lo_half, x, zero), jnp.where(lo_half, zero, x)], axis=0)


def _dot_nt(a, b):
    return lax.dot_general(a, b, (((1,), (1,)), ((), ())), preferred_element_type=F32)


def _dot_tn(a, b):
    return lax.dot_general(a, b, (((0,), (0,)), ((), ())), preferred_element_type=F32)


def _gla_kernel(q_ref, k_ref, la_ref, v_ref, gg_ref, p_ref, lvl_ref, gn_ref, o_ref, st_ref,
                *, n_macro):
    st_ref[...] = jnp.zeros_like(st_ref)
    lo_half = lax.broadcasted_iota(jnp.int32, (MACRO, LANES), 1) < (LANES // 2)

    def body(m, carry):
        r0 = pl.multiple_of(m * MACRO, MACRO)
        rows = pl.ds(r0, MACRO)
        q = q_ref[rows, :]
        k = k_ref[rows, :]
        la = la_ref[rows, :]
        la_hi = la.astype(BF16)
        la_lo = (la - la_hi.astype(F32)).astype(BF16)
        pm = p_ref[...]
        x = jnp.exp(jnp.dot(pm, la_hi, preferred_element_type=F32)
                    + jnp.dot(pm, la_lo, preferred_element_type=F32))
        qg = q * x[0:MACRO]
        kg = k * x[MACRO:2 * MACRO]
        a_last = x[MACRO - 1:MACRO, :]
        lvl = lvl_ref[...]
        for pair in range(GLA_HEADS // 2):
            sl = slice(LANES * pair, LANES * (pair + 1))
            att = _dot_nt(_split_heads(q[:, sl], lo_half).astype(BF16), k[:, sl].astype(BF16))
            att = jnp.where(lvl == GLA_LEVELS, att, 0.0)
            for l in range(GLA_LEVELS):
                xl = x[(2 + l) * MACRO:(3 + l) * MACRO, sl]
                a = _dot_nt(_split_heads(q[:, sl] * xl, lo_half).astype(BF16),
                            (k[:, sl] * xl).astype(BF16))
                att = jnp.where(lvl == l, a, att)
            att = att.astype(BF16)
            st = st_ref[pair]
            inter = _dot_nt(_split_heads(qg[:, sl], lo_half).astype(BF16), st.astype(BF16))
            kgp = kg[:, sl].astype(BF16)
            upd = []
            for e in range(2):
                h = 2 * pair + e
                hs = slice(GLA_DV * h, GLA_DV * (h + 1))
                vh = v_ref[rows, hs]
                o = (jnp.dot(att[e * MACRO:(e + 1) * MACRO], vh, preferred_element_type=F32)
                     + inter[e * MACRO:(e + 1) * MACRO])
                gate = gg_ref[rows, hs].astype(F32)
                o = _rms(o, gn_ref[...]) * (gate / (1.0 + jnp.exp(-gate)))
                o_ref[rows, hs] = o.astype(o_ref.dtype)
                upd.append(_dot_tn(vh, kgp))
            st_ref[pair] = st * a_last[:, sl] + jnp.where(lo_half, upd[0], upd[1])
        return carry

    lax.fori_loop(0, n_macro, body, 0)


def _gla(gq, gk, la, gv, gg, gn, *, batch, seq):
    p_np, lvl_np = _gla_constants()
    pm = jnp.asarray(p_np, BF16)
    lvl = jnp.asarray(lvl_np)
    seq_spec = lambda width: pl.BlockSpec((None, seq, width), lambda b: (b, 0, 0))
    const = lambda shape: pl.BlockSpec(shape, lambda b: (0,) * len(shape))
    r3 = lambda a: a.reshape(batch, seq, a.shape[-1])
    return pl.pallas_call(
        functools.partial(_gla_kernel, n_macro=seq // MACRO),
        grid=(batch,),
        in_specs=[seq_spec(GLA_QK), seq_spec(GLA_QK), seq_spec(GLA_QK), seq_spec(GLA_W),
                  seq_spec(GLA_W), const(pm.shape), const(lvl.shape), const(gn.shape)],
        out_specs=seq_spec(GLA_W),
        out_shape=jax.ShapeDtypeStruct((batch, seq, GLA_W), BF16),
        scratch_shapes=[pltpu.VMEM((GLA_HEADS // 2, GLA_DV, LANES), F32)],
        compiler_params=pltpu.CompilerParams(
            dimension_semantics=("arbitrary",), vmem_limit_bytes=VMEM_LIMIT),
        name="gla",
    )(r3(gq), r3(gk), r3(la), r3(gv), r3(gg), pm, lvl, gn)


def _t5_bucket_np(dist):
    max_exact = REL_BUCKETS // 2
    d = np.maximum(dist, 1).astype(np.float32)
    large = max_exact + (np.log(d / np.float32(max_exact)) / np.float32(math.log(REL_MAX_DIST / max_exact))
                         * np.float32(REL_BUCKETS - max_exact)).astype(np.int32)
    large = np.minimum(large, REL_BUCKETS - 1)
    return np.where(dist < max_exact, dist, large).astype(np.int32)


def _bucket_table():
    qi = np.arange(BLK)[:, None]
    ki = np.arange(2 * BLK)[None, :]
    j = qi + BLK - ki
    valid = (j >= 0) & (j <= BAND)
    return np.stack([np.where(valid, _t5_bucket_np(np.maximum(j, 0) * d), -1) for d in DILATIONS]).astype(np.int32)


def _bias_kernel(rel_ref, bkt_ref, o_ref):
    pair = pl.program_id(1)
    bkt = bkt_ref[...]
    for e in range(2):
        acc = jnp.full(bkt.shape, NEG, F32)
        for u in range(REL_BUCKETS):
            acc = jnp.where(bkt == u, rel_ref[u, 2 * pair + e], acc)
        o_ref[e * BLK:(e + 1) * BLK, :] = acc


def _bias_table(rel_bias):
    bkt = jnp.asarray(_bucket_table())
    n_pat = len(DILATIONS)
    return pl.pallas_call(
        _bias_kernel,
        grid=(n_pat, DIL_HEADS // 2),
        in_specs=[pl.BlockSpec(memory_space=pltpu.SMEM),
                  pl.BlockSpec((None, BLK, 2 * BLK), lambda p, h: (p, 0, 0))],
        out_specs=pl.BlockSpec((None, None, 2 * BLK, 2 * BLK), lambda p, h: (p, h, 0, 0)),
        out_shape=jax.ShapeDtypeStruct((n_pat, DIL_HEADS // 2, 2 * BLK, 2 * BLK), F32),
        compiler_params=pltpu.CompilerParams(dimension_semantics=("arbitrary", "arbitrary")),
        name="bias_table",
    )(rel_bias, bkt)


def _attn_block(q_ref, k_ref, v_ref, bias, row0, has_prev, lo_half):
    q = q_ref[pl.ds(row0, BLK), :]
    if has_prev:
        krows = pl.ds(pl.multiple_of(row0 - BLK, BLK), 2 * BLK)
    else:
        krows = pl.ds(row0, BLK)
        bias = bias[:, BLK:]
    kk = k_ref[krows, :]
    vv = v_ref[krows, :]
    logits = _dot_nt(_split_heads(q, lo_half), kk) + bias
    mx = jnp.max(logits, axis=-1, keepdims=True)
    e = jnp.exp(logits - mx).astype(BF16)
    lo_k = lax.broadcasted_iota(jnp.int32, vv.shape, 1) < DIL_HD
    one = jnp.ones_like(vv)
    ta = jnp.dot(e[:BLK], jnp.where(lo_k, vv, one), preferred_element_type=F32)
    tb = jnp.dot(e[BLK:], jnp.where(lo_k, one, vv), preferred_element_type=F32)
    u = jnp.where(lo_half, ta, tb)
    v = pltpu.roll(jnp.where(lo_half, tb, ta), DIL_HD, 1)
    mt = jnp.where(lo_half, jnp.broadcast_to(mx[:BLK], (BLK, LANES)),
                   jnp.broadcast_to(mx[BLK:], (BLK, LANES)))
    return u, v, mt


def _attn_kernel(q1, k1, v1, q4, k4, v4, q16, k16, v16, bias_ref, o_ref, u_ref, w_ref, m_ref, *, seq):
    lo_half = lax.broadcasted_iota(jnp.int32, (BLK, LANES), 1) < DIL_HD
    refs = ((q1, k1, v1), (q4, k4, v4), (q16, k16, v16))
    for p, d in enumerate(DILATIONS):
        q_ref, k_ref, v_ref = refs[p]
        nb = seq // d // BLK

        def run_block(r, n, has_prev, p=p, d=d, nb=nb, q_ref=q_ref, k_ref=k_ref, v_ref=v_ref):
            row0 = pl.multiple_of((r * nb + n) * BLK, BLK)
            u, v, mt = _attn_block(q_ref, k_ref, v_ref, bias_ref[p], row0, has_prev, lo_half)
            dst = pl.ds(n * (BLK * d) + r, BLK, stride=d) if d > 1 else pl.ds(row0, BLK)
            u_ref[p, dst, :] = u
            w_ref[p, dst, :] = v
            m_ref[p, dst, :] = mt

        def residue(r, carry, nb=nb, run_block=run_block):
            run_block(r, 0, False)
            if nb > 1:
                def inner(n, c):
                    run_block(r, n, True)
                    return c
                lax.fori_loop(1, nb, inner, 0)
            return carry

        lax.fori_loop(0, d, residue, 0)

    chunk = 2 * BLK

    def combine(i, carry):
        rows = pl.ds(pl.multiple_of(i * chunk, chunk), chunk)
        ms = [m_ref[p, rows, :] for p in range(len(DILATIONS))]
        mmax = functools.reduce(jnp.maximum, ms)
        cs = [jnp.exp(mp - mmax) for mp in ms]
        num = sum(c * u_ref[p, rows, :] for p, c in enumerate(cs))
        den = sum(c * w_ref[p, rows, :] for p, c in enumerate(cs))
        o_ref[rows, :] = (num / den).astype(o_ref.dtype)
        return carry

    lax.fori_loop(0, seq // chunk, combine, 0)


def _dil_attn(a1, a4, a16, bias, *, batch, seq):
    n_pat = len(DILATIONS)
    n_pair = DIL_HEADS // 2
    arrs = [a.reshape(batch, seq, 3 * DIL_W) for a in (a1, a4, a16)]
    lane_blocks = DIL_W // LANES

    def spec(part):
        return pl.BlockSpec((None, seq, LANES), lambda b, h: (b, 0, part * lane_blocks + h))

    in_specs = [spec(part) for _ in range(n_pat) for part in range(3)]
    in_specs.append(pl.BlockSpec((n_pat, None, 2 * BLK, 2 * BLK), lambda b, h: (0, h, 0, 0)))
    args = [a for a in arrs for _ in range(3)] + [bias]
    return pl.pallas_call(
        functools.partial(_attn_kernel, seq=seq),
        grid=(batch, n_pair),
        in_specs=in_specs,
        out_specs=pl.BlockSpec((None, seq, LANES), lambda b, h: (b, 0, h)),
        out_shape=jax.ShapeDtypeStruct((batch, seq, DIL_W), BF16),
        scratch_shapes=[pltpu.VMEM((n_pat, seq, LANES), F32) for _ in range(3)],
        compiler_params=pltpu.CompilerParams(
            dimension_semantics=("arbitrary", "arbitrary"), vmem_limit_bytes=VMEM_LIMIT),
        name="dil_attn",
    )(*args)


def _post_kernel(h_ref, og_ref, od_ref, p_ref, wo_ref, w1_ref, w2_ref, wg_ref, wp_ref,
                 g_mix_ref, g_pre_ref, g_post_ref, o_ref, *, ff_chunk):
    mix = (jnp.dot(og_ref[...], wo_ref[:GLA_W, :], preferred_element_type=F32)
           + jnp.dot(od_ref[...], wo_ref[GLA_W:, :], preferred_element_type=F32))
    h1 = h_ref[...] + _rms(mix, g_mix_ref[...])
    xn = _rms(h1, g_pre_ref[...]).astype(BF16)
    f = jnp.zeros_like(h1)
    for c in range(w1_ref.shape[1] // ff_chunk):
        cols = slice(c * ff_chunk, (c + 1) * ff_chunk)
        a = jnp.maximum(jnp.dot(xn, w1_ref[:, cols], preferred_element_type=F32), 0.0)
        f = f + jnp.dot((a * a).astype(BF16), w2_ref[cols, :], preferred_element_type=F32)
    h2 = h1 + _rms(f, g_post_ref[...])
    gate = jnp.dot(h2.astype(BF16), wg_ref[...], preferred_element_type=F32)
    emb = jnp.dot(p_ref[...].astype(BF16), wp_ref[...], preferred_element_type=F32)
    o_ref[...] = h2 + emb / (1.0 + jnp.exp(-gate))


def _post(h, og, od, p, wo, w1, w2, wg, wp, g_mix, g_pre, g_post, *, tm):
    t, dm = h.shape
    row = lambda width: pl.BlockSpec((tm, width), lambda i: (i, 0))
    const = lambda a: pl.BlockSpec(a.shape, lambda i: (0, 0), pipeline_mode=pl.Buffered(1))
    consts = (wo, w1, w2, wg, wp, g_mix, g_pre, g_post)
    return pl.pallas_call(
        functools.partial(_post_kernel, ff_chunk=1024),
        grid=(t // tm,),
        in_specs=[row(dm), row(GLA_W), row(DIL_W), row(p.shape[1])] + [const(a) for a in consts],
        out_specs=row(dm),
        out_shape=jax.ShapeDtypeStruct((t, dm), F32),
        compiler_params=pltpu.CompilerParams(
            dimension_semantics=("arbitrary",), vmem_limit_bytes=VMEM_LIMIT),
        name="post",
    )(h, og, od, p, *consts)


def kernel(x, p, w_in, w_gla_a2, b_gla_a, gla_norm_g, w_out, rel_bias, pre_mix_g, post_mix_g,
           pre_mlp_g, post_mlp_g, w_mlp_in, w_mlp_out, w_ple_gate, w_ple_proj):
    batch, seq, dm = x.shape
    depth = w_in.shape[0]
    assert seq % (max(DILATIONS) * BLK) == 0 and seq % MACRO == 0
    t = batch * seq
    row1 = lambda a: a.reshape(1, -1)
    bias = _bias_table(rel_bias)
    h = x.reshape(t, dm)
    splits = np.cumsum([GLA_QK, GLA_QK, GLA_W, GLA_W, GLA_LOWRANK, DIL_W, DIL_W])
    for i in range(depth):
        gq_w, gk_w, gv_w, gg_w, lr_w, dq_w, dk_w, dv_w = jnp.split(w_in[i], splits, axis=1)
        w = jnp.concatenate(
            [gq_w, gk_w, gv_w, gg_w, dq_w, dk_w, dv_w, lr_w,
             jnp.zeros((dm, C_END - C_LR - GLA_LOWRANK), w_in.dtype)], axis=1).astype(BF16)
        wa2 = jnp.zeros((C_END - C_LR, GLA_QK), BF16).at[:GLA_LOWRANK].set(w_gla_a2[i].astype(BF16))
        gq, gk, la, gv, gg, a1, a4, a16 = _in_proj(
            h, row1(pre_mix_g[i]), w, wa2, row1(b_gla_a[i]), batch=batch, seq=seq, tm=512)
        og = _gla(gq, gk, la, gv, gg, row1(gla_norm_g[i]), batch=batch, seq=seq)
        od = _dil_attn(a1, a4, a16, bias, batch=batch, seq=seq)
        h = _post(h, og.reshape(t, GLA_W), od.reshape(t, DIL_W), p[i].reshape(t, -1),
                  w_out[i].astype(BF16), w_mlp_in[i].astype(BF16), w_mlp_out[i].astype(BF16),
                  w_ple_gate[i].astype(BF16), w_ple_proj[i].astype(BF16),
                  row1(post_mix_g[i]), row1(pre_mlp_g[i]), row1(post_mlp_g[i]), tm=256)
    return h.reshape(batch, seq, dm)
```

```python
import functools
import math

import numpy as np
import jax
import jax.numpy as jnp
from jax import lax
from jax.experimental import pallas as pl
from jax.experimental.pallas import tpu as pltpu

F32 = jnp.float32
BF16 = jnp.bfloat16

EPS = 1e-6
GLA_HEADS = 4
GLA_DK = 64
GLA_DV = 128
GLA_QK = GLA_HEADS * GLA_DK
GLA_W = GLA_HEADS * GLA_DV
GLA_LOWRANK = 16
GLA_TAU = 16.0
DIL_HEADS = 8
DIL_HD = 64
DIL_W = DIL_HEADS * DIL_HD
DILATIONS = (1, 4, 16)
BAND = 128
BLK = 128
REL_BUCKETS = 32
REL_MAX_DIST = 2048
NEG = -1e30

LANES = 128
MACRO = 128
GLA_LEVELS = 7
VMEM_LIMIT = 56 * 1024 * 1024

C_GQ, C_GK, C_GV, C_GG, C_DQ, C_LR, C_END = 0, 256, 512, 1024, 1536, 3072, 3200


def _rms(x, g):
    return x * lax.rsqrt(jnp.mean(x * x, axis=-1, keepdims=True) + EPS) * g


def _in_proj_kernel(h_ref, g_ref, w_ref, wa2_ref, ba_ref,
                    gq_ref, gk_ref, la_ref, gv_ref, gg_ref, a1_ref, a4_ref, a16_ref,
                    slab_ref, *, tm):
    xn = _rms(h_ref[...], g_ref[...]).astype(BF16)

    def proj(lo, hi):
        return jnp.dot(xn, w_ref[:, lo:hi], preferred_element_type=F32)

    gq_ref[...] = proj(C_GQ, C_GK) * (GLA_DK ** -0.5)
    gk_ref[...] = proj(C_GK, C_GV)
    gv_ref[...] = proj(C_GV, C_GG).astype(BF16)
    gg_ref[...] = proj(C_GG, C_DQ).astype(BF16)
    z = jnp.dot(proj(C_LR, C_END).astype(BF16), wa2_ref[...],
                preferred_element_type=F32) + ba_ref[...]
    la_ref[...] = (jnp.minimum(z, 0.0) - jnp.log1p(jnp.exp(-jnp.abs(z)))) * (1.0 / GLA_TAU)

    for c in range(3):
        y = proj(C_DQ + DIL_W * c, C_DQ + DIL_W * (c + 1))
        if c == 0:
            y = y * (DIL_HD ** -0.5)
        for s in range(DIL_W // LANES):
            col = DIL_W * c + LANES * s
            ys = y[:, LANES * s:LANES * (s + 1)]
            a1_ref[0, :, col:col + LANES] = ys.astype(BF16)
            idx = c * (DIL_W // LANES) + s
            slab_ref[idx] = ys
            for d, ref in ((4, a4_ref), (16, a16_ref)):
                for r in range(d):
                    ref[r, :, col:col + LANES] = slab_ref[idx, pl.ds(r, tm // d, stride=d), :].astype(BF16)


def _in_proj(h, g, w, wa2, ba, *, batch, seq, tm):
    t = batch * seq
    nt = seq // tm
    row = lambda width: pl.BlockSpec((tm, width), lambda b, i: (b * nt + i, 0))
    const = lambda shape: pl.BlockSpec(shape, lambda b, i: (0,) * len(shape))
    dil_spec = lambda d: pl.BlockSpec((None, d, tm // d, 3 * DIL_W), lambda b, i: (b, 0, i, 0))
    out_shape = (
        jax.ShapeDtypeStruct((t, GLA_QK), F32), jax.ShapeDtypeStruct((t, GLA_QK), F32),
        jax.ShapeDtypeStruct((t, GLA_QK), F32),
        jax.ShapeDtypeStruct((t, GLA_W), BF16), jax.ShapeDtypeStruct((t, GLA_W), BF16),
    ) + tuple(jax.ShapeDtypeStruct((batch, d, seq // d, 3 * DIL_W), BF16) for d in DILATIONS)
    return pl.pallas_call(
        functools.partial(_in_proj_kernel, tm=tm),
        grid=(batch, nt),
        in_specs=[row(h.shape[1]), const(g.shape), const(w.shape), const(wa2.shape), const(ba.shape)],
        out_specs=(row(GLA_QK), row(GLA_QK), row(GLA_QK), row(GLA_W), row(GLA_W))
        + tuple(dil_spec(d) for d in DILATIONS),
        out_shape=out_shape,
        scratch_shapes=[pltpu.VMEM((3 * DIL_W // LANES, tm, LANES), F32)],
        compiler_params=pltpu.CompilerParams(
            dimension_semantics=("arbitrary", "arbitrary"), vmem_limit_bytes=VMEM_LIMIT),
        name="in_proj",
    )(h, g, w, wa2, ba)


def _gla_constants():
    n = MACRO
    i = np.arange(n)[:, None]
    t = np.arange(n)[None, :]
    blocks = [t <= i, t > i]
    for l in range(GLA_LEVELS):
        ref = ((i >> (l + 1)) << (l + 1)) + (1 << l) - 1
        odd = ((i >> l) & 1) == 1
        blocks.append(np.where(odd, (t > ref) & (t <= i), (t > i) & (t <= ref)))
    p = np.concatenate(blocks, axis=0).astype(np.float32)
    j = np.arange(n)[None, :]
    lvl = np.where(j < i, np.floor(np.log2(np.maximum(i ^ j, 1))).astype(np.int32),
                   np.where(j == i, GLA_LEVELS, -1)).astype(np.int32)
    return p, np.concatenate([lvl, lvl], axis=0)


def _split_heads(x, lo_half):
    zero = jnp.zeros_like(x)
    return jnp.concatenate([jnp.where(lo_half, x, zero), jnp.where(lo_half, zero, x)], axis=0)


def _dot_nt(a, b):
    return lax.dot_general(a, b, (((1,), (1,)), ((), ())), preferred_element_type=F32)


def _dot_tn(a, b):
    return lax.dot_general(a, b, (((0,), (0,)), ((), ())), preferred_element_type=F32)


def _gla_kernel(q_ref, k_ref, la_ref, v_ref, gg_ref, p_ref, lvl_ref, gn_ref, o_ref, st_ref,
                *, n_macro):
    st_ref[...] = jnp.zeros_like(st_ref)
    lo_half = lax.broadcasted_iota(jnp.int32, (MACRO, LANES), 1) < (LANES // 2)

    def body(m, carry):
        r0 = pl.multiple_of(m * MACRO, MACRO)
        rows = pl.ds(r0, MACRO)
        q = q_ref[rows, :]
        k = k_ref[rows, :]
        la = la_ref[rows, :]
        la_hi = la.astype(BF16)
        la_lo = (la - la_hi.astype(F32)).astype(BF16)
        pm = p_ref[...]
        x = jnp.exp(jnp.dot(pm, la_hi, preferred_element_type=F32)
                    + jnp.dot(pm, la_lo, preferred_element_type=F32))
        qg = q * x[0:MACRO]
        kg = k * x[MACRO:2 * MACRO]
        a_last = x[MACRO - 1:MACRO, :]
        lvl = lvl_ref[...]
        for pair in range(GLA_HEADS // 2):
            sl = slice(LANES * pair, LANES * (pair + 1))
            att = _dot_nt(_split_heads(q[:, sl], lo_half).astype(BF16), k[:, sl].astype(BF16))
            att = jnp.where(lvl == GLA_LEVELS, att, 0.0)
            for l in range(GLA_LEVELS):
                xl = x[(2 + l) * MACRO:(3 + l) * MACRO, sl]
                a = _dot_nt(_split_heads(q[:, sl] * xl, lo_half).astype(BF16),
                            (k[:, sl] * xl).astype(BF16))
                att = jnp.where(lvl == l, a, att)
            att = att.astype(BF16)
            st = st_ref[pair]
            inter = _dot_nt(_split_heads(qg[:, sl], lo_half).astype(BF16), st.astype(BF16))
            kgp = kg[:, sl].astype(BF16)
            upd = []
            for e in range(2):
                h = 2 * pair + e
                hs = slice(GLA_DV * h, GLA_DV * (h + 1))
                vh = v_ref[rows, hs]
                o = (jnp.dot(att[e * MACRO:(e + 1) * MACRO], vh, preferred_element_type=F32)
                     + inter[e * MACRO:(e + 1) * MACRO])
                gate = gg_ref[rows, hs].astype(F32)
                o = _rms(o, gn_ref[...]) * (gate / (1.0 + jnp.exp(-gate)))
                o_ref[rows, hs] = o.astype(o_ref.dtype)
                upd.append(_dot_tn(vh, kgp))
            st_ref[pair] = st * a_last[:, sl] + jnp.where(lo_half, upd[0], upd[1])
        return carry

    lax.fori_loop(0, n_macro, body, 0)


def _gla(gq, gk, la, gv, gg, gn, *, batch, seq):
    p_np, lvl_np = _gla_constants()
    pm = jnp.asarray(p_np, BF16)
    lvl = jnp.asarray(lvl_np)
    seq_spec = lambda width: pl.BlockSpec((None, seq, width), lambda b: (b, 0, 0))
    const = lambda shape: pl.BlockSpec(shape, lambda b: (0,) * len(shape))
    r3 = lambda a: a.reshape(batch, seq, a.shape[-1])
    return pl.pallas_call(
        functools.partial(_gla_kernel, n_macro=seq // MACRO),
        grid=(batch,),
        in_specs=[seq_spec(GLA_QK), seq_spec(GLA_QK), seq_spec(GLA_QK), seq_spec(GLA_W),
                  seq_spec(GLA_W), const(pm.shape), const(lvl.shape), const(gn.shape)],
        out_specs=seq_spec(GLA_W),
        out_shape=jax.ShapeDtypeStruct((batch, seq, GLA_W), BF16),
        scratch_shapes=[pltpu.VMEM((GLA_HEADS // 2, GLA_DV, LANES), F32)],
        compiler_params=pltpu.CompilerParams(
            dimension_semantics=("arbitrary",), vmem_limit_bytes=VMEM_LIMIT),
        name="gla",
    )(r3(gq), r3(gk), r3(la), r3(gv), r3(gg), pm, lvl, gn)


def _t5_bucket_np(dist):
    max_exact = REL_BUCKETS // 2
    d = np.maximum(dist, 1).astype(np.float32)
    large = max_exact + (np.log(d / np.float32(max_exact)) / np.float32(math.log(REL_MAX_DIST / max_exact))
                         * np.float32(REL_BUCKETS - max_exact)).astype(np.int32)
    large = np.minimum(large, REL_BUCKETS - 1)
    return np.where(dist < max_exact, dist, large).astype(np.int32)


def _bucket_table():
    qi = np.arange(BLK)[:, None]
    ki = np.arange(2 * BLK)[None, :]
    j = qi + BLK - ki
    valid = (j >= 0) & (j <= BAND)
    return np.stack([np.where(valid, _t5_bucket_np(np.maximum(j, 0) * d), -1) for d in DILATIONS]).astype(np.int32)


def _bias_kernel(rel_ref, bkt_ref, o_ref):
    pair = pl.program_id(1)
    bkt = bkt_ref[...]
    for e in range(2):
        acc = jnp.full(bkt.shape, NEG, F32)
        for u in range(REL_BUCKETS):
            acc = jnp.where(bkt == u, rel_ref[u, 2 * pair + e], acc)
        o_ref[e * BLK:(e + 1) * BLK, :] = acc


def _bias_table(rel_bias):
    bkt = jnp.asarray(_bucket_table())
    n_pat = len(DILATIONS)
    return pl.pallas_call(
        _bias_kernel,
        grid=(n_pat, DIL_HEADS // 2),
        in_specs=[pl.BlockSpec(memory_space=pltpu.SMEM),
                  pl.BlockSpec((None, BLK, 2 * BLK), lambda p, h: (p, 0, 0))],
        out_specs=pl.BlockSpec((None, None, 2 * BLK, 2 * BLK), lambda p, h: (p, h, 0, 0)),
        out_shape=jax.ShapeDtypeStruct((n_pat, DIL_HEADS // 2, 2 * BLK, 2 * BLK), F32),
        compiler_params=pltpu.CompilerParams(dimension_semantics=("arbitrary", "arbitrary")),
        name="bias_table",
    )(rel_bias, bkt)


def _attn_block(q_ref, k_ref, v_ref, bias, row0, has_prev, lo_half):
    q = q_ref[pl.ds(row0, BLK), :]
    if has_prev:
        krows = pl.ds(row0 - BLK, 2 * BLK)
    else:
        krows = pl.ds(row0, BLK)
        bias = bias[:, BLK:]
    kk = k_ref[krows, :]
    vv = v_ref[krows, :]
    logits = _dot_nt(_split_heads(q, lo_half), kk) + bias
    mx = jnp.max(logits, axis=-1, keepdims=True)
    e = jnp.exp(logits - mx).astype(BF16)
    lo_k = lax.broadcasted_iota(jnp.int32, vv.shape, 1) < DIL_HD
    one = jnp.ones_like(vv)
    ta = jnp.dot(e[:BLK], jnp.where(lo_k, vv, one), preferred_element_type=F32)
    tb = jnp.dot(e[BLK:], jnp.where(lo_k, one, vv), preferred_element_type=F32)
    u = jnp.where(lo_half, ta, tb)
    v = pltpu.roll(jnp.where(lo_half, tb, ta), DIL_HD, 1)
    mt = jnp.where(lo_half, jnp.broadcast_to(mx[:BLK], (BLK, LANES)),
                   jnp.broadcast_to(mx[BLK:], (BLK, LANES)))
    return u, v, mt


def _attn_kernel(q1, k1, v1, q4, k4, v4, q16, k16, v16, bias_ref, o_ref, u_ref, w_ref, m_ref, *, seq):
    lo_half = lax.broadcasted_iota(jnp.int32, (BLK, LANES), 1) < DIL_HD
    refs = ((q1, k1, v1), (q4, k4, v4), (q16, k16, v16))
    for p, d in enumerate(DILATIONS):
        q_ref, k_ref, v_ref = refs[p]
        nb = seq // d // BLK

        for g in range(seq // BLK):
            r, n = divmod(g, nb)
            row0 = g * BLK
            u, v, mt = _attn_block(q_ref, k_ref, v_ref, bias_ref[p], row0, n > 0, lo_half)
            dst = pl.ds(n * (BLK * d) + r, BLK, stride=d) if d > 1 else pl.ds(row0, BLK)
            u_ref[p, dst, :] = u
            w_ref[p, dst, :] = v
            m_ref[p, dst, :] = mt

    chunk = 2 * BLK

    def combine(i, carry):
        rows = pl.ds(pl.multiple_of(i * chunk, chunk), chunk)
        ms = [m_ref[p, rows, :] for p in range(len(DILATIONS))]
        mmax = functools.reduce(jnp.maximum, ms)
        cs = [jnp.exp(mp - mmax) for mp in ms]
        num = sum(c * u_ref[p, rows, :] for p, c in enumerate(cs))
        den = sum(c * w_ref[p, rows, :] for p, c in enumerate(cs))
        o_ref[rows, :] = (num / den).astype(o_ref.dtype)
        return carry

    lax.fori_loop(0, seq // chunk, combine, 0)


def _dil_attn(a1, a4, a16, bias, *, batch, seq):
    n_pat = len(DILATIONS)
    n_pair = DIL_HEADS // 2
    arrs = [a.reshape(batch, seq, 3 * DIL_W) for a in (a1, a4, a16)]
    lane_blocks = DIL_W // LANES

    def spec(part):
        return pl.BlockSpec((None, seq, LANES), lambda b, h: (b, 0, part * lane_blocks + h))

    in_specs = [spec(part) for _ in range(n_pat) for part in range(3)]
    in_specs.append(pl.BlockSpec((n_pat, None, 2 * BLK, 2 * BLK), lambda b, h: (0, h, 0, 0)))
    args = [a for a in arrs for _ in range(3)] + [bias]
    return pl.pallas_call(
        functools.partial(_attn_kernel, seq=seq),
        grid=(batch, n_pair),
        in_specs=in_specs,
        out_specs=pl.BlockSpec((None, seq, LANES), lambda b, h: (b, 0, h)),
        out_shape=jax.ShapeDtypeStruct((batch, seq, DIL_W), BF16),
        scratch_shapes=[pltpu.VMEM((n_pat, seq, LANES), F32) for _ in range(3)],
        compiler_params=pltpu.CompilerParams(
            dimension_semantics=("arbitrary", "arbitrary"), vmem_limit_bytes=VMEM_LIMIT),
        name="dil_attn",
    )(*args)


def _post_kernel(h_ref, og_ref, od_ref, p_ref, wo_ref, w1_ref, w2_ref, wg_ref, wp_ref,
                 g_mix_ref, g_pre_ref, g_post_ref, o_ref, *, ff_chunk):
    mix = (jnp.dot(og_ref[...], wo_ref[:GLA_W, :], preferred_element_type=F32)
           + jnp.dot(od_ref[...], wo_ref[GLA_W:, :], preferred_element_type=F32))
    h1 = h_ref[...] + _rms(mix, g_mix_ref[...])
    xn = _rms(h1, g_pre_ref[...]).astype(BF16)
    f = jnp.zeros_like(h1)
    for c in range(w1_ref.shape[1] // ff_chunk):
        cols = slice(c * ff_chunk, (c + 1) * ff_chunk)
        a = jnp.maximum(jnp.dot(xn, w1_ref[:, cols], preferred_element_type=F32), 0.0)
        f = f + jnp.dot((a * a).astype(BF16), w2_ref[cols, :], preferred_element_type=F32)
    h2 = h1 + _rms(f, g_post_ref[...])
    gate = jnp.dot(h2.astype(BF16), wg_ref[...], preferred_element_type=F32)
    emb = jnp.dot(p_ref[...].astype(BF16), wp_ref[...], preferred_element_type=F32)
    o_ref[...] = h2 + emb / (1.0 + jnp.exp(-gate))


def _post(h, og, od, p, wo, w1, w2, wg, wp, g_mix, g_pre, g_post, *, tm):
    t, dm = h.shape
    row = lambda width: pl.BlockSpec((tm, width), lambda i: (i, 0))
    const = lambda a: pl.BlockSpec(a.shape, lambda i: (0, 0), pipeline_mode=pl.Buffered(1))
    consts = (wo, w1, w2, wg, wp, g_mix, g_pre, g_post)
    return pl.pallas_call(
        functools.partial(_post_kernel, ff_chunk=1024),
        grid=(t // tm,),
        in_specs=[row(dm), row(GLA_W), row(DIL_W), row(p.shape[1])] + [const(a) for a in consts],
        out_specs=row(dm),
        out_shape=jax.ShapeDtypeStruct((t, dm), F32),
        compiler_params=pltpu.CompilerParams(
            dimension_semantics=("arbitrary",), vmem_limit_bytes=VMEM_LIMIT),
        name="post",
    )(h, og, od, p, *consts)


def kernel(x, p, w_in, w_gla_a2, b_gla_a, gla_norm_g, w_out, rel_bias, pre_mix_g, post_mix_g,
           pre_mlp_g, post_mlp_g, w_mlp_in, w_mlp_out, w_ple_gate, w_ple_proj):
    batch, seq, dm = x.shape
    depth = w_in.shape[0]
    assert seq % (max(DILATIONS) * BLK) == 0 and seq % MACRO == 0
    t = batch * seq
    row1 = lambda a: a.reshape(1, -1)
    bias = _bias_table(rel_bias)
    h = x.reshape(t, dm)
    splits = np.cumsum([GLA_QK, GLA_QK, GLA_W, GLA_W, GLA_LOWRANK, DIL_W, DIL_W])
    for i in range(depth):
        gq_w, gk_w, gv_w, gg_w, lr_w, dq_w, dk_w, dv_w = jnp.split(w_in[i], splits, axis=1)
        w = jnp.concatenate(
            [gq_w, gk_w, gv_w, gg_w, dq_w, dk_w, dv_w, lr_w,
             jnp.zeros((dm, C_END - C_LR - GLA_LOWRANK), w_in.dtype)], axis=1).astype(BF16)
        wa2 = jnp.zeros((C_END - C_LR, GLA_QK), BF16).at[:GLA_LOWRANK].set(w_gla_a2[i].astype(BF16))
        gq, gk, la, gv, gg, a1, a4, a16 = _in_proj(
            h, row1(pre_mix_g[i]), w, wa2, row1(b_gla_a[i]), batch=batch, seq=seq, tm=512)
        og = _gla(gq, gk, la, gv, gg, row1(gla_norm_g[i]), batch=batch, seq=seq)
        od = _dil_attn(a1, a4, a16, bias, batch=batch, seq=seq)
        h = _post(h, og.reshape(t, GLA_W), od.reshape(t, DIL_W), p[i].reshape(t, -1),
                  w_out[i].astype(BF16), w_mlp_in[i].astype(BF16), w_mlp_out[i].astype(BF16),
                  w_ple_gate[i].astype(BF16), w_ple_proj[i].astype(BF16),
                  row1(post_mix_g[i]), row1(pre_mlp_g[i]), row1(post_mlp_g[i]), tm=256)
    return h.reshape(batch, seq, dm)
```

```python
import functools
import math

import numpy as np
import jax
import jax.numpy as jnp
from jax import lax
from jax.experimental import pallas as pl
from jax.experimental.pallas import tpu as pltpu

F32 = jnp.float32
BF16 = jnp.bfloat16

EPS = 1e-6
GLA_HEADS = 4
GLA_DK = 64
GLA_DV = 128
GLA_QK = GLA_HEADS * GLA_DK
GLA_W = GLA_HEADS * GLA_DV
GLA_LOWRANK = 16
GLA_TAU = 16.0
DIL_HEADS = 8
DIL_HD = 64
DIL_W = DIL_HEADS * DIL_HD
DILATIONS = (1, 4, 16)
BAND = 128
BLK = 128
REL_BUCKETS = 32
REL_MAX_DIST = 2048
NEG = -1e30
LOG2E = math.log2(math.e)

LANES = 128
MACRO = 128
GLA_LEVELS = 7
VMEM_LIMIT = 56 * 1024 * 1024

C_GQ, C_GK, C_GV, C_GG, C_DQ, C_LR, C_END = 0, 256, 512, 1024, 1536, 3072, 3200


def _rms(x, g):
    return x * lax.rsqrt(jnp.mean(x * x, axis=-1, keepdims=True) + EPS) * g


def _in_proj_kernel(h_ref, g_ref, w_ref, wa2_ref, ba_ref,
                    gq_ref, gk_ref, la_ref, gv_ref, gg_ref, a1_ref, a4_ref, a16_ref,
                    slab_ref, *, tm):
    xn = _rms(h_ref[...], g_ref[...]).astype(BF16)

    def proj(lo, hi):
        return jnp.dot(xn, w_ref[:, lo:hi], preferred_element_type=F32)

    gq_ref[...] = proj(C_GQ, C_GK) * (GLA_DK ** -0.5)
    gk_ref[...] = proj(C_GK, C_GV)
    gv_ref[...] = proj(C_GV, C_GG).astype(BF16)
    gg_ref[...] = proj(C_GG, C_DQ).astype(BF16)
    z = jnp.dot(proj(C_LR, C_END).astype(BF16), wa2_ref[...],
                preferred_element_type=F32) + ba_ref[...]
    la_ref[...] = (jnp.minimum(z, 0.0) - jnp.log1p(jnp.exp(-jnp.abs(z)))) * (1.0 / GLA_TAU)

    for c in range(3):
        y = proj(C_DQ + DIL_W * c, C_DQ + DIL_W * (c + 1))
        if c == 0:
            y = y * (LOG2E * DIL_HD ** -0.5)
        for s in range(DIL_W // LANES):
            col = DIL_W * c + LANES * s
            ys = y[:, LANES * s:LANES * (s + 1)]
            a1_ref[0, :, col:col + LANES] = ys.astype(BF16)
            idx = c * (DIL_W // LANES) + s
            slab_ref[idx] = ys
            for d, ref in ((4, a4_ref), (16, a16_ref)):
                for r in range(d):
                    ref[r, :, col:col + LANES] = slab_ref[idx, pl.ds(r, tm // d, stride=d), :].astype(BF16)


def _in_proj(h, g, w, wa2, ba, *, batch, seq, tm):
    t = batch * seq
    nt = seq // tm
    row = lambda width: pl.BlockSpec((tm, width), lambda b, i: (b * nt + i, 0))
    const = lambda shape: pl.BlockSpec(shape, lambda b, i: (0,) * len(shape))
    dil_spec = lambda d: pl.BlockSpec((None, d, tm // d, 3 * DIL_W), lambda b, i: (b, 0, i, 0))
    out_shape = (
        jax.ShapeDtypeStruct((t, GLA_QK), F32), jax.ShapeDtypeStruct((t, GLA_QK), F32),
        jax.ShapeDtypeStruct((t, GLA_QK), F32),
        jax.ShapeDtypeStruct((t, GLA_W), BF16), jax.ShapeDtypeStruct((t, GLA_W), BF16),
    ) + tuple(jax.ShapeDtypeStruct((batch, d, seq // d, 3 * DIL_W), BF16) for d in DILATIONS)
    return pl.pallas_call(
        functools.partial(_in_proj_kernel, tm=tm),
        grid=(batch, nt),
        in_specs=[row(h.shape[1]), const(g.shape), const(w.shape), const(wa2.shape), const(ba.shape)],
        out_specs=(row(GLA_QK), row(GLA_QK), row(GLA_QK), row(GLA_W), row(GLA_W))
        + tuple(dil_spec(d) for d in DILATIONS),
        out_shape=out_shape,
        scratch_shapes=[pltpu.VMEM((3 * DIL_W // LANES, tm, LANES), F32)],
        compiler_params=pltpu.CompilerParams(
            dimension_semantics=("arbitrary", "arbitrary"), vmem_limit_bytes=VMEM_LIMIT),
        name="in_proj",
    )(h, g, w, wa2, ba)


def _gla_constants():
    n = MACRO
    i = np.arange(n)[:, None]
    t = np.arange(n)[None, :]
    blocks = [t <= i, t > i]
    for l in range(GLA_LEVELS):
        ref = ((i >> (l + 1)) << (l + 1)) + (1 << l) - 1
        odd = ((i >> l) & 1) == 1
        blocks.append(np.where(odd, (t > ref) & (t <= i), (t > i) & (t <= ref)))
    p = np.concatenate(blocks, axis=0).astype(np.float32)
    j = np.arange(n)[None, :]
    lvl = np.where(j < i, np.floor(np.log2(np.maximum(i ^ j, 1))).astype(np.int32),
                   np.where(j == i, GLA_LEVELS, -1)).astype(np.int32)
    return p, np.concatenate([lvl, lvl], axis=0)


def _split_heads(x, lo_half):
    zero = jnp.zeros_like(x)
    return jnp.concatenate([jnp.where(lo_half, x, zero), jnp.where(lo_half, zero, x)], axis=0)


def _dot_nt(a, b):
    return lax.dot_general(a, b, (((1,), (1,)), ((), ())), preferred_element_type=F32)


def _dot_tn(a, b):
    return lax.dot_general(a, b, (((0,), (0,)), ((), ())), preferred_element_type=F32)


def _gla_kernel(q_ref, k_ref, la_ref, v_ref, gg_ref, p_ref, lvl_ref, gn_ref, o_ref, st_ref,
                *, n_macro):
    st_ref[...] = jnp.zeros_like(st_ref)
    lo_half = lax.broadcasted_iota(jnp.int32, (MACRO, LANES), 1) < (LANES // 2)

    def body(m, carry):
        r0 = pl.multiple_of(m * MACRO, MACRO)
        rows = pl.ds(r0, MACRO)
        q = q_ref[rows, :]
        k = k_ref[rows, :]
        la = la_ref[rows, :]
        la_hi = la.astype(BF16)
        la_lo = (la - la_hi.astype(F32)).astype(BF16)
        pm = p_ref[...]
        x = jnp.exp(jnp.dot(pm, la_hi, preferred_element_type=F32)
                    + jnp.dot(pm, la_lo, preferred_element_type=F32))
        qg = q * x[0:MACRO]
        kg = k * x[MACRO:2 * MACRO]
        a_last = x[MACRO - 1:MACRO, :]
        lvl = lvl_ref[...]
        for pair in range(GLA_HEADS // 2):
            sl = slice(LANES * pair, LANES * (pair + 1))
            att = _dot_nt(_split_heads(q[:, sl], lo_half).astype(BF16), k[:, sl].astype(BF16))
            att = jnp.where(lvl == GLA_LEVELS, att, 0.0)
            for l in range(GLA_LEVELS):
                xl = x[(2 + l) * MACRO:(3 + l) * MACRO, sl]
                a = _dot_nt(_split_heads(q[:, sl] * xl, lo_half).astype(BF16),
                            (k[:, sl] * xl).astype(BF16))
                att = jnp.where(lvl == l, a, att)
            att = att.astype(BF16)
            st = st_ref[pair]
            inter = _dot_nt(_split_heads(qg[:, sl], lo_half).astype(BF16), st.astype(BF16))
            kgp = kg[:, sl].astype(BF16)
            upd = []
            for e in range(2):
                h = 2 * pair + e
                hs = slice(GLA_DV * h, GLA_DV * (h + 1))
                vh = v_ref[rows, hs]
                o = (jnp.dot(att[e * MACRO:(e + 1) * MACRO], vh, preferred_element_type=F32)
                     + inter[e * MACRO:(e + 1) * MACRO])
                gate = gg_ref[rows, hs].astype(F32)
                o = _rms(o, gn_ref[...]) * (gate / (1.0 + jnp.exp(-gate)))
                o_ref[rows, hs] = o.astype(o_ref.dtype)
                upd.append(_dot_tn(vh, kgp))
            st_ref[pair] = st * a_last[:, sl] + jnp.where(lo_half, upd[0], upd[1])
        return carry

    lax.fori_loop(0, n_macro, body, 0)


def _gla(gq, gk, la, gv, gg, gn, *, batch, seq):
    p_np, lvl_np = _gla_constants()
    pm = jnp.asarray(p_np, BF16)
    lvl = jnp.asarray(lvl_np)
    seq_spec = lambda width: pl.BlockSpec((None, seq, width), lambda b: (b, 0, 0))
    const = lambda shape: pl.BlockSpec(shape, lambda b: (0,) * len(shape))
    r3 = lambda a: a.reshape(batch, seq, a.shape[-1])
    return pl.pallas_call(
        functools.partial(_gla_kernel, n_macro=seq // MACRO),
        grid=(batch,),
        in_specs=[seq_spec(GLA_QK), seq_spec(GLA_QK), seq_spec(GLA_QK), seq_spec(GLA_W),
                  seq_spec(GLA_W), const(pm.shape), const(lvl.shape), const(gn.shape)],
        out_specs=seq_spec(GLA_W),
        out_shape=jax.ShapeDtypeStruct((batch, seq, GLA_W), BF16),
        scratch_shapes=[pltpu.VMEM((GLA_HEADS // 2, GLA_DV, LANES), F32)],
        compiler_params=pltpu.CompilerParams(
            dimension_semantics=("arbitrary",), vmem_limit_bytes=VMEM_LIMIT),
        name="gla",
    )(r3(gq), r3(gk), r3(la), r3(gv), r3(gg), pm, lvl, gn)


def _t5_bucket_np(dist):
    max_exact = REL_BUCKETS // 2
    d = np.maximum(dist, 1).astype(np.float32)
    large = max_exact + (np.log(d / np.float32(max_exact)) / np.float32(math.log(REL_MAX_DIST / max_exact))
                         * np.float32(REL_BUCKETS - max_exact)).astype(np.int32)
    large = np.minimum(large, REL_BUCKETS - 1)
    return np.where(dist < max_exact, dist, large).astype(np.int32)


def _bucket_table():
    qi = np.arange(BLK)[:, None]
    ki = np.arange(2 * BLK)[None, :]
    j = qi + BLK - ki
    valid = (j >= 0) & (j <= BAND)
    return np.stack([np.where(valid, _t5_bucket_np(np.maximum(j, 0) * d), -1) for d in DILATIONS]).astype(np.int32)


def _bias_kernel(rel_ref, bkt_ref, o_ref):
    pair = pl.program_id(1)
    bkt = bkt_ref[...]
    for e in range(2):
        acc = jnp.full(bkt.shape, NEG, F32)
        for u in range(REL_BUCKETS):
            acc = jnp.where(bkt == u, rel_ref[u, 2 * pair + e] * LOG2E, acc)
        o_ref[e * BLK:(e + 1) * BLK, :] = acc


def _bias_table(rel_bias):
    bkt = jnp.asarray(_bucket_table())
    n_pat = len(DILATIONS)
    return pl.pallas_call(
        _bias_kernel,
        grid=(n_pat, DIL_HEADS // 2),
        in_specs=[pl.BlockSpec(memory_space=pltpu.SMEM),
                  pl.BlockSpec((None, BLK, 2 * BLK), lambda p, h: (p, 0, 0))],
        out_specs=pl.BlockSpec((None, None, 2 * BLK, 2 * BLK), lambda p, h: (p, h, 0, 0)),
        out_shape=jax.ShapeDtypeStruct((n_pat, DIL_HEADS // 2, 2 * BLK, 2 * BLK), F32),
        compiler_params=pltpu.CompilerParams(dimension_semantics=("arbitrary", "arbitrary")),
        name="bias_table",
    )(rel_bias, bkt)


def _attn_kernel(q1, k1, v1, q4, k4, v4, q16, k16, v16, bias_ref, o_ref,
                 s_ref, mb_ref, u_ref, w_ref, m_ref, *, seq):
    lo_half = lax.broadcasted_iota(jnp.int32, (BLK, LANES), 1) < DIL_HD
    refs = ((q1, k1, v1), (q4, k4, v4), (q16, k16, v16))
    nblk = seq // BLK

    def key_rows(g, has_prev):
        return pl.ds((g - 1) * BLK, 2 * BLK) if has_prev else pl.ds(g * BLK, BLK)

    one_trip = jnp.minimum(pl.program_id(0) + 1, 1)

    def region(fn):
        lax.fori_loop(0, one_trip, lambda i, c: (fn(), c)[1], 0)

    def scores(p):
        q_ref, k_ref, _ = refs[p]
        nb = nblk // DILATIONS[p]
        for g in range(nblk):
            has_prev = g % nb > 0
            nk = 2 * BLK if has_prev else BLK
            bias = bias_ref[p] if has_prev else bias_ref[p, :, BLK:]
            q = q_ref[pl.ds(g * BLK, BLK), :]
            s = _dot_nt(_split_heads(q, lo_half), k_ref[key_rows(g, has_prev), :]) + bias
            s_ref[p * nblk + g, :, :nk] = s
            mb_ref[p * nblk + g] = jnp.broadcast_to(jnp.max(s, axis=-1, keepdims=True), (2 * BLK, LANES))

    def outputs(p):
        _, _, v_ref = refs[p]
        d = DILATIONS[p]
        nb = nblk // d
        for g in range(nblk):
            r, n = divmod(g, nb)
            has_prev = n > 0
            nk = 2 * BLK if has_prev else BLK
            mb = mb_ref[p * nblk + g]
            e = jnp.concatenate(
                [jnp.exp2(s_ref[p * nblk + g, :, c * LANES:(c + 1) * LANES] - mb) for c in range(nk // LANES)],
                axis=1).astype(BF16)
            vv = v_ref[key_rows(g, has_prev), :]
            lo_k = lax.broadcasted_iota(jnp.int32, vv.shape, 1) < DIL_HD
            one = jnp.ones_like(vv)
            ta = jnp.dot(e[:BLK], jnp.where(lo_k, vv, one), preferred_element_type=F32)
            tb = jnp.dot(e[BLK:], jnp.where(lo_k, one, vv), preferred_element_type=F32)
            dst = pl.ds(n * (BLK * d) + r, BLK, stride=d) if d > 1 else pl.ds(g * BLK, BLK)
            u_ref[p, dst, :] = jnp.where(lo_half, ta, tb)
            w_ref[p, dst, :] = pltpu.roll(jnp.where(lo_half, tb, ta), DIL_HD, 1)
            m_ref[p, dst, :] = jnp.where(lo_half, mb[:BLK], mb[BLK:])

    for p in range(len(DILATIONS)):
        region(functools.partial(scores, p))
        region(functools.partial(outputs, p))

    chunk = 2 * BLK

    def combine(i, carry):
        rows = pl.ds(pl.multiple_of(i * chunk, chunk), chunk)
        ms = [m_ref[p, rows, :] for p in range(len(DILATIONS))]
        mmax = functools.reduce(jnp.maximum, ms)
        cs = [jnp.exp2(mp - mmax) for mp in ms]
        num = sum(c * u_ref[p, rows, :] for p, c in enumerate(cs))
        den = sum(c * w_ref[p, rows, :] for p, c in enumerate(cs))
        o_ref[rows, :] = (num / den).astype(o_ref.dtype)
        return carry

    lax.fori_loop(0, seq // chunk, combine, 0)


def _dil_attn(a1, a4, a16, bias, *, batch, seq):
    n_pat = len(DILATIONS)
    n_pair = DIL_HEADS // 2
    arrs = [a.reshape(batch, seq, 3 * DIL_W) for a in (a1, a4, a16)]
    lane_blocks = DIL_W // LANES

    def spec(part):
        return pl.BlockSpec((None, seq, LANES), lambda b, h: (b, 0, part * lane_blocks + h))

    in_specs = [spec(part) for _ in range(n_pat) for part in range(3)]
    in_specs.append(pl.BlockSpec((n_pat, None, 2 * BLK, 2 * BLK), lambda b, h: (0, h, 0, 0)))
    args = [a for a in arrs for _ in range(3)] + [bias]
    return pl.pallas_call(
        functools.partial(_attn_kernel, seq=seq),
        grid=(batch, n_pair),
        in_specs=in_specs,
        out_specs=pl.BlockSpec((None, seq, LANES), lambda b, h: (b, 0, h)),
        out_shape=jax.ShapeDtypeStruct((batch, seq, DIL_W), BF16),
        scratch_shapes=[pltpu.VMEM((n_pat * seq // BLK, 2 * BLK, 2 * BLK), F32),
                        pltpu.VMEM((n_pat * seq // BLK, 2 * BLK, LANES), F32)]
        + [pltpu.VMEM((n_pat, seq, LANES), F32) for _ in range(3)],
        compiler_params=pltpu.CompilerParams(
            dimension_semantics=("arbitrary", "arbitrary"), vmem_limit_bytes=VMEM_LIMIT),
        name="dil_attn",
    )(*args)


def _post_kernel(h_ref, og_ref, od_ref, p_ref, wo_ref, w1_ref, w2_ref, wg_ref, wp_ref,
                 g_mix_ref, g_pre_ref, g_post_ref, o_ref, *, ff_chunk):
    mix = (jnp.dot(og_ref[...], wo_ref[:GLA_W, :], preferred_element_type=F32)
           + jnp.dot(od_ref[...], wo_ref[GLA_W:, :], preferred_element_type=F32))
    h1 = h_ref[...] + _rms(mix, g_mix_ref[...])
    xn = _rms(h1, g_pre_ref[...]).astype(BF16)
    f = jnp.zeros_like(h1)
    for c in range(w1_ref.shape[1] // ff_chunk):
        cols = slice(c * ff_chunk, (c + 1) * ff_chunk)
        a = jnp.maximum(jnp.dot(xn, w1_ref[:, cols], preferred_element_type=F32), 0.0)
        f = f + jnp.dot((a * a).astype(BF16), w2_ref[cols, :], preferred_element_type=F32)
    h2 = h1 + _rms(f, g_post_ref[...])
    gate = jnp.dot(h2.astype(BF16), wg_ref[...], preferred_element_type=F32)
    emb = jnp.dot(p_ref[...].astype(BF16), wp_ref[...], preferred_element_type=F32)
    o_ref[...] = h2 + emb / (1.0 + jnp.exp(-gate))


def _post(h, og, od, p, wo, w1, w2, wg, wp, g_mix, g_pre, g_post, *, tm):
    t, dm = h.shape
    row = lambda width: pl.BlockSpec((tm, width), lambda i: (i, 0))
    const = lambda a: pl.BlockSpec(a.shape, lambda i: (0, 0), pipeline_mode=pl.Buffered(1))
    consts = (wo, w1, w2, wg, wp, g_mix, g_pre, g_post)
    return pl.pallas_call(
        functools.partial(_post_kernel, ff_chunk=1024),
        grid=(t // tm,),
        in_specs=[row(dm), row(GLA_W), row(DIL_W), row(p.shape[1])] + [const(a) for a in consts],
        out_specs=row(dm),
        out_shape=jax.ShapeDtypeStruct((t, dm), F32),
        compiler_params=pltpu.CompilerParams(
            dimension_semantics=("arbitrary",), vmem_limit_bytes=VMEM_LIMIT),
        name="post",
    )(h, og, od, p, *consts)


def kernel(x, p, w_in, w_gla_a2, b_gla_a, gla_norm_g, w_out, rel_bias, pre_mix_g, post_mix_g,
           pre_mlp_g, post_mlp_g, w_mlp_in, w_mlp_out, w_ple_gate, w_ple_proj):
    batch, seq, dm = x.shape
    depth = w_in.shape[0]
    assert seq % (max(DILATIONS) * BLK) == 0 and seq % MACRO == 0
    t = batch * seq
    row1 = lambda a: a.reshape(1, -1)
    bias = _bias_table(rel_bias)
    h = x.reshape(t, dm)
    splits = np.cumsum([GLA_QK, GLA_QK, GLA_W, GLA_W, GLA_LOWRANK, DIL_W, DIL_W])
    for i in range(depth):
        gq_w, gk_w, gv_w, gg_w, lr_w, dq_w, dk_w, dv_w = jnp.split(w_in[i], splits, axis=1)
        w = jnp.concatenate(
            [gq_w, gk_w, gv_w, gg_w, dq_w, dk_w, dv_w, lr_w,
             jnp.zeros((dm, C_END - C_LR - GLA_LOWRANK), w_in.dtype)], axis=1).astype(BF16)
        wa2 = jnp.zeros((C_END - C_LR, GLA_QK), BF16).at[:GLA_LOWRANK].set(w_gla_a2[i].astype(BF16))
        gq, gk, la, gv, gg, a1, a4, a16 = _in_proj(
            h, row1(pre_mix_g[i]), w, wa2, row1(b_gla_a[i]), batch=batch, seq=seq, tm=512)
        og = _gla(gq, gk, la, gv, gg, row1(gla_norm_g[i]), batch=batch, seq=seq)
        od = _dil_attn(a1, a4, a16, bias, batch=batch, seq=seq)
        h = _post(h, og.reshape(t, GLA_W), od.reshape(t, DIL_W), p[i].reshape(t, -1),
                  w_out[i].astype(BF16), w_mlp_in[i].astype(BF16), w_mlp_out[i].astype(BF16),
                  w_ple_gate[i].astype(BF16), w_ple_proj[i].astype(BF16),
                  row1(post_mix_g[i]), row1(pre_mlp_g[i]), row1(post_mlp_g[i]), tm=256)
    return h.reshape(batch, seq, dm)
```

```python
import functools
import math

import numpy as np
import jax
import jax.numpy as jnp
from jax import lax
from jax.experimental import pallas as pl
from jax.experimental.pallas import tpu as pltpu

F32 = jnp.float32
BF16 = jnp.bfloat16

EPS = 1e-6
GLA_HEADS = 4
GLA_DK = 64
GLA_DV = 128
GLA_QK = GLA_HEADS * GLA_DK
GLA_W = GLA_HEADS * GLA_DV
GLA_LOWRANK = 16
GLA_TAU = 16.0
DIL_HEADS = 8
DIL_HD = 64
DIL_W = DIL_HEADS * DIL_HD
DILATIONS = (1, 4, 16)
BAND = 128
BLK = 128
REL_BUCKETS = 32
REL_MAX_DIST = 2048
NEG = -1e30
LOG2E = math.log2(math.e)

LANES = 128
MACRO = 128
GLA_LEVELS = 7
VMEM_LIMIT = 56 * 1024 * 1024

C_GQ, C_GK, C_GV, C_GG, C_DQ, C_LR, C_END = 0, 256, 512, 1024, 1536, 3072, 3200


def _rms(x, g):
    return x * lax.rsqrt(jnp.mean(x * x, axis=-1, keepdims=True) + EPS) * g


def _in_proj_kernel(h_ref, g_ref, w_ref, wa2_ref, ba_ref,
                    gq_ref, gk_ref, la_ref, gv_ref, gg_ref, a1_ref, a4_ref, a16_ref,
                    slab_ref, slab4_ref, *, tm):
    xn = _rms(h_ref[...], g_ref[...]).astype(BF16)

    def proj(lo, hi):
        return jnp.dot(xn, w_ref[:, lo:hi], preferred_element_type=F32)

    for c in range(3):
        y = proj(C_DQ + DIL_W * c, C_DQ + DIL_W * (c + 1))
        if c == 0:
            y = y * (LOG2E * DIL_HD ** -0.5)
        for s in range(DIL_W // LANES):
            col = DIL_W * c + LANES * s
            ys = y[:, LANES * s:LANES * (s + 1)]
            a1_ref[0, :, col:col + LANES] = ys.astype(BF16)
            idx = c * (DIL_W // LANES) + s
            slab_ref[idx] = ys
            for r in range(4):
                y4 = slab_ref[idx, pl.ds(r, tm // 4, stride=4), :]
                a4_ref[r, :, col:col + LANES] = y4.astype(BF16)
                slab4_ref[idx, r] = y4
            for r in range(16):
                a16_ref[r, :, col:col + LANES] = slab4_ref[
                    idx, r % 4, pl.ds(r // 4, tm // 16, stride=4), :].astype(BF16)

    gq_ref[...] = proj(C_GQ, C_GK) * (GLA_DK ** -0.5)
    gk_ref[...] = proj(C_GK, C_GV)
    gv_ref[...] = proj(C_GV, C_GG).astype(BF16)
    gg_ref[...] = proj(C_GG, C_DQ).astype(BF16)
    z = jnp.dot(proj(C_LR, C_END).astype(BF16), wa2_ref[...],
                preferred_element_type=F32) + ba_ref[...]
    la_ref[...] = (jnp.minimum(z, 0.0) - jnp.log1p(jnp.exp(-jnp.abs(z)))) * (1.0 / GLA_TAU)


def _in_proj(h, g, w, wa2, ba, *, batch, seq, tm):
    t = batch * seq
    nt = seq // tm
    row = lambda width: pl.BlockSpec((tm, width), lambda b, i: (b * nt + i, 0))
    const = lambda shape: pl.BlockSpec(shape, lambda b, i: (0,) * len(shape))
    dil_spec = lambda d: pl.BlockSpec((None, d, tm // d, 3 * DIL_W), lambda b, i: (b, 0, i, 0))
    out_shape = (
        jax.ShapeDtypeStruct((t, GLA_QK), F32), jax.ShapeDtypeStruct((t, GLA_QK), F32),
        jax.ShapeDtypeStruct((t, GLA_QK), F32),
        jax.ShapeDtypeStruct((t, GLA_W), BF16), jax.ShapeDtypeStruct((t, GLA_W), BF16),
    ) + tuple(jax.ShapeDtypeStruct((batch, d, seq // d, 3 * DIL_W), BF16) for d in DILATIONS)
    return pl.pallas_call(
        functools.partial(_in_proj_kernel, tm=tm),
        grid=(batch, nt),
        in_specs=[row(h.shape[1]), const(g.shape), const(w.shape), const(wa2.shape), const(ba.shape)],
        out_specs=(row(GLA_QK), row(GLA_QK), row(GLA_QK), row(GLA_W), row(GLA_W))
        + tuple(dil_spec(d) for d in DILATIONS),
        out_shape=out_shape,
        scratch_shapes=[pltpu.VMEM((3 * DIL_W // LANES, tm, LANES), F32),
                        pltpu.VMEM((3 * DIL_W // LANES, 4, tm // 4, LANES), F32)],
        compiler_params=pltpu.CompilerParams(
            dimension_semantics=("arbitrary", "arbitrary"), vmem_limit_bytes=VMEM_LIMIT),
        name="in_proj",
    )(h, g, w, wa2, ba)


def _gla_constants():
    n = MACRO
    i = np.arange(n)[:, None]
    t = np.arange(n)[None, :]
    blocks = [t <= i, t > i]
    for l in range(GLA_LEVELS):
        ref = ((i >> (l + 1)) << (l + 1)) + (1 << l) - 1
        odd = ((i >> l) & 1) == 1
        blocks.append(np.where(odd, (t > ref) & (t <= i), (t > i) & (t <= ref)))
    p = np.concatenate(blocks, axis=0).astype(np.float32)
    j = np.arange(n)[None, :]
    lvl = np.where(j < i, np.floor(np.log2(np.maximum(i ^ j, 1))).astype(np.int32),
                   np.where(j == i, GLA_LEVELS, -1)).astype(np.int32)
    return p, np.concatenate([lvl, lvl], axis=0)


def _split_heads(x, lo_half):
    zero = jnp.zeros_like(x)
    return jnp.concatenate([jnp.where(lo_half, x, zero), jnp.where(lo_half, zero, x)], axis=0)


def _dot_nt(a, b):
    return lax.dot_general(a, b, (((1,), (1,)), ((), ())), preferred_element_type=F32)


def _dot_tn(a, b):
    return lax.dot_general(a, b, (((0,), (0,)), ((), ())), preferred_element_type=F32)


def _gla_kernel(q_ref, k_ref, la_ref, v_ref, gg_ref, p_ref, lvl_ref, gn_ref, o_ref, st_ref,
                *, n_macro, unroll):
    st_ref[...] = jnp.zeros_like(st_ref)
    lo_half = lax.broadcasted_iota(jnp.int32, (MACRO, LANES), 1) < (LANES // 2)

    def body(m, carry):
        r0 = pl.multiple_of(m * MACRO, MACRO)
        rows = pl.ds(r0, MACRO)
        q = q_ref[rows, :]
        k = k_ref[rows, :]
        la = la_ref[rows, :]
        la_hi = la.astype(BF16)
        la2 = jnp.concatenate([la_hi, (la - la_hi.astype(F32)).astype(BF16)], axis=1)

        def decay(blk):
            r = jnp.dot(p_ref[blk * MACRO:(blk + 1) * MACRO, :], la2, preferred_element_type=F32)
            return jnp.exp(r[:, :GLA_QK] + r[:, GLA_QK:])

        x_start = decay(0)
        qg = q * x_start
        kg = k * decay(1)
        a_last = x_start[MACRO - 1:MACRO, :]
        lvl = lvl_ref[...]
        xs = [decay(2 + l) for l in range(GLA_LEVELS)]
        for pair in range(GLA_HEADS // 2):
            sl = slice(LANES * pair, LANES * (pair + 1))
            att = _dot_nt(_split_heads(q[:, sl], lo_half).astype(BF16), k[:, sl].astype(BF16))
            att = jnp.where(lvl == GLA_LEVELS, att, 0.0)
            for l in range(GLA_LEVELS):
                xl = xs[l][:, sl]
                a = _dot_nt(_split_heads(q[:, sl] * xl, lo_half).astype(BF16),
                            (k[:, sl] * xl).astype(BF16))
                att = jnp.where(lvl == l, a, att)
            att = att.astype(BF16)
            st = st_ref[pair]
            inter = _dot_nt(_split_heads(qg[:, sl], lo_half).astype(BF16), st.astype(BF16))
            kgp = kg[:, sl].astype(BF16)
            upd = []
            for e in range(2):
                h = 2 * pair + e
                hs = slice(GLA_DV * h, GLA_DV * (h + 1))
                vh = v_ref[rows, hs]
                o = (jnp.dot(att[e * MACRO:(e + 1) * MACRO], vh, preferred_element_type=F32)
                     + inter[e * MACRO:(e + 1) * MACRO])
                gate = gg_ref[rows, hs].astype(F32)
                o = _rms(o, gn_ref[...]) * (gate / (1.0 + jnp.exp(-gate)))
                o_ref[rows, hs] = o.astype(o_ref.dtype)
                upd.append(_dot_tn(vh, kgp))
            st_ref[pair] = st * a_last[:, sl] + jnp.where(lo_half, upd[0], upd[1])
        return carry

    lax.fori_loop(0, n_macro, body, 0, unroll=unroll)


def _gla(gq, gk, la, gv, gg, gn, *, batch, seq, unroll=2):
    p_np, lvl_np = _gla_constants()
    pm = jnp.asarray(p_np, BF16)
    lvl = jnp.asarray(lvl_np)
    seq_spec = lambda width: pl.BlockSpec((None, seq, width), lambda b: (b, 0, 0))
    const = lambda shape: pl.BlockSpec(shape, lambda b: (0,) * len(shape))
    r3 = lambda a: a.reshape(batch, seq, a.shape[-1])
    return pl.pallas_call(
        functools.partial(_gla_kernel, n_macro=seq // MACRO, unroll=unroll),
        grid=(batch,),
        in_specs=[seq_spec(GLA_QK), seq_spec(GLA_QK), seq_spec(GLA_QK), seq_spec(GLA_W),
                  seq_spec(GLA_W), const(pm.shape), const(lvl.shape), const(gn.shape)],
        out_specs=seq_spec(GLA_W),
        out_shape=jax.ShapeDtypeStruct((batch, seq, GLA_W), BF16),
        scratch_shapes=[pltpu.VMEM((GLA_HEADS // 2, GLA_DV, LANES), F32)],
        compiler_params=pltpu.CompilerParams(
            dimension_semantics=("arbitrary",), vmem_limit_bytes=VMEM_LIMIT),
        name="gla",
    )(r3(gq), r3(gk), r3(la), r3(gv), r3(gg), pm, lvl, gn)


def _t5_bucket_np(dist):
    max_exact = REL_BUCKETS // 2
    d = np.maximum(dist, 1).astype(np.float32)
    large = max_exact + (np.log(d / np.float32(max_exact)) / np.float32(math.log(REL_MAX_DIST / max_exact))
                         * np.float32(REL_BUCKETS - max_exact)).astype(np.int32)
    large = np.minimum(large, REL_BUCKETS - 1)
    return np.where(dist < max_exact, dist, large).astype(np.int32)


def _bucket_table():
    qi = np.arange(BLK)[:, None]
    ki = np.arange(2 * BLK)[None, :]
    j = qi + BLK - ki
    valid = (j >= 0) & (j <= BAND)
    return np.stack([np.where(valid, _t5_bucket_np(np.maximum(j, 0) * d), -1) for d in DILATIONS]).astype(np.int32)


def _bias_kernel(rel_ref, bkt_ref, o_ref):
    pair = pl.program_id(1)
    bkt = bkt_ref[...]
    for e in range(2):
        acc = jnp.full(bkt.shape, NEG, F32)
        for u in range(REL_BUCKETS):
            acc = jnp.where(bkt == u, rel_ref[u, 2 * pair + e] * LOG2E, acc)
        o_ref[e * BLK:(e + 1) * BLK, :] = acc


def _bias_table(rel_bias):
    bkt = jnp.asarray(_bucket_table())
    n_pat = len(DILATIONS)
    return pl.pallas_call(
        _bias_kernel,
        grid=(n_pat, DIL_HEADS // 2),
        in_specs=[pl.BlockSpec(memory_space=pltpu.SMEM),
                  pl.BlockSpec((None, BLK, 2 * BLK), lambda p, h: (p, 0, 0))],
        out_specs=pl.BlockSpec((None, None, 2 * BLK, 2 * BLK), lambda p, h: (p, h, 0, 0)),
        out_shape=jax.ShapeDtypeStruct((n_pat, DIL_HEADS // 2, 2 * BLK, 2 * BLK), F32),
        compiler_params=pltpu.CompilerParams(dimension_semantics=("arbitrary", "arbitrary")),
        name="bias_table",
    )(rel_bias, bkt)


def _attn_kernel(q1, k1, v1, q4, k4, v4, q16, k16, v16, bias_ref, o_ref,
                 s_ref, mb_ref, u_ref, w_ref, m_ref, *, seq):
    lo_half = lax.broadcasted_iota(jnp.int32, (BLK, LANES), 1) < DIL_HD
    refs = ((q1, k1, v1), (q4, k4, v4), (q16, k16, v16))
    nblk = seq // BLK

    def key_rows(g, has_prev):
        return pl.ds((g - 1) * BLK, 2 * BLK) if has_prev else pl.ds(g * BLK, BLK)

    one_trip = jnp.minimum(pl.program_id(0) + 1, 1)

    def region(fn):
        lax.fori_loop(0, one_trip, lambda i, c: (fn(), c)[1], 0)

    def scores(p):
        q_ref, k_ref, _ = refs[p]
        nb = nblk // DILATIONS[p]
        for g in range(nblk):
            has_prev = g % nb > 0
            nk = 2 * BLK if has_prev else BLK
            bias = bias_ref[p] if has_prev else bias_ref[p, :, BLK:]
            q = q_ref[pl.ds(g * BLK, BLK), :]
            s = _dot_nt(_split_heads(q, lo_half), k_ref[key_rows(g, has_prev), :]) + bias
            s_ref[p * nblk + g, :, :nk] = s
            mb_ref[p * nblk + g] = jnp.broadcast_to(jnp.max(s, axis=-1, keepdims=True), (2 * BLK, LANES))

    def outputs(p):
        _, _, v_ref = refs[p]
        d = DILATIONS[p]
        nb = nblk // d
        for g in range(nblk):
            r, n = divmod(g, nb)
            has_prev = n > 0
            nk = 2 * BLK if has_prev else BLK
            mb = mb_ref[p * nblk + g]
            e = jnp.concatenate(
                [jnp.exp2(s_ref[p * nblk + g, :, c * LANES:(c + 1) * LANES] - mb) for c in range(nk // LANES)],
                axis=1).astype(BF16)
            vv = v_ref[key_rows(g, has_prev), :]
            lo_k = lax.broadcasted_iota(jnp.int32, vv.shape, 1) < DIL_HD
            one = jnp.ones_like(vv)
            ta = jnp.dot(e[:BLK], jnp.where(lo_k, vv, one), preferred_element_type=F32)
            tb = jnp.dot(e[BLK:], jnp.where(lo_k, one, vv), preferred_element_type=F32)
            dst = pl.ds(n * (BLK * d) + r, BLK, stride=d) if d > 1 else pl.ds(g * BLK, BLK)
            u_ref[p, dst, :] = jnp.where(lo_half, ta, tb)
            w_ref[p, dst, :] = pltpu.roll(jnp.where(lo_half, tb, ta), DIL_HD, 1)
            m_ref[p, dst, :] = jnp.where(lo_half, mb[:BLK], mb[BLK:])

    for p in range(len(DILATIONS)):
        region(functools.partial(scores, p))
        region(functools.partial(outputs, p))

    chunk = 2 * BLK

    def combine(i, carry):
        rows = pl.ds(pl.multiple_of(i * chunk, chunk), chunk)
        ms = [m_ref[p, rows, :] for p in range(len(DILATIONS))]
        mmax = functools.reduce(jnp.maximum, ms)
        cs = [jnp.exp2(mp - mmax) for mp in ms]
        num = sum(c * u_ref[p, rows, :] for p, c in enumerate(cs))
        den = sum(c * w_ref[p, rows, :] for p, c in enumerate(cs))
        o_ref[rows, :] = (num / den).astype(o_ref.dtype)
        return carry

    lax.fori_loop(0, seq // chunk, combine, 0)


def _dil_attn(a1, a4, a16, bias, *, batch, seq):
    n_pat = len(DILATIONS)
    n_pair = DIL_HEADS // 2
    arrs = [a.reshape(batch, seq, 3 * DIL_W) for a in (a1, a4, a16)]
    lane_blocks = DIL_W // LANES

    def spec(part):
        return pl.BlockSpec((None, seq, LANES), lambda b, h: (b, 0, part * lane_blocks + h))

    in_specs = [spec(part) for _ in range(n_pat) for part in range(3)]
    in_specs.append(pl.BlockSpec((n_pat, None, 2 * BLK, 2 * BLK), lambda b, h: (0, h, 0, 0)))
    args = [a for a in arrs for _ in range(3)] + [bias]
    return pl.pallas_call(
        functools.partial(_attn_kernel, seq=seq),
        grid=(batch, n_pair),
        in_specs=in_specs,
        out_specs=pl.BlockSpec((None, seq, LANES), lambda b, h: (b, 0, h)),
        out_shape=jax.ShapeDtypeStruct((batch, seq, DIL_W), BF16),
        scratch_shapes=[pltpu.VMEM((n_pat * seq // BLK, 2 * BLK, 2 * BLK), F32),
                        pltpu.VMEM((n_pat * seq // BLK, 2 * BLK, LANES), F32)]
        + [pltpu.VMEM((n_pat, seq, LANES), F32) for _ in range(3)],
        compiler_params=pltpu.CompilerParams(
            dimension_semantics=("arbitrary", "arbitrary"), vmem_limit_bytes=VMEM_LIMIT),
        name="dil_attn",
    )(*args)


def _post_kernel(h_ref, og_ref, od_ref, p_ref, wo_ref, w1_ref, w2_ref, wg_ref, wp_ref,
                 g_mix_ref, g_pre_ref, g_post_ref, o_ref, *, ff_chunk):
    mix = (jnp.dot(og_ref[...], wo_ref[:GLA_W, :], preferred_element_type=F32)
           + jnp.dot(od_ref[...], wo_ref[GLA_W:, :], preferred_element_type=F32))
    h1 = h_ref[...] + _rms(mix, g_mix_ref[...])
    xn = _rms(h1, g_pre_ref[...]).astype(BF16)
    f = jnp.zeros_like(h1)
    for c in range(w1_ref.shape[1] // ff_chunk):
        cols = slice(c * ff_chunk, (c + 1) * ff_chunk)
        a = jnp.maximum(jnp.dot(xn, w1_ref[:, cols], preferred_element_type=F32), 0.0)
        f = f + jnp.dot((a * a).astype(BF16), w2_ref[cols, :], preferred_element_type=F32)
    h2 = h1 + _rms(f, g_post_ref[...])
    gate = jnp.dot(h2.astype(BF16), wg_ref[...], preferred_element_type=F32)
    emb = jnp.dot(p_ref[...].astype(BF16), wp_ref[...], preferred_element_type=F32)
    o_ref[...] = h2 + emb / (1.0 + jnp.exp(-gate))


def _post(h, og, od, p, wo, w1, w2, wg, wp, g_mix, g_pre, g_post, *, tm, ff_chunk=1024):
    t, dm = h.shape
    row = lambda width: pl.BlockSpec((tm, width), lambda i: (i, 0))
    const = lambda a: pl.BlockSpec(a.shape, lambda i: (0, 0), pipeline_mode=pl.Buffered(1))
    consts = (wo, w1, w2, wg, wp, g_mix, g_pre, g_post)
    return pl.pallas_call(
        functools.partial(_post_kernel, ff_chunk=ff_chunk),
        grid=(t // tm,),
        in_specs=[row(dm), row(GLA_W), row(DIL_W), row(p.shape[1])] + [const(a) for a in consts],
        out_specs=row(dm),
        out_shape=jax.ShapeDtypeStruct((t, dm), F32),
        compiler_params=pltpu.CompilerParams(
            dimension_semantics=("arbitrary",), vmem_limit_bytes=VMEM_LIMIT),
        name="post",
    )(h, og, od, p, *consts)


def kernel(x, p, w_in, w_gla_a2, b_gla_a, gla_norm_g, w_out, rel_bias, pre_mix_g, post_mix_g,
           pre_mlp_g, post_mlp_g, w_mlp_in, w_mlp_out, w_ple_gate, w_ple_proj):
    batch, seq, dm = x.shape
    depth = w_in.shape[0]
    assert seq % (max(DILATIONS) * BLK) == 0 and seq % MACRO == 0
    t = batch * seq
    row1 = lambda a: a.reshape(1, -1)
    bias = _bias_table(rel_bias)
    h = x.reshape(t, dm)
    splits = np.cumsum([GLA_QK, GLA_QK, GLA_W, GLA_W, GLA_LOWRANK, DIL_W, DIL_W])
    for i in range(depth):
        gq_w, gk_w, gv_w, gg_w, lr_w, dq_w, dk_w, dv_w = jnp.split(w_in[i], splits, axis=1)
        w = jnp.concatenate(
            [gq_w, gk_w, gv_w, gg_w, dq_w, dk_w, dv_w, lr_w,
             jnp.zeros((dm, C_END - C_LR - GLA_LOWRANK), w_in.dtype)], axis=1).astype(BF16)
        wa2 = jnp.zeros((C_END - C_LR, GLA_QK), BF16).at[:GLA_LOWRANK].set(w_gla_a2[i].astype(BF16))
        gq, gk, la, gv, gg, a1, a4, a16 = _in_proj(
            h, row1(pre_mix_g[i]), w, wa2, row1(b_gla_a[i]), batch=batch, seq=seq, tm=512)
        og = _gla(gq, gk, la, gv, gg, row1(gla_norm_g[i]), batch=batch, seq=seq)
        od = _dil_attn(a1, a4, a16, bias, batch=batch, seq=seq)
        h = _post(h, og.reshape(t, GLA_W), od.reshape(t, DIL_W), p[i].reshape(t, -1),
                  w_out[i].astype(BF16), w_mlp_in[i].astype(BF16), w_mlp_out[i].astype(BF16),
                  w_ple_gate[i].astype(BF16), w_ple_proj[i].astype(BF16),
                  row1(post_mix_g[i]), row1(pre_mlp_g[i]), row1(post_mlp_g[i]), tm=512)
    return h.reshape(batch, seq, dm)
```

```python
import functools
import math

import numpy as np
import jax
import jax.numpy as jnp
from jax import lax
from jax.experimental import pallas as pl
from jax.experimental.pallas import tpu as pltpu

F32 = jnp.float32
BF16 = jnp.bfloat16

EPS = 1e-6
GLA_HEADS = 4
GLA_DK = 64
GLA_DV = 128
GLA_QK = GLA_HEADS * GLA_DK
GLA_W = GLA_HEADS * GLA_DV
GLA_LOWRANK = 16
GLA_TAU = 16.0
DIL_HEADS = 8
DIL_HD = 64
DIL_W = DIL_HEADS * DIL_HD
DILATIONS = (1, 4, 16)
BAND = 128
BLK = 128
REL_BUCKETS = 32
REL_MAX_DIST = 2048
NEG = -1e30
LOG2E = math.log2(math.e)

LANES = 128
MACRO = 128
GLA_LEVELS = 7
VMEM_LIMIT = 56 * 1024 * 1024

C_GQ, C_GK, C_GV, C_GG, C_DQ, C_LR, C_END = 0, 256, 512, 1024, 1536, 3072, 3200


def _rms(x, g):
    return x * lax.rsqrt(jnp.mean(x * x, axis=-1, keepdims=True) + EPS) * g


def _in_proj_kernel(h_ref, g_ref, w_ref, wa2_ref, ba_ref,
                    gq_ref, gk_ref, la_ref, gv_ref, gg_ref, a1_ref, a4_ref, a16_ref,
                    slab_ref, slab4_ref, *, tm):
    xn = _rms(h_ref[...], g_ref[...]).astype(BF16)

    def proj(lo, hi):
        return jnp.dot(xn, w_ref[:, lo:hi], preferred_element_type=F32)

    for c in range(3):
        y = proj(C_DQ + DIL_W * c, C_DQ + DIL_W * (c + 1))
        if c == 0:
            y = y * (LOG2E * DIL_HD ** -0.5)
        for s in range(DIL_W // LANES):
            col = DIL_W * c + LANES * s
            ys = y[:, LANES * s:LANES * (s + 1)]
            a1_ref[0, :, col:col + LANES] = ys.astype(BF16)
            idx = c * (DIL_W // LANES) + s
            slab_ref[idx] = ys
            for r in range(4):
                y4 = slab_ref[idx, pl.ds(r, tm // 4, stride=4), :]
                a4_ref[r, :, col:col + LANES] = y4.astype(BF16)
                slab4_ref[idx, r] = y4
            for r in range(16):
                a16_ref[r, :, col:col + LANES] = slab4_ref[
                    idx, r % 4, pl.ds(r // 4, tm // 16, stride=4), :].astype(BF16)

    gq_ref[...] = proj(C_GQ, C_GK) * (GLA_DK ** -0.5)
    gk_ref[...] = proj(C_GK, C_GV)
    gv_ref[...] = proj(C_GV, C_GG).astype(BF16)
    gg_ref[...] = proj(C_GG, C_DQ).astype(BF16)
    z = jnp.dot(proj(C_LR, C_END).astype(BF16), wa2_ref[...],
                preferred_element_type=F32) + ba_ref[...]
    la_ref[...] = (jnp.minimum(z, 0.0) - jnp.log1p(jnp.exp(-jnp.abs(z)))) * (1.0 / GLA_TAU)


def _in_proj(h, g, w, wa2, ba, *, layer, batch, seq, tm):
    t = batch * seq
    nt = seq // tm
    row = lambda width: pl.BlockSpec((tm, width), lambda b, i: (b * nt + i, 0))
    const = lambda shape: pl.BlockSpec((None,) + shape[1:], lambda b, i: (layer, 0, 0))
    dil_spec = lambda d: pl.BlockSpec((None, d, tm // d, 3 * DIL_W), lambda b, i: (b, 0, i, 0))
    out_shape = (
        jax.ShapeDtypeStruct((t, GLA_QK), F32), jax.ShapeDtypeStruct((t, GLA_QK), F32),
        jax.ShapeDtypeStruct((t, GLA_QK), F32),
        jax.ShapeDtypeStruct((t, GLA_W), BF16), jax.ShapeDtypeStruct((t, GLA_W), BF16),
    ) + tuple(jax.ShapeDtypeStruct((batch, d, seq // d, 3 * DIL_W), BF16) for d in DILATIONS)
    return pl.pallas_call(
        functools.partial(_in_proj_kernel, tm=tm),
        grid=(batch, nt),
        in_specs=[row(h.shape[1]), const(g.shape), const(w.shape), const(wa2.shape), const(ba.shape)],
        out_specs=(row(GLA_QK), row(GLA_QK), row(GLA_QK), row(GLA_W), row(GLA_W))
        + tuple(dil_spec(d) for d in DILATIONS),
        out_shape=out_shape,
        scratch_shapes=[pltpu.VMEM((3 * DIL_W // LANES, tm, LANES), F32),
                        pltpu.VMEM((3 * DIL_W // LANES, 4, tm // 4, LANES), F32)],
        compiler_params=pltpu.CompilerParams(
            dimension_semantics=("arbitrary", "arbitrary"), vmem_limit_bytes=VMEM_LIMIT),
        name="in_proj",
    )(h, g, w, wa2, ba)


def _gla_constants():
    n = MACRO
    i = np.arange(n)[:, None]
    t = np.arange(n)[None, :]
    blocks = [t <= i, t > i]
    for l in range(GLA_LEVELS):
        ref = ((i >> (l + 1)) << (l + 1)) + (1 << l) - 1
        odd = ((i >> l) & 1) == 1
        blocks.append(np.where(odd, (t > ref) & (t <= i), (t > i) & (t <= ref)))
    p = np.concatenate(blocks, axis=0).astype(np.float32)
    j = np.arange(n)[None, :]
    lvl = np.where(j < i, np.floor(np.log2(np.maximum(i ^ j, 1))).astype(np.int32),
                   np.where(j == i, GLA_LEVELS, -1)).astype(np.int32)
    return p, np.concatenate([lvl, lvl], axis=0)


def _split_heads(x, lo_half):
    zero = jnp.zeros_like(x)
    return jnp.concatenate([jnp.where(lo_half, x, zero), jnp.where(lo_half, zero, x)], axis=0)


def _dot_nt(a, b):
    return lax.dot_general(a, b, (((1,), (1,)), ((), ())), preferred_element_type=F32)


def _dot_tn(a, b):
    return lax.dot_general(a, b, (((0,), (0,)), ((), ())), preferred_element_type=F32)


def _gla_kernel(q_ref, k_ref, la_ref, v_ref, gg_ref, p_ref, lvl_ref, gn_ref, o_ref, st_ref,
                *, n_macro, unroll):
    st_ref[...] = jnp.zeros_like(st_ref)
    lo_half = lax.broadcasted_iota(jnp.int32, (MACRO, LANES), 1) < (LANES // 2)

    def body(m, carry):
        r0 = pl.multiple_of(m * MACRO, MACRO)
        rows = pl.ds(r0, MACRO)
        q = q_ref[rows, :]
        k = k_ref[rows, :]
        la = la_ref[rows, :]
        la_hi = la.astype(BF16)
        la2 = jnp.concatenate([la_hi, (la - la_hi.astype(F32)).astype(BF16)], axis=1)

        def decay(blk):
            r = jnp.dot(p_ref[blk * MACRO:(blk + 1) * MACRO, :], la2, preferred_element_type=F32)
            return jnp.exp(r[:, :GLA_QK] + r[:, GLA_QK:])

        x_start = decay(0)
        qg = q * x_start
        kg = k * decay(1)
        a_last = x_start[MACRO - 1:MACRO, :]
        lvl = lvl_ref[...]
        xs = [decay(2 + l) for l in range(GLA_LEVELS)]
        for pair in range(GLA_HEADS // 2):
            sl = slice(LANES * pair, LANES * (pair + 1))
            att = _dot_nt(_split_heads(q[:, sl], lo_half).astype(BF16), k[:, sl].astype(BF16))
            att = jnp.where(lvl == GLA_LEVELS, att, 0.0)
            for l in range(GLA_LEVELS):
                xl = xs[l][:, sl]
                a = _dot_nt(_split_heads(q[:, sl] * xl, lo_half).astype(BF16),
                            (k[:, sl] * xl).astype(BF16))
                att = jnp.where(lvl == l, a, att)
            att = att.astype(BF16)
            st = st_ref[pair]
            inter = _dot_nt(_split_heads(qg[:, sl], lo_half).astype(BF16), st.astype(BF16))
            kgp = kg[:, sl].astype(BF16)
            upd = []
            for e in range(2):
                h = 2 * pair + e
                hs = slice(GLA_DV * h, GLA_DV * (h + 1))
                vh = v_ref[rows, hs]
                o = (jnp.dot(att[e * MACRO:(e + 1) * MACRO], vh, preferred_element_type=F32)
                     + inter[e * MACRO:(e + 1) * MACRO])
                gate = gg_ref[rows, hs].astype(F32)
                o = _rms(o, gn_ref[...]) * (gate / (1.0 + jnp.exp(-gate)))
                o_ref[rows, hs] = o.astype(o_ref.dtype)
                upd.append(_dot_tn(vh, kgp))
            st_ref[pair] = st * a_last[:, sl] + jnp.where(lo_half, upd[0], upd[1])
        return carry

    lax.fori_loop(0, n_macro, body, 0, unroll=unroll)


def _gla(gq, gk, la, gv, gg, gn, *, batch, seq, unroll=2):
    p_np, lvl_np = _gla_constants()
    pm = jnp.asarray(p_np, BF16)
    lvl = jnp.asarray(lvl_np)
    seq_spec = lambda width: pl.BlockSpec((None, seq, width), lambda b: (b, 0, 0))
    const = lambda shape: pl.BlockSpec(shape, lambda b: (0,) * len(shape))
    r3 = lambda a: a.reshape(batch, seq, a.shape[-1])
    return pl.pallas_call(
        functools.partial(_gla_kernel, n_macro=seq // MACRO, unroll=unroll),
        grid=(batch,),
        in_specs=[seq_spec(GLA_QK), seq_spec(GLA_QK), seq_spec(GLA_QK), seq_spec(GLA_W),
                  seq_spec(GLA_W), const(pm.shape), const(lvl.shape), const(gn.shape)],
        out_specs=seq_spec(GLA_W),
        out_shape=jax.ShapeDtypeStruct((batch, seq, GLA_W), BF16),
        scratch_shapes=[pltpu.VMEM((GLA_HEADS // 2, GLA_DV, LANES), F32)],
        compiler_params=pltpu.CompilerParams(
            dimension_semantics=("arbitrary",), vmem_limit_bytes=VMEM_LIMIT),
        name="gla",
    )(r3(gq), r3(gk), r3(la), r3(gv), r3(gg), pm, lvl, gn)


def _t5_bucket_np(dist):
    max_exact = REL_BUCKETS // 2
    d = np.maximum(dist, 1).astype(np.float32)
    large = max_exact + (np.log(d / np.float32(max_exact)) / np.float32(math.log(REL_MAX_DIST / max_exact))
                         * np.float32(REL_BUCKETS - max_exact)).astype(np.int32)
    large = np.minimum(large, REL_BUCKETS - 1)
    return np.where(dist < max_exact, dist, large).astype(np.int32)


def _bucket_table():
    qi = np.arange(BLK)[:, None]
    ki = np.arange(2 * BLK)[None, :]
    j = qi + BLK - ki
    valid = (j >= 0) & (j <= BAND)
    return np.stack([np.where(valid, _t5_bucket_np(np.maximum(j, 0) * d), -1) for d in DILATIONS]).astype(np.int32)


def _bias_kernel(rel_ref, bkt_ref, o_ref):
    pair = pl.program_id(1)
    bkt = bkt_ref[...]
    for e in range(2):
        acc = jnp.full(bkt.shape, NEG, F32)
        for u in range(REL_BUCKETS):
            acc = jnp.where(bkt == u, rel_ref[u, 2 * pair + e] * LOG2E, acc)
        o_ref[e * BLK:(e + 1) * BLK, :] = acc


def _bias_table(rel_bias):
    bkt = jnp.asarray(_bucket_table())
    n_pat = len(DILATIONS)
    return pl.pallas_call(
        _bias_kernel,
        grid=(n_pat, DIL_HEADS // 2),
        in_specs=[pl.BlockSpec(memory_space=pltpu.SMEM),
                  pl.BlockSpec((None, BLK, 2 * BLK), lambda p, h: (p, 0, 0))],
        out_specs=pl.BlockSpec((None, None, 2 * BLK, 2 * BLK), lambda p, h: (p, h, 0, 0)),
        out_shape=jax.ShapeDtypeStruct((n_pat, DIL_HEADS // 2, 2 * BLK, 2 * BLK), F32),
        compiler_params=pltpu.CompilerParams(dimension_semantics=("arbitrary", "arbitrary")),
        name="bias_table",
    )(rel_bias, bkt)


def _attn_kernel(q1, k1, v1, q4, k4, v4, q16, k16, v16, bias_ref, o_ref,
                 s_ref, mb_ref, u_ref, w_ref, m_ref, *, seq):
    lo_half = lax.broadcasted_iota(jnp.int32, (BLK, LANES), 1) < DIL_HD
    refs = ((q1, k1, v1), (q4, k4, v4), (q16, k16, v16))
    nblk = seq // BLK

    def key_rows(g, has_prev):
        return pl.ds((g - 1) * BLK, 2 * BLK) if has_prev else pl.ds(g * BLK, BLK)

    one_trip = jnp.minimum(pl.program_id(0) + 1, 1)

    def region(fn):
        lax.fori_loop(0, one_trip, lambda i, c: (fn(), c)[1], 0)

    def scores(p):
        q_ref, k_ref, _ = refs[p]
        nb = nblk // DILATIONS[p]
        for g in range(nblk):
            has_prev = g % nb > 0
            nk = 2 * BLK if has_prev else BLK
            bias = bias_ref[p] if has_prev else bias_ref[p, :, BLK:]
            q = q_ref[pl.ds(g * BLK, BLK), :]
            s = _dot_nt(_split_heads(q, lo_half), k_ref[key_rows(g, has_prev), :]) + bias
            s_ref[p * nblk + g, :, :nk] = s
            mb_ref[p * nblk + g] = jnp.broadcast_to(jnp.max(s, axis=-1, keepdims=True), (2 * BLK, LANES))

    def outputs(p):
        _, _, v_ref = refs[p]
        d = DILATIONS[p]
        nb = nblk // d
        for g in range(nblk):
            r, n = divmod(g, nb)
            has_prev = n > 0
            nk = 2 * BLK if has_prev else BLK
            mb = mb_ref[p * nblk + g]
            e = jnp.concatenate(
                [jnp.exp2(s_ref[p * nblk + g, :, c * LANES:(c + 1) * LANES] - mb) for c in range(nk // LANES)],
                axis=1).astype(BF16)
            vv = v_ref[key_rows(g, has_prev), :]
            lo_k = lax.broadcasted_iota(jnp.int32, vv.shape, 1) < DIL_HD
            one = jnp.ones_like(vv)
            ta = jnp.dot(e[:BLK], jnp.where(lo_k, vv, one), preferred_element_type=F32)
            tb = jnp.dot(e[BLK:], jnp.where(lo_k, one, vv), preferred_element_type=F32)
            dst = pl.ds(n * (BLK * d) + r, BLK, stride=d) if d > 1 else pl.ds(g * BLK, BLK)
            u_ref[p, dst, :] = jnp.where(lo_half, ta, tb)
            w_ref[p, dst, :] = pltpu.roll(jnp.where(lo_half, tb, ta), DIL_HD, 1)
            m_ref[p, dst, :] = jnp.where(lo_half, mb[:BLK], mb[BLK:])

    for p in range(len(DILATIONS)):
        region(functools.partial(scores, p))
        region(functools.partial(outputs, p))

    chunk = 2 * BLK

    def combine(i, carry):
        rows = pl.ds(pl.multiple_of(i * chunk, chunk), chunk)
        ms = [m_ref[p, rows, :] for p in range(len(DILATIONS))]
        mmax = functools.reduce(jnp.maximum, ms)
        cs = [jnp.exp2(mp - mmax) for mp in ms]
        num = sum(c * u_ref[p, rows, :] for p, c in enumerate(cs))
        den = sum(c * w_ref[p, rows, :] for p, c in enumerate(cs))
        o_ref[rows, :] = (num / den).astype(o_ref.dtype)
        return carry

    lax.fori_loop(0, seq // chunk, combine, 0)


def _dil_attn(a1, a4, a16, bias, *, batch, seq):
    n_pat = len(DILATIONS)
    n_pair = DIL_HEADS // 2
    arrs = [a.reshape(batch, seq, 3 * DIL_W) for a in (a1, a4, a16)]
    lane_blocks = DIL_W // LANES

    def spec(part):
        return pl.BlockSpec((None, seq, LANES), lambda b, h: (b, 0, part * lane_blocks + h))

    in_specs = [spec(part) for _ in range(n_pat) for part in range(3)]
    in_specs.append(pl.BlockSpec((n_pat, None, 2 * BLK, 2 * BLK), lambda b, h: (0, h, 0, 0)))
    args = [a for a in arrs for _ in range(3)] + [bias]
    return pl.pallas_call(
        functools.partial(_attn_kernel, seq=seq),
        grid=(batch, n_pair),
        in_specs=in_specs,
        out_specs=pl.BlockSpec((None, seq, LANES), lambda b, h: (b, 0, h)),
        out_shape=jax.ShapeDtypeStruct((batch, seq, DIL_W), BF16),
        scratch_shapes=[pltpu.VMEM((n_pat * seq // BLK, 2 * BLK, 2 * BLK), F32),
                        pltpu.VMEM((n_pat * seq // BLK, 2 * BLK, LANES), F32)]
        + [pltpu.VMEM((n_pat, seq, LANES), F32) for _ in range(3)],
        compiler_params=pltpu.CompilerParams(
            dimension_semantics=("arbitrary", "arbitrary"), vmem_limit_bytes=VMEM_LIMIT),
        name="dil_attn",
    )(*args)


def _post_kernel(h_ref, og_ref, od_ref, p_ref, wo_ref, w1_ref, w2_ref, wg_ref, wp_ref,
                 g_mix_ref, g_pre_ref, g_post_ref, o_ref, *, ff_chunk):
    mix = (jnp.dot(og_ref[...], wo_ref[:GLA_W, :], preferred_element_type=F32)
           + jnp.dot(od_ref[...], wo_ref[GLA_W:, :], preferred_element_type=F32))
    h1 = h_ref[...] + _rms(mix, g_mix_ref[...])
    xn = _rms(h1, g_pre_ref[...]).astype(BF16)
    f = jnp.zeros_like(h1)
    for c in range(w1_ref.shape[1] // ff_chunk):
        cols = slice(c * ff_chunk, (c + 1) * ff_chunk)
        a = jnp.maximum(jnp.dot(xn, w1_ref[:, cols], preferred_element_type=F32), 0.0)
        f = f + jnp.dot((a * a).astype(BF16), w2_ref[cols, :], preferred_element_type=F32)
    h2 = h1 + _rms(f, g_post_ref[...])
    gate = jnp.dot(h2.astype(BF16), wg_ref[...], preferred_element_type=F32)
    emb = jnp.dot(p_ref[...].astype(BF16), wp_ref[...], preferred_element_type=F32)
    o_ref[...] = h2 + emb / (1.0 + jnp.exp(-gate))


def _post(h, og, od, p, wo, w1, w2, wg, wp, g_mix, g_pre, g_post, *, layer, tm, ff_chunk=1024):
    t, dm = h.shape
    row = lambda width: pl.BlockSpec((tm, width), lambda i: (i, 0))
    const = lambda a: pl.BlockSpec((None,) + a.shape[1:], lambda i: (layer, 0, 0),
                                   pipeline_mode=pl.Buffered(1))
    consts = (wo, w1, w2, wg, wp, g_mix, g_pre, g_post)
    return pl.pallas_call(
        functools.partial(_post_kernel, ff_chunk=ff_chunk),
        grid=(t // tm,),
        in_specs=[row(dm), row(GLA_W), row(DIL_W),
                  pl.BlockSpec((None, tm, p.shape[2]), lambda i: (layer, i, 0))]
        + [const(a) for a in consts],
        out_specs=row(dm),
        out_shape=jax.ShapeDtypeStruct((t, dm), F32),
        compiler_params=pltpu.CompilerParams(
            dimension_semantics=("arbitrary",), vmem_limit_bytes=VMEM_LIMIT),
        name="post",
    )(h, og, od, p, *consts)


def kernel(x, p, w_in, w_gla_a2, b_gla_a, gla_norm_g, w_out, rel_bias, pre_mix_g, post_mix_g,
           pre_mlp_g, post_mlp_g, w_mlp_in, w_mlp_out, w_ple_gate, w_ple_proj):
    batch, seq, dm = x.shape
    depth = w_in.shape[0]
    assert seq % (max(DILATIONS) * BLK) == 0 and seq % MACRO == 0
    t = batch * seq
    rows1 = lambda a: a.reshape(depth, 1, -1)
    bias = _bias_table(rel_bias)
    h = x.reshape(t, dm)
    lr0 = GLA_QK * 2 + GLA_W * 2
    lr1 = lr0 + GLA_LOWRANK
    w = jnp.concatenate(
        [w_in[:, :, :lr0], w_in[:, :, lr1:], w_in[:, :, lr0:lr1],
         jnp.zeros((depth, dm, C_END - C_LR - GLA_LOWRANK), w_in.dtype)], axis=2).astype(BF16)
    wa2 = jnp.pad(w_gla_a2, ((0, 0), (0, C_END - C_LR - GLA_LOWRANK), (0, 0))).astype(BF16)
    wo, w1, w2, wg, wp = (a.astype(BF16) for a in (w_out, w_mlp_in, w_mlp_out, w_ple_gate, w_ple_proj))
    p2 = p.reshape(depth, t, -1)
    for i in range(depth):
        gq, gk, la, gv, gg, a1, a4, a16 = _in_proj(
            h, rows1(pre_mix_g), w, wa2, rows1(b_gla_a), layer=i, batch=batch, seq=seq, tm=512)
        og = _gla(gq, gk, la, gv, gg, gla_norm_g[i].reshape(1, -1), batch=batch, seq=seq)
        od = _dil_attn(a1, a4, a16, bias, batch=batch, seq=seq)
        h = _post(h, og.reshape(t, GLA_W), od.reshape(t, DIL_W), p2, wo, w1, w2, wg, wp,
                  rows1(post_mix_g), rows1(pre_mlp_g), rows1(post_mlp_g), layer=i, tm=512)
    return h.reshape(batch, seq, dm)
```

```python
import functools
import math

import numpy as np
import jax
import jax.numpy as jnp
from jax import lax
from jax.experimental import pallas as pl
from jax.experimental.pallas import tpu as pltpu

F32 = jnp.float32
BF16 = jnp.bfloat16

EPS = 1e-6
GLA_HEADS = 4
GLA_DK = 64
GLA_DV = 128
GLA_QK = GLA_HEADS * GLA_DK
GLA_W = GLA_HEADS * GLA_DV
GLA_LOWRANK = 16
GLA_TAU = 16.0
DIL_HEADS = 8
DIL_HD = 64
DIL_W = DIL_HEADS * DIL_HD
DILATIONS = (1, 4, 16)
BAND = 128
BLK = 128
REL_BUCKETS = 32
REL_MAX_DIST = 2048
NEG = -1e30
LOG2E = math.log2(math.e)

LANES = 128
MACRO = 128
GLA_LEVELS = 7
VMEM_LIMIT = 56 * 1024 * 1024

C_GQ, C_GK, C_GV, C_GG, C_DQ, C_LR, C_END = 0, 256, 512, 1024, 1536, 3072, 3200


def _rms(x, g):
    return x * lax.rsqrt(jnp.mean(x * x, axis=-1, keepdims=True) + EPS) * g


def _w_in_kernel(w_ref, o_ref, *, lr0):
    lr1 = lr0 + GLA_LOWRANK
    rows = w_ref.shape[0]
    o_ref[:, :lr0] = w_ref[:, :lr0].astype(BF16)
    o_ref[:, lr0:C_LR] = w_ref[:, lr1:].astype(BF16)
    o_ref[:, C_LR:] = jnp.concatenate(
        [w_ref[:, lr0:lr1], jnp.zeros((rows, C_END - C_LR - GLA_LOWRANK), F32)], axis=1).astype(BF16)


def _prep_w_in(w_in, *, tr=256):
    depth, dm, width = w_in.shape
    lr0 = C_DQ
    assert width - GLA_LOWRANK == C_LR
    return pl.pallas_call(
        functools.partial(_w_in_kernel, lr0=lr0),
        grid=(depth, dm // tr),
        in_specs=[pl.BlockSpec((None, tr, width), lambda l, i: (l, i, 0))],
        out_specs=pl.BlockSpec((None, tr, C_END), lambda l, i: (l, i, 0)),
        out_shape=jax.ShapeDtypeStruct((depth, dm, C_END), BF16),
        compiler_params=pltpu.CompilerParams(dimension_semantics=("arbitrary", "arbitrary")),
        name="prep_w_in",
    )(w_in)


def _in_proj_kernel(h_ref, g_ref, w_ref, wa2_ref, ba_ref,
                    gq_ref, gk_ref, la_ref, gv_ref, gg_ref, a1_ref, a4_ref, a16_ref,
                    slab_ref, slab4_ref, *, tm):
    xn = _rms(h_ref[...], g_ref[...]).astype(BF16)

    def proj(lo, hi):
        return jnp.dot(xn, w_ref[:, lo:hi], preferred_element_type=F32)

    for c in range(3):
        y = proj(C_DQ + DIL_W * c, C_DQ + DIL_W * (c + 1))
        if c == 0:
            y = y * (LOG2E * DIL_HD ** -0.5)
        for s in range(DIL_W // LANES):
            col = DIL_W * c + LANES * s
            ys = y[:, LANES * s:LANES * (s + 1)]
            a1_ref[0, :, col:col + LANES] = ys.astype(BF16)
            idx = c * (DIL_W // LANES) + s
            slab_ref[idx] = ys
            for r in range(4):
                y4 = slab_ref[idx, pl.ds(r, tm // 4, stride=4), :]
                a4_ref[r, :, col:col + LANES] = y4.astype(BF16)
                slab4_ref[idx, r] = y4
            for r in range(16):
                a16_ref[r, :, col:col + LANES] = slab4_ref[
                    idx, r % 4, pl.ds(r // 4, tm // 16, stride=4), :].astype(BF16)

    gq_ref[...] = (proj(C_GQ, C_GK) * (GLA_DK ** -0.5)).astype(BF16)
    gk_ref[...] = proj(C_GK, C_GV).astype(BF16)
    gv_ref[...] = proj(C_GV, C_GG).astype(BF16)
    gg_ref[...] = proj(C_GG, C_DQ).astype(BF16)
    z = jnp.dot(proj(C_LR, C_END).astype(BF16), wa2_ref[...],
                preferred_element_type=F32) + ba_ref[...]
    la_ref[...] = (jnp.minimum(z, 0.0) - jnp.log1p(jnp.exp(-jnp.abs(z)))) * (1.0 / GLA_TAU)


def _in_proj(h, g, w, wa2, ba, *, layer, batch, seq, tm):
    t = batch * seq
    nt = seq // tm
    row = lambda width: pl.BlockSpec((tm, width), lambda b, i: (b * nt + i, 0))
    const = lambda shape: pl.BlockSpec((None,) + shape[1:], lambda b, i: (layer, 0, 0))
    dil_spec = lambda d: pl.BlockSpec((None, d, tm // d, 3 * DIL_W), lambda b, i: (b, 0, i, 0))
    out_shape = (
        jax.ShapeDtypeStruct((t, GLA_QK), BF16), jax.ShapeDtypeStruct((t, GLA_QK), BF16),
        jax.ShapeDtypeStruct((t, GLA_QK), F32),
        jax.ShapeDtypeStruct((t, GLA_W), BF16), jax.ShapeDtypeStruct((t, GLA_W), BF16),
    ) + tuple(jax.ShapeDtypeStruct((batch, d, seq // d, 3 * DIL_W), BF16) for d in DILATIONS)
    return pl.pallas_call(
        functools.partial(_in_proj_kernel, tm=tm),
        grid=(batch, nt),
        in_specs=[row(h.shape[1]), const(g.shape), const(w.shape), const(wa2.shape), const(ba.shape)],
        out_specs=(row(GLA_QK), row(GLA_QK), row(GLA_QK), row(GLA_W), row(GLA_W))
        + tuple(dil_spec(d) for d in DILATIONS),
        out_shape=out_shape,
        scratch_shapes=[pltpu.VMEM((3 * DIL_W // LANES, tm, LANES), F32),
                        pltpu.VMEM((3 * DIL_W // LANES, 4, tm // 4, LANES), F32)],
        compiler_params=pltpu.CompilerParams(
            dimension_semantics=("arbitrary", "arbitrary"), vmem_limit_bytes=VMEM_LIMIT),
        name="in_proj",
    )(h, g, w, wa2, ba)


def _gla_constants():
    n = MACRO
    i = np.arange(n)[:, None]
    t = np.arange(n)[None, :]
    blocks = [t <= i, t > i]
    for l in range(GLA_LEVELS):
        ref = ((i >> (l + 1)) << (l + 1)) + (1 << l) - 1
        odd = ((i >> l) & 1) == 1
        blocks.append(np.where(odd, (t > ref) & (t <= i), (t > i) & (t <= ref)))
    p = np.concatenate(blocks, axis=0).astype(np.float32)
    j = np.arange(n)[None, :]
    lvl = np.where(j < i, np.floor(np.log2(np.maximum(i ^ j, 1))).astype(np.int32),
                   np.where(j == i, GLA_LEVELS, -1)).astype(np.int32)
    return p, np.concatenate([lvl, lvl], axis=0)


def _split_heads(x, lo_half):
    zero = jnp.zeros_like(x)
    return jnp.concatenate([jnp.where(lo_half, x, zero), jnp.where(lo_half, zero, x)], axis=0)


def _dot_nt(a, b):
    return lax.dot_general(a, b, (((1,), (1,)), ((), ())), preferred_element_type=F32)


def _dot_tn(a, b):
    return lax.dot_general(a, b, (((0,), (0,)), ((), ())), preferred_element_type=F32)


def _gla_kernel(q_ref, k_ref, la_ref, v_ref, gg_ref, p_ref, lvl_ref, gn_ref, o_ref, st_ref,
                *, n_macro, unroll):
    st_ref[...] = jnp.zeros_like(st_ref)
    lo_half = lax.broadcasted_iota(jnp.int32, (MACRO, LANES), 1) < (LANES // 2)

    def body(m, carry):
        r0 = pl.multiple_of(m * MACRO, MACRO)
        rows = pl.ds(r0, MACRO)
        q = q_ref[rows, :].astype(F32)
        k = k_ref[rows, :].astype(F32)
        la = la_ref[rows, :]
        la_hi = la.astype(BF16)
        la2 = jnp.concatenate([la_hi, (la - la_hi.astype(F32)).astype(BF16)], axis=1)

        def decay(blk):
            r = jnp.dot(p_ref[blk * MACRO:(blk + 1) * MACRO, :], la2, preferred_element_type=F32)
            return jnp.exp(r[:, :GLA_QK] + r[:, GLA_QK:])

        x_start = decay(0)
        qg = q * x_start
        kg = k * decay(1)
        a_last = x_start[MACRO - 1:MACRO, :]
        lvl = lvl_ref[...]
        xs = [decay(2 + l) for l in range(GLA_LEVELS)]
        for pair in range(GLA_HEADS // 2):
            sl = slice(LANES * pair, LANES * (pair + 1))
            att = _dot_nt(_split_heads(q[:, sl], lo_half).astype(BF16), k[:, sl].astype(BF16))
            att = jnp.where(lvl == GLA_LEVELS, att, 0.0)
            for l in range(GLA_LEVELS):
                xl = xs[l][:, sl]
                a = _dot_nt(_split_heads(q[:, sl] * xl, lo_half).astype(BF16),
                            (k[:, sl] * xl).astype(BF16))
                att = jnp.where(lvl == l, a, att)
            att = att.astype(BF16)
            st = st_ref[pair]
            inter = _dot_nt(_split_heads(qg[:, sl], lo_half).astype(BF16), st.astype(BF16))
            kgp = kg[:, sl].astype(BF16)
            upd = []
            for e in range(2):
                h = 2 * pair + e
                hs = slice(GLA_DV * h, GLA_DV * (h + 1))
                vh = v_ref[rows, hs]
                o = (jnp.dot(att[e * MACRO:(e + 1) * MACRO], vh, preferred_element_type=F32)
                     + inter[e * MACRO:(e + 1) * MACRO])
                gate = gg_ref[rows, hs].astype(F32)
                o = _rms(o, gn_ref[...]) * (gate / (1.0 + jnp.exp(-gate)))
                o_ref[rows, hs] = o.astype(o_ref.dtype)
                upd.append(_dot_tn(vh, kgp))
            st_ref[pair] = st * a_last[:, sl] + jnp.where(lo_half, upd[0], upd[1])
        return carry

    lax.fori_loop(0, n_macro, body, 0, unroll=unroll)


def _gla(gq, gk, la, gv, gg, gn, *, batch, seq, unroll=2):
    p_np, lvl_np = _gla_constants()
    pm = jnp.asarray(p_np, BF16)
    lvl = jnp.asarray(lvl_np)
    seq_spec = lambda width: pl.BlockSpec((None, seq, width), lambda b: (b, 0, 0))
    const = lambda shape: pl.BlockSpec(shape, lambda b: (0,) * len(shape))
    r3 = lambda a: a.reshape(batch, seq, a.shape[-1])
    return pl.pallas_call(
        functools.partial(_gla_kernel, n_macro=seq // MACRO, unroll=unroll),
        grid=(batch,),
        in_specs=[seq_spec(GLA_QK), seq_spec(GLA_QK), seq_spec(GLA_QK), seq_spec(GLA_W),
                  seq_spec(GLA_W), const(pm.shape), const(lvl.shape), const(gn.shape)],
        out_specs=seq_spec(GLA_W),
        out_shape=jax.ShapeDtypeStruct((batch, seq, GLA_W), BF16),
        scratch_shapes=[pltpu.VMEM((GLA_HEADS // 2, GLA_DV, LANES), F32)],
        compiler_params=pltpu.CompilerParams(
            dimension_semantics=("arbitrary",), vmem_limit_bytes=VMEM_LIMIT),
        name="gla",
    )(r3(gq), r3(gk), r3(la), r3(gv), r3(gg), pm, lvl, gn)


def _t5_bucket_np(dist):
    max_exact = REL_BUCKETS // 2
    d = np.maximum(dist, 1).astype(np.float32)
    large = max_exact + (np.log(d / np.float32(max_exact)) / np.float32(math.log(REL_MAX_DIST / max_exact))
                         * np.float32(REL_BUCKETS - max_exact)).astype(np.int32)
    large = np.minimum(large, REL_BUCKETS - 1)
    return np.where(dist < max_exact, dist, large).astype(np.int32)


def _bucket_table():
    qi = np.arange(BLK)[:, None]
    ki = np.arange(2 * BLK)[None, :]
    j = qi + BLK - ki
    valid = (j >= 0) & (j <= BAND)
    return np.stack([np.where(valid, _t5_bucket_np(np.maximum(j, 0) * d), -1) for d in DILATIONS]).astype(np.int32)


def _bias_kernel(rel_ref, bkt_ref, o_ref):
    pair = pl.program_id(1)
    bkt = bkt_ref[...]
    for e in range(2):
        acc = jnp.full(bkt.shape, NEG, F32)
        for u in range(REL_BUCKETS):
            acc = jnp.where(bkt == u, rel_ref[u, 2 * pair + e] * LOG2E, acc)
        o_ref[e * BLK:(e + 1) * BLK, :] = acc


def _bias_table(rel_bias):
    bkt = jnp.asarray(_bucket_table())
    n_pat = len(DILATIONS)
    return pl.pallas_call(
        _bias_kernel,
        grid=(n_pat, DIL_HEADS // 2),
        in_specs=[pl.BlockSpec(memory_space=pltpu.SMEM),
                  pl.BlockSpec((None, BLK, 2 * BLK), lambda p, h: (p, 0, 0))],
        out_specs=pl.BlockSpec((None, None, 2 * BLK, 2 * BLK), lambda p, h: (p, h, 0, 0)),
        out_shape=jax.ShapeDtypeStruct((n_pat, DIL_HEADS // 2, 2 * BLK, 2 * BLK), F32),
        compiler_params=pltpu.CompilerParams(dimension_semantics=("arbitrary", "arbitrary")),
        name="bias_table",
    )(rel_bias, bkt)


def _attn_kernel(q1, k1, v1, q4, k4, v4, q16, k16, v16, bias_ref, o_ref,
                 s_ref, mb_ref, u_ref, w_ref, m_ref, *, seq):
    lo_half = lax.broadcasted_iota(jnp.int32, (BLK, LANES), 1) < DIL_HD
    refs = ((q1, k1, v1), (q4, k4, v4), (q16, k16, v16))
    nblk = seq // BLK

    def key_rows(g, has_prev):
        return pl.ds((g - 1) * BLK, 2 * BLK) if has_prev else pl.ds(g * BLK, BLK)

    one_trip = jnp.minimum(pl.program_id(0) + 1, 1)

    def region(fn):
        lax.fori_loop(0, one_trip, lambda i, c: (fn(), c)[1], 0)

    def natural_rows(p, g):
        d = DILATIONS[p]
        r, n = divmod(g, nblk // d)
        return pl.ds(n * (BLK * d) + r, BLK, stride=d) if d > 1 else pl.ds(g * BLK, BLK)

    def scores(p):
        q_ref, k_ref, _ = refs[p]
        nb = nblk // DILATIONS[p]
        for g in range(nblk):
            has_prev = g % nb > 0
            nk = 2 * BLK if has_prev else BLK
            bias = bias_ref[p] if has_prev else bias_ref[p, :, BLK:]
            q = q_ref[pl.ds(g * BLK, BLK), :]
            s = _dot_nt(_split_heads(q, lo_half), k_ref[key_rows(g, has_prev), :]) + bias
            s_ref[p * nblk + g, :, :nk] = s
            mb = jnp.broadcast_to(jnp.max(s, axis=-1, keepdims=True), (2 * BLK, LANES))
            mb_ref[p * nblk + g] = mb
            m_ref[p, natural_rows(p, g), :] = jnp.where(lo_half, mb[:BLK], mb[BLK:])

    def outputs(p):
        _, _, v_ref = refs[p]
        nb = nblk // DILATIONS[p]
        for g in range(nblk):
            has_prev = g % nb > 0
            nk = 2 * BLK if has_prev else BLK
            idx = p * nblk + g
            e = jnp.concatenate(
                [jnp.exp2(s_ref[idx, pl.ds(h * BLK, BLK), c * LANES:(c + 1) * LANES]
                          - mb_ref[idx, pl.ds(h * BLK, BLK), :])
                 for h in range(2) for c in range(nk // LANES)], axis=1).astype(BF16)
            vv = v_ref[key_rows(g, has_prev), :]
            lo_k = lax.broadcasted_iota(jnp.int32, vv.shape, 1) < DIL_HD
            zero = jnp.zeros_like(vv)
            lane = lax.broadcasted_iota(jnp.int32, vv.shape, 1)
            sum_a = jnp.where(lane < DIL_HD, 1.0, 0.0).astype(BF16)
            sum_b = jnp.where(lane < DIL_HD, 0.0, 1.0).astype(BF16)
            rhs = jnp.concatenate(
                [jnp.concatenate([jnp.where(lo_k, vv, zero), sum_a], axis=1),
                 jnp.concatenate([jnp.where(lo_k, zero, vv), sum_b], axis=1)], axis=0)
            uw = jnp.dot(e, rhs, preferred_element_type=F32)
            dst = natural_rows(p, g)
            u_ref[p, dst, :] = uw[:, :LANES]
            w_ref[p, dst, :] = uw[:, LANES:]

    for p in range(len(DILATIONS)):
        region(functools.partial(scores, p))
        region(functools.partial(outputs, p))

    chunk = 2 * BLK

    def combine(i, carry):
        rows = pl.ds(pl.multiple_of(i * chunk, chunk), chunk)
        ms = [m_ref[p, rows, :] for p in range(len(DILATIONS))]
        mmax = functools.reduce(jnp.maximum, ms)
        cs = [jnp.exp2(mp - mmax) for mp in ms]
        num = sum(c * u_ref[p, rows, :] for p, c in enumerate(cs))
        den = sum(c * w_ref[p, rows, :] for p, c in enumerate(cs))
        o_ref[rows, :] = (num / den).astype(o_ref.dtype)
        return carry

    lax.fori_loop(0, seq // chunk, combine, 0)


def _dil_attn(a1, a4, a16, bias, *, batch, seq):
    n_pat = len(DILATIONS)
    n_pair = DIL_HEADS // 2
    arrs = [a.reshape(batch, seq, 3 * DIL_W) for a in (a1, a4, a16)]
    lane_blocks = DIL_W // LANES

    def spec(part):
        return pl.BlockSpec((None, seq, LANES), lambda b, h: (b, 0, part * lane_blocks + h))

    in_specs = [spec(part) for _ in range(n_pat) for part in range(3)]
    in_specs.append(pl.BlockSpec((n_pat, None, 2 * BLK, 2 * BLK), lambda b, h: (0, h, 0, 0)))
    args = [a for a in arrs for _ in range(3)] + [bias]
    return pl.pallas_call(
        functools.partial(_attn_kernel, seq=seq),
        grid=(batch, n_pair),
        in_specs=in_specs,
        out_specs=pl.BlockSpec((None, seq, LANES), lambda b, h: (b, 0, h)),
        out_shape=jax.ShapeDtypeStruct((batch, seq, DIL_W), BF16),
        scratch_shapes=[pltpu.VMEM((n_pat * seq // BLK, 2 * BLK, 2 * BLK), F32),
                        pltpu.VMEM((n_pat * seq // BLK, 2 * BLK, LANES), F32)]
        + [pltpu.VMEM((n_pat, seq, LANES), F32) for _ in range(3)],
        compiler_params=pltpu.CompilerParams(
            dimension_semantics=("arbitrary", "arbitrary"), vmem_limit_bytes=VMEM_LIMIT),
        name="dil_attn",
    )(*args)


def _post_kernel(h_ref, og_ref, od_ref, p_ref, wo_ref, w1_ref, w2_ref, wg_ref, wp_ref,
                 g_mix_ref, g_pre_ref, g_post_ref, o_ref, *, ff_chunk):
    mix = (jnp.dot(og_ref[...], wo_ref[:GLA_W, :], preferred_element_type=F32)
           + jnp.dot(od_ref[...], wo_ref[GLA_W:, :], preferred_element_type=F32))
    h1 = h_ref[...] + _rms(mix, g_mix_ref[...])
    xn = _rms(h1, g_pre_ref[...]).astype(BF16)
    f = jnp.zeros_like(h1)
    for c in range(w1_ref.shape[1] // ff_chunk):
        cols = slice(c * ff_chunk, (c + 1) * ff_chunk)
        a = jnp.maximum(jnp.dot(xn, w1_ref[:, cols], preferred_element_type=F32), 0.0)
        f = f + jnp.dot((a * a).astype(BF16), w2_ref[cols, :], preferred_element_type=F32)
    h2 = h1 + _rms(f, g_post_ref[...])
    gate = jnp.dot(h2.astype(BF16), wg_ref[...], preferred_element_type=F32)
    emb = jnp.dot(p_ref[...].astype(BF16), wp_ref[...], preferred_element_type=F32)
    o_ref[...] = h2 + emb / (1.0 + jnp.exp(-gate))


def _post(h, og, od, p, wo, w1, w2, wg, wp, g_mix, g_pre, g_post, *, layer, tm, ff_chunk=1024):
    t, dm = h.shape
    row = lambda width: pl.BlockSpec((tm, width), lambda i: (i, 0))
    const = lambda a: pl.BlockSpec((None,) + a.shape[1:], lambda i: (layer, 0, 0),
                                   pipeline_mode=pl.Buffered(1))
    consts = (wo, w1, w2, wg, wp, g_mix, g_pre, g_post)
    return pl.pallas_call(
        functools.partial(_post_kernel, ff_chunk=ff_chunk),
        grid=(t // tm,),
        in_specs=[row(dm), row(GLA_W), row(DIL_W),
                  pl.BlockSpec((None, tm, p.shape[2]), lambda i: (layer, i, 0))]
        + [const(a) for a in consts],
        out_specs=row(dm),
        out_shape=jax.ShapeDtypeStruct((t, dm), F32),
        compiler_params=pltpu.CompilerParams(
            dimension_semantics=("arbitrary",), vmem_limit_bytes=VMEM_LIMIT),
        name="post",
    )(h, og, od, p, *consts)


def kernel(x, p, w_in, w_gla_a2, b_gla_a, gla_norm_g, w_out, rel_bias, pre_mix_g, post_mix_g,
           pre_mlp_g, post_mlp_g, w_mlp_in, w_mlp_out, w_ple_gate, w_ple_proj):
    batch, seq, dm = x.shape
    depth = w_in.shape[0]
    assert seq % (max(DILATIONS) * BLK) == 0 and seq % MACRO == 0
    t = batch * seq
    rows1 = lambda a: a.reshape(depth, 1, -1)
    bias = _bias_table(rel_bias)
    h = x.reshape(t, dm)
    w = _prep_w_in(w_in)
    wa2 = jnp.pad(w_gla_a2, ((0, 0), (0, C_END - C_LR - GLA_LOWRANK), (0, 0))).astype(BF16)
    wo, w1, w2, wg, wp = (a.astype(BF16) for a in (w_out, w_mlp_in, w_mlp_out, w_ple_gate, w_ple_proj))
    p2 = p.reshape(depth, t, -1)
    for i in range(depth):
        gq, gk, la, gv, gg, a1, a4, a16 = _in_proj(
            h, rows1(pre_mix_g), w, wa2, rows1(b_gla_a), layer=i, batch=batch, seq=seq, tm=512)
        og = _gla(gq, gk, la, gv, gg, gla_norm_g[i].reshape(1, -1), batch=batch, seq=seq)
        od = _dil_attn(a1, a4, a16, bias, batch=batch, seq=seq)
        h = _post(h, og.reshape(t, GLA_W), od.reshape(t, DIL_W), p2, wo, w1, w2, wg, wp,
                  rows1(post_mix_g), rows1(pre_mlp_g), rows1(post_mlp_g), layer=i, tm=512)
    return h.reshape(batch, seq, dm)
```

```python
import functools
import math

import numpy as np
import jax
import jax.numpy as jnp
from jax import lax
from jax.experimental import pallas as pl
from jax.experimental.pallas import tpu as pltpu

F32 = jnp.float32
BF16 = jnp.bfloat16

EPS = 1e-6
GLA_HEADS = 4
GLA_DK = 64
GLA_DV = 128
GLA_QK = GLA_HEADS * GLA_DK
GLA_W = GLA_HEADS * GLA_DV
GLA_LOWRANK = 16
GLA_TAU = 16.0
DIL_HEADS = 8
DIL_HD = 64
DIL_W = DIL_HEADS * DIL_HD
DILATIONS = (1, 4, 16)
BAND = 128
BLK = 128
REL_BUCKETS = 32
REL_MAX_DIST = 2048
NEG = -1e30
LOG2E = math.log2(math.e)

LANES = 128
MACRO = 128
GLA_LEVELS = 7
VMEM_LIMIT = 56 * 1024 * 1024

C_GQ, C_GK, C_GV, C_GG, C_DQ, C_LR, C_END = 0, 256, 512, 1024, 1536, 3072, 3200


def _rms(x, g):
    return x * lax.rsqrt(jnp.mean(x * x, axis=-1, keepdims=True) + EPS) * g


def _w_in_kernel(wt_ref, o_ref, *, lr0):
    lr1 = lr0 + GLA_LOWRANK
    tk = wt_ref.shape[1]
    o_ref[:, :lr0] = wt_ref[:lr0, :].T.astype(BF16)
    o_ref[:, lr0:C_LR] = wt_ref[lr1:, :].T.astype(BF16)
    o_ref[:, C_LR:] = jnp.concatenate(
        [wt_ref[lr0:lr1, :].T, jnp.zeros((tk, C_END - C_LR - GLA_LOWRANK), F32)], axis=1).astype(BF16)


def _prep_w_in(w_in, *, tk=256):
    depth, dm, width = w_in.shape
    lr0 = C_DQ
    assert width - GLA_LOWRANK == C_LR
    return pl.pallas_call(
        functools.partial(_w_in_kernel, lr0=lr0),
        grid=(depth, dm // tk),
        in_specs=[pl.BlockSpec((None, width, tk), lambda l, i: (l, 0, i))],
        out_specs=pl.BlockSpec((None, tk, C_END), lambda l, i: (l, i, 0)),
        out_shape=jax.ShapeDtypeStruct((depth, dm, C_END), BF16),
        compiler_params=pltpu.CompilerParams(dimension_semantics=("arbitrary", "arbitrary")),
        name="prep_w_in",
    )(jnp.swapaxes(w_in, 1, 2))


def _in_proj_kernel(h_ref, g_ref, w_ref, wa2_ref, ba_ref,
                    gq_ref, gk_ref, la_ref, gv_ref, gg_ref, a1_ref, a4_ref, a16_ref,
                    slab_ref, slab4_ref, *, tm):
    xn = _rms(h_ref[...], g_ref[...]).astype(BF16)

    def proj(lo, hi):
        return jnp.dot(xn, w_ref[:, lo:hi], preferred_element_type=F32)

    for c in range(3):
        y = proj(C_DQ + DIL_W * c, C_DQ + DIL_W * (c + 1))
        if c == 0:
            y = y * (LOG2E * DIL_HD ** -0.5)
        for s in range(DIL_W // LANES):
            col = DIL_W * c + LANES * s
            ys = y[:, LANES * s:LANES * (s + 1)]
            a1_ref[0, :, col:col + LANES] = ys.astype(BF16)
            idx = c * (DIL_W // LANES) + s
            slab_ref[idx] = ys
            for r in range(4):
                y4 = slab_ref[idx, pl.ds(r, tm // 4, stride=4), :]
                a4_ref[r, :, col:col + LANES] = y4.astype(BF16)
                slab4_ref[idx, r] = y4
            for r in range(16):
                a16_ref[r, :, col:col + LANES] = slab4_ref[
                    idx, r % 4, pl.ds(r // 4, tm // 16, stride=4), :].astype(BF16)

    gq_ref[...] = (proj(C_GQ, C_GK) * (GLA_DK ** -0.5)).astype(BF16)
    gk_ref[...] = proj(C_GK, C_GV).astype(BF16)
    gv_ref[...] = proj(C_GV, C_GG).astype(BF16)
    gg_ref[...] = proj(C_GG, C_DQ).astype(BF16)
    z = jnp.dot(proj(C_LR, C_END).astype(BF16), wa2_ref[...],
                preferred_element_type=F32) + ba_ref[...]
    la_ref[...] = (jnp.minimum(z, 0.0) - jnp.log1p(jnp.exp(-jnp.abs(z)))) * (1.0 / GLA_TAU)


def _in_proj(h, g, w, wa2, ba, *, layer, batch, seq, tm):
    t = batch * seq
    nt = seq // tm
    row = lambda width: pl.BlockSpec((tm, width), lambda b, i: (b * nt + i, 0))
    const = lambda shape: pl.BlockSpec((None,) + shape[1:], lambda b, i: (layer, 0, 0))
    dil_spec = lambda d: pl.BlockSpec((None, d, tm // d, 3 * DIL_W), lambda b, i: (b, 0, i, 0))
    out_shape = (
        jax.ShapeDtypeStruct((t, GLA_QK), BF16), jax.ShapeDtypeStruct((t, GLA_QK), BF16),
        jax.ShapeDtypeStruct((t, GLA_QK), F32),
        jax.ShapeDtypeStruct((t, GLA_W), BF16), jax.ShapeDtypeStruct((t, GLA_W), BF16),
    ) + tuple(jax.ShapeDtypeStruct((batch, d, seq // d, 3 * DIL_W), BF16) for d in DILATIONS)
    return pl.pallas_call(
        functools.partial(_in_proj_kernel, tm=tm),
        grid=(batch, nt),
        in_specs=[row(h.shape[1]), const(g.shape), const(w.shape), const(wa2.shape), const(ba.shape)],
        out_specs=(row(GLA_QK), row(GLA_QK), row(GLA_QK), row(GLA_W), row(GLA_W))
        + tuple(dil_spec(d) for d in DILATIONS),
        out_shape=out_shape,
        scratch_shapes=[pltpu.VMEM((3 * DIL_W // LANES, tm, LANES), F32),
                        pltpu.VMEM((3 * DIL_W // LANES, 4, tm // 4, LANES), F32)],
        compiler_params=pltpu.CompilerParams(
            dimension_semantics=("arbitrary", "arbitrary"), vmem_limit_bytes=VMEM_LIMIT),
        name="in_proj",
    )(h, g, w, wa2, ba)


def _gla_constants():
    n = MACRO
    i = np.arange(n)[:, None]
    t = np.arange(n)[None, :]
    blocks = [t <= i, t > i]
    for l in range(GLA_LEVELS):
        ref = ((i >> (l + 1)) << (l + 1)) + (1 << l) - 1
        odd = ((i >> l) & 1) == 1
        blocks.append(np.where(odd, (t > ref) & (t <= i), (t > i) & (t <= ref)))
    p = np.concatenate(blocks, axis=0).astype(np.float32)
    j = np.arange(n)[None, :]
    lvl = np.where(j < i, np.floor(np.log2(np.maximum(i ^ j, 1))).astype(np.int32),
                   np.where(j == i, GLA_LEVELS, -1)).astype(np.int32)
    return p, np.concatenate([lvl, lvl], axis=0)


def _split_heads(x, lo_half):
    zero = jnp.zeros_like(x)
    return jnp.concatenate([jnp.where(lo_half, x, zero), jnp.where(lo_half, zero, x)], axis=0)


def _dot_nt(a, b):
    return lax.dot_general(a, b, (((1,), (1,)), ((), ())), preferred_element_type=F32)


def _dot_tn(a, b):
    return lax.dot_general(a, b, (((0,), (0,)), ((), ())), preferred_element_type=F32)


def _gla_kernel(q_ref, k_ref, la_ref, v_ref, gg_ref, p_ref, lvl_ref, gn_ref, o_ref, st_ref,
                *, n_macro, unroll):
    st_ref[...] = jnp.zeros_like(st_ref)
    lo_half = lax.broadcasted_iota(jnp.int32, (MACRO, LANES), 1) < (LANES // 2)

    def body(m, carry):
        r0 = pl.multiple_of(m * MACRO, MACRO)
        rows = pl.ds(r0, MACRO)
        q = q_ref[rows, :].astype(F32)
        k = k_ref[rows, :].astype(F32)
        la = la_ref[rows, :]
        la_hi = la.astype(BF16)
        la2 = jnp.concatenate([la_hi, (la - la_hi.astype(F32)).astype(BF16)], axis=1)

        def decay(blk):
            r = jnp.dot(p_ref[blk * MACRO:(blk + 1) * MACRO, :], la2, preferred_element_type=F32)
            return jnp.exp(r[:, :GLA_QK] + r[:, GLA_QK:])

        x_start = decay(0)
        qg = q * x_start
        kg = k * decay(1)
        a_last = x_start[MACRO - 1:MACRO, :]
        lvl = lvl_ref[...]
        xs = [decay(2 + l) for l in range(GLA_LEVELS)]
        for pair in range(GLA_HEADS // 2):
            sl = slice(LANES * pair, LANES * (pair + 1))
            att = _dot_nt(_split_heads(q[:, sl], lo_half).astype(BF16), k[:, sl].astype(BF16))
            att = jnp.where(lvl == GLA_LEVELS, att, 0.0)
            for l in range(GLA_LEVELS):
                xl = xs[l][:, sl]
                a = _dot_nt(_split_heads(q[:, sl] * xl, lo_half).astype(BF16),
                            (k[:, sl] * xl).astype(BF16))
                att = jnp.where(lvl == l, a, att)
            att = att.astype(BF16)
            st = st_ref[pair]
            inter = _dot_nt(_split_heads(qg[:, sl], lo_half).astype(BF16), st.astype(BF16))
            kgp = kg[:, sl].astype(BF16)
            upd = []
            for e in range(2):
                h = 2 * pair + e
                hs = slice(GLA_DV * h, GLA_DV * (h + 1))
                vh = v_ref[rows, hs]
                o = (jnp.dot(att[e * MACRO:(e + 1) * MACRO], vh, preferred_element_type=F32)
                     + inter[e * MACRO:(e + 1) * MACRO])
                gate = gg_ref[rows, hs].astype(F32)
                o = _rms(o, gn_ref[...]) * (gate / (1.0 + jnp.exp(-gate)))
                o_ref[rows, hs] = o.astype(o_ref.dtype)
                upd.append(_dot_tn(vh, kgp))
            st_ref[pair] = st * a_last[:, sl] + jnp.where(lo_half, upd[0], upd[1])
        return carry

    lax.fori_loop(0, n_macro, body, 0, unroll=unroll)


def _gla(gq, gk, la, gv, gg, gn, *, batch, seq, unroll=2):
    p_np, lvl_np = _gla_constants()
    pm = jnp.asarray(p_np, BF16)
    lvl = jnp.asarray(lvl_np)
    seq_spec = lambda width: pl.BlockSpec((None, seq, width), lambda b: (b, 0, 0))
    const = lambda shape: pl.BlockSpec(shape, lambda b: (0,) * len(shape))
    r3 = lambda a: a.reshape(batch, seq, a.shape[-1])
    return pl.pallas_call(
        functools.partial(_gla_kernel, n_macro=seq // MACRO, unroll=unroll),
        grid=(batch,),
        in_specs=[seq_spec(GLA_QK), seq_spec(GLA_QK), seq_spec(GLA_QK), seq_spec(GLA_W),
                  seq_spec(GLA_W), const(pm.shape), const(lvl.shape), const(gn.shape)],
        out_specs=seq_spec(GLA_W),
        out_shape=jax.ShapeDtypeStruct((batch, seq, GLA_W), BF16),
        scratch_shapes=[pltpu.VMEM((GLA_HEADS // 2, GLA_DV, LANES), F32)],
        compiler_params=pltpu.CompilerParams(
            dimension_semantics=("arbitrary",), vmem_limit_bytes=VMEM_LIMIT),
        name="gla",
    )(r3(gq), r3(gk), r3(la), r3(gv), r3(gg), pm, lvl, gn)


def _t5_bucket_np(dist):
    max_exact = REL_BUCKETS // 2
    d = np.maximum(dist, 1).astype(np.float32)
    large = max_exact + (np.log(d / np.float32(max_exact)) / np.float32(math.log(REL_MAX_DIST / max_exact))
                         * np.float32(REL_BUCKETS - max_exact)).astype(np.int32)
    large = np.minimum(large, REL_BUCKETS - 1)
    return np.where(dist < max_exact, dist, large).astype(np.int32)


def _bucket_table():
    qi = np.arange(BLK)[:, None]
    ki = np.arange(2 * BLK)[None, :]
    j = qi + BLK - ki
    valid = (j >= 0) & (j <= BAND)
    return np.stack([np.where(valid, _t5_bucket_np(np.maximum(j, 0) * d), -1) for d in DILATIONS]).astype(np.int32)


def _bias_kernel(rel_ref, bkt_ref, o_ref):
    pair = pl.program_id(1)
    bkt = bkt_ref[...]
    for e in range(2):
        acc = jnp.full(bkt.shape, NEG, F32)
        for u in range(REL_BUCKETS):
            acc = jnp.where(bkt == u, rel_ref[u, 2 * pair + e] * LOG2E, acc)
        o_ref[e * BLK:(e + 1) * BLK, :] = acc


def _bias_table(rel_bias):
    bkt = jnp.asarray(_bucket_table())
    n_pat = len(DILATIONS)
    return pl.pallas_call(
        _bias_kernel,
        grid=(n_pat, DIL_HEADS // 2),
        in_specs=[pl.BlockSpec(memory_space=pltpu.SMEM),
                  pl.BlockSpec((None, BLK, 2 * BLK), lambda p, h: (p, 0, 0))],
        out_specs=pl.BlockSpec((None, None, 2 * BLK, 2 * BLK), lambda p, h: (p, h, 0, 0)),
        out_shape=jax.ShapeDtypeStruct((n_pat, DIL_HEADS // 2, 2 * BLK, 2 * BLK), F32),
        compiler_params=pltpu.CompilerParams(dimension_semantics=("arbitrary", "arbitrary")),
        name="bias_table",
    )(rel_bias, bkt)


def _attn_kernel(q1, k1, v1, q4, k4, v4, q16, k16, v16, bias_ref, o_ref,
                 s_ref, mb_ref, u_ref, w_ref, m_ref, *, seq):
    lo_half = lax.broadcasted_iota(jnp.int32, (BLK, LANES), 1) < DIL_HD
    refs = ((q1, k1, v1), (q4, k4, v4), (q16, k16, v16))
    nblk = seq // BLK

    def key_rows(g, has_prev):
        return pl.ds((g - 1) * BLK, 2 * BLK) if has_prev else pl.ds(g * BLK, BLK)

    one_trip = jnp.minimum(pl.program_id(0) + 1, 1)

    def region(fn):
        lax.fori_loop(0, one_trip, lambda i, c: (fn(), c)[1], 0)

    def natural_rows(p, g):
        d = DILATIONS[p]
        r, n = divmod(g, nblk // d)
        return pl.ds(n * (BLK * d) + r, BLK, stride=d) if d > 1 else pl.ds(g * BLK, BLK)

    def scores(p):
        q_ref, k_ref, _ = refs[p]
        nb = nblk // DILATIONS[p]
        for g in range(nblk):
            has_prev = g % nb > 0
            nk = 2 * BLK if has_prev else BLK
            bias = bias_ref[p] if has_prev else bias_ref[p, :, BLK:]
            q = q_ref[pl.ds(g * BLK, BLK), :]
            s = _dot_nt(_split_heads(q, lo_half), k_ref[key_rows(g, has_prev), :]) + bias
            s_ref[p * nblk + g, :, :nk] = s
            mb = jnp.broadcast_to(jnp.max(s, axis=-1, keepdims=True), (2 * BLK, LANES))
            mb_ref[p * nblk + g] = mb
            m_ref[p, natural_rows(p, g), :] = jnp.where(lo_half, mb[:BLK], mb[BLK:])

    def outputs(p):
        _, _, v_ref = refs[p]
        nb = nblk // DILATIONS[p]
        for g in range(nblk):
            has_prev = g % nb > 0
            nk = 2 * BLK if has_prev else BLK
            idx = p * nblk + g
            e = jnp.concatenate(
                [jnp.exp2(s_ref[idx, pl.ds(h * BLK, BLK), c * LANES:(c + 1) * LANES]
                          - mb_ref[idx, pl.ds(h * BLK, BLK), :])
                 for h in range(2) for c in range(nk // LANES)], axis=1).astype(BF16)
            vv = v_ref[key_rows(g, has_prev), :]
            lo_k = lax.broadcasted_iota(jnp.int32, vv.shape, 1) < DIL_HD
            zero = jnp.zeros_like(vv)
            lane = lax.broadcasted_iota(jnp.int32, vv.shape, 1)
            sum_a = jnp.where(lane < DIL_HD, 1.0, 0.0).astype(BF16)
            sum_b = jnp.where(lane < DIL_HD, 0.0, 1.0).astype(BF16)
            rhs = jnp.concatenate(
                [jnp.concatenate([jnp.where(lo_k, vv, zero), sum_a], axis=1),
                 jnp.concatenate([jnp.where(lo_k, zero, vv), sum_b], axis=1)], axis=0)
            uw = jnp.dot(e, rhs, preferred_element_type=F32)
            dst = natural_rows(p, g)
            u_ref[p, dst, :] = uw[:, :LANES]
            w_ref[p, dst, :] = uw[:, LANES:]

    for p in range(len(DILATIONS)):
        region(functools.partial(scores, p))
        region(functools.partial(outputs, p))

    chunk = 2 * BLK

    def combine(i, carry):
        rows = pl.ds(pl.multiple_of(i * chunk, chunk), chunk)
        ms = [m_ref[p, rows, :] for p in range(len(DILATIONS))]
        mmax = functools.reduce(jnp.maximum, ms)
        cs = [jnp.exp2(mp - mmax) for mp in ms]
        num = sum(c * u_ref[p, rows, :] for p, c in enumerate(cs))
        den = sum(c * w_ref[p, rows, :] for p, c in enumerate(cs))
        o_ref[rows, :] = (num / den).astype(o_ref.dtype)
        return carry

    lax.fori_loop(0, seq // chunk, combine, 0)


def _dil_attn(a1, a4, a16, bias, *, batch, seq):
    n_pat = len(DILATIONS)
    n_pair = DIL_HEADS // 2
    arrs = [a.reshape(batch, seq, 3 * DIL_W) for a in (a1, a4, a16)]
    lane_blocks = DIL_W // LANES

    def spec(part):
        return pl.BlockSpec((None, seq, LANES), lambda b, h: (b, 0, part * lane_blocks + h))

    in_specs = [spec(part) for _ in range(n_pat) for part in range(3)]
    in_specs.append(pl.BlockSpec((n_pat, None, 2 * BLK, 2 * BLK), lambda b, h: (0, h, 0, 0)))
    args = [a for a in arrs for _ in range(3)] + [bias]
    return pl.pallas_call(
        functools.partial(_attn_kernel, seq=seq),
        grid=(batch, n_pair),
        in_specs=in_specs,
        out_specs=pl.BlockSpec((None, seq, LANES), lambda b, h: (b, 0, h)),
        out_shape=jax.ShapeDtypeStruct((batch, seq, DIL_W), BF16),
        scratch_shapes=[pltpu.VMEM((n_pat * seq // BLK, 2 * BLK, 2 * BLK), F32),
                        pltpu.VMEM((n_pat * seq // BLK, 2 * BLK, LANES), F32)]
        + [pltpu.VMEM((n_pat, seq, LANES), F32) for _ in range(3)],
        compiler_params=pltpu.CompilerParams(
            dimension_semantics=("arbitrary", "arbitrary"), vmem_limit_bytes=VMEM_LIMIT),
        name="dil_attn",
    )(*args)


def _post_kernel(h_ref, og_ref, od_ref, p_ref, wo_ref, w1_ref, w2_ref, wg_ref, wp_ref,
                 g_mix_ref, g_pre_ref, g_post_ref, o_ref, *, ff_chunk):
    mix = (jnp.dot(og_ref[...], wo_ref[:GLA_W, :], preferred_element_type=F32)
           + jnp.dot(od_ref[...], wo_ref[GLA_W:, :], preferred_element_type=F32))
    emb = jnp.dot(p_ref[...].astype(BF16), wp_ref[...], preferred_element_type=F32)
    h1 = h_ref[...] + _rms(mix, g_mix_ref[...])
    xn = _rms(h1, g_pre_ref[...]).astype(BF16)
    f = jnp.zeros_like(h1)
    for c in range(w1_ref.shape[1] // ff_chunk):
        cols = slice(c * ff_chunk, (c + 1) * ff_chunk)
        a = jnp.maximum(jnp.dot(xn, w1_ref[:, cols], preferred_element_type=F32), 0.0)
        f = f + jnp.dot((a * a).astype(BF16), w2_ref[cols, :], preferred_element_type=F32)
    h2 = h1 + _rms(f, g_post_ref[...])
    gate = jnp.dot(h2.astype(BF16), wg_ref[...], preferred_element_type=F32)
    o_ref[...] = h2 + emb / (1.0 + jnp.exp(-gate))


def _post(h, og, od, p, wo, w1, w2, wg, wp, g_mix, g_pre, g_post, *, layer, tm, ff_chunk=1024):
    t, dm = h.shape
    row = lambda width: pl.BlockSpec((tm, width), lambda i: (i, 0))
    const = lambda a: pl.BlockSpec((None,) + a.shape[1:], lambda i: (layer, 0, 0),
                                   pipeline_mode=pl.Buffered(1))
    consts = (wo, w1, w2, wg, wp, g_mix, g_pre, g_post)
    return pl.pallas_call(
        functools.partial(_post_kernel, ff_chunk=ff_chunk),
        grid=(t // tm,),
        in_specs=[row(dm), row(GLA_W), row(DIL_W),
                  pl.BlockSpec((None, tm, p.shape[2]), lambda i: (layer, i, 0))]
        + [const(a) for a in consts],
        out_specs=row(dm),
        out_shape=jax.ShapeDtypeStruct((t, dm), F32),
        compiler_params=pltpu.CompilerParams(
            dimension_semantics=("arbitrary",), vmem_limit_bytes=VMEM_LIMIT),
        name="post",
    )(h, og, od, p, *consts)


def kernel(x, p, w_in, w_gla_a2, b_gla_a, gla_norm_g, w_out, rel_bias, pre_mix_g, post_mix_g,
           pre_mlp_g, post_mlp_g, w_mlp_in, w_mlp_out, w_ple_gate, w_ple_proj):
    batch, seq, dm = x.shape
    depth = w_in.shape[0]
    assert seq % (max(DILATIONS) * BLK) == 0 and seq % MACRO == 0
    t = batch * seq
    rows1 = lambda a: a.reshape(depth, 1, -1)
    bias = _bias_table(rel_bias)
    h = x.reshape(t, dm)
    w = _prep_w_in(w_in)
    wa2 = jnp.pad(w_gla_a2, ((0, 0), (0, C_END - C_LR - GLA_LOWRANK), (0, 0))).astype(BF16)
    wo, w1, w2, wg, wp = (a.astype(BF16) for a in (w_out, w_mlp_in, w_mlp_out, w_ple_gate, w_ple_proj))
    p2 = p.reshape(depth, t, -1)
    for i in range(depth):
        gq, gk, la, gv, gg, a1, a4, a16 = _in_proj(
            h, rows1(pre_mix_g), w, wa2, rows1(b_gla_a), layer=i, batch=batch, seq=seq, tm=512)
        og = _gla(gq, gk, la, gv, gg, gla_norm_g[i].reshape(1, -1), batch=batch, seq=seq)
        od = _dil_attn(a1, a4, a16, bias, batch=batch, seq=seq)
        h = _post(h, og.reshape(t, GLA_W), od.reshape(t, DIL_W), p2, wo, w1, w2, wg, wp,
                  rows1(post_mix_g), rows1(pre_mlp_g), rows1(post_mlp_g), layer=i, tm=512)
    return h.reshape(batch, seq, dm)
```

```python
import functools
import math

import numpy as np
import jax
import jax.numpy as jnp
from jax import lax
from jax.experimental import pallas as pl
from jax.experimental.pallas import tpu as pltpu

F32 = jnp.float32
BF16 = jnp.bfloat16

EPS = 1e-6
GLA_HEADS = 4
GLA_DK = 64
GLA_DV = 128
GLA_QK = GLA_HEADS * GLA_DK
GLA_W = GLA_HEADS * GLA_DV
GLA_LOWRANK = 16
GLA_TAU = 16.0
DIL_HEADS = 8
DIL_HD = 64
DIL_W = DIL_HEADS * DIL_HD
DILATIONS = (1, 4, 16)
BAND = 128
BLK = 128
REL_BUCKETS = 32
REL_MAX_DIST = 2048
NEG = -1e30
LOG2E = math.log2(math.e)

LANES = 128
MACRO = 128
GLA_LEVELS = 7
VMEM_LIMIT = 56 * 1024 * 1024

C_GQ, C_GK, C_GV, C_GG, C_DQ, C_LR, C_END = 0, 256, 512, 1024, 1536, 3072, 3200


def _rms(x, g):
    return x * lax.rsqrt(jnp.mean(x * x, axis=-1, keepdims=True) + EPS) * g


def _w_in_kernel(wt_ref, o_ref, *, lr0):
    lr1 = lr0 + GLA_LOWRANK
    tk = wt_ref.shape[1]
    o_ref[:, :lr0] = wt_ref[:lr0, :].T.astype(BF16)
    o_ref[:, lr0:C_LR] = wt_ref[lr1:, :].T.astype(BF16)
    o_ref[:, C_LR:] = jnp.concatenate(
        [wt_ref[lr0:lr1, :].T, jnp.zeros((tk, C_END - C_LR - GLA_LOWRANK), F32)], axis=1).astype(BF16)


def _prep_w_in(w_in, *, tk=256):
    depth, dm, width = w_in.shape
    lr0 = C_DQ
    assert width - GLA_LOWRANK == C_LR
    return pl.pallas_call(
        functools.partial(_w_in_kernel, lr0=lr0),
        grid=(depth, dm // tk),
        in_specs=[pl.BlockSpec((None, width, tk), lambda l, i: (l, 0, i))],
        out_specs=pl.BlockSpec((None, tk, C_END), lambda l, i: (l, i, 0)),
        out_shape=jax.ShapeDtypeStruct((depth, dm, C_END), BF16),
        compiler_params=pltpu.CompilerParams(dimension_semantics=("arbitrary", "arbitrary")),
        name="prep_w_in",
    )(jnp.swapaxes(w_in, 1, 2))


def _in_proj_kernel(h_ref, g_ref, w_ref, wa2_ref, ba_ref,
                    gq_ref, gk_ref, la_ref, gv_ref, gg_ref, a1_ref, a4_ref, a16_ref,
                    slab_ref, slab4_ref, *, tm):
    xn = _rms(h_ref[...], g_ref[...]).astype(BF16)

    def proj(lo, hi):
        return jnp.dot(xn, w_ref[:, lo:hi], preferred_element_type=F32)

    for c in range(3):
        y = proj(C_DQ + DIL_W * c, C_DQ + DIL_W * (c + 1))
        if c == 0:
            y = y * (LOG2E * DIL_HD ** -0.5)
        for s in range(DIL_W // LANES):
            ys = y[:, LANES * s:LANES * (s + 1)]
            idx = c * (DIL_W // LANES) + s
            a1_ref[0, idx] = ys.astype(BF16)
            slab_ref[idx] = ys
            for r in range(4):
                y4 = slab_ref[idx, pl.ds(r, tm // 4, stride=4), :]
                a4_ref[r, idx] = y4.astype(BF16)
                slab4_ref[idx, r] = y4
            for r in range(16):
                a16_ref[r, idx] = slab4_ref[idx, r % 4, pl.ds(r // 4, tm // 16, stride=4), :].astype(BF16)

    gq_ref[...] = (proj(C_GQ, C_GK) * (GLA_DK ** -0.5)).astype(BF16)
    gk_ref[...] = proj(C_GK, C_GV).astype(BF16)
    gv_ref[...] = proj(C_GV, C_GG).astype(BF16)
    gg_ref[...] = proj(C_GG, C_DQ).astype(BF16)
    z = jnp.dot(proj(C_LR, C_END).astype(BF16), wa2_ref[...],
                preferred_element_type=F32) + ba_ref[...]
    la_ref[...] = (jnp.minimum(z, 0.0) - jnp.log1p(jnp.exp(-jnp.abs(z)))) * (1.0 / GLA_TAU)


def _in_proj(h, g, w, wa2, ba, *, layer, batch, seq, tm):
    t = batch * seq
    nt = seq // tm
    row = lambda width: pl.BlockSpec((tm, width), lambda b, i: (b * nt + i, 0))
    const = lambda shape: pl.BlockSpec((None,) + shape[1:], lambda b, i: (layer, 0, 0))
    n_slab = 3 * DIL_W // LANES
    dil_spec = lambda d: pl.BlockSpec((None, d, n_slab, tm // d, LANES), lambda b, i: (b, 0, 0, i, 0))
    out_shape = (
        jax.ShapeDtypeStruct((t, GLA_QK), BF16), jax.ShapeDtypeStruct((t, GLA_QK), BF16),
        jax.ShapeDtypeStruct((t, GLA_QK), F32),
        jax.ShapeDtypeStruct((t, GLA_W), BF16), jax.ShapeDtypeStruct((t, GLA_W), BF16),
    ) + tuple(jax.ShapeDtypeStruct((batch, d, n_slab, seq // d, LANES), BF16) for d in DILATIONS)
    return pl.pallas_call(
        functools.partial(_in_proj_kernel, tm=tm),
        grid=(batch, nt),
        in_specs=[row(h.shape[1]), const(g.shape), const(w.shape), const(wa2.shape), const(ba.shape)],
        out_specs=(row(GLA_QK), row(GLA_QK), row(GLA_QK), row(GLA_W), row(GLA_W))
        + tuple(dil_spec(d) for d in DILATIONS),
        out_shape=out_shape,
        scratch_shapes=[pltpu.VMEM((n_slab, tm, LANES), F32),
                        pltpu.VMEM((n_slab, 4, tm // 4, LANES), F32)],
        compiler_params=pltpu.CompilerParams(
            dimension_semantics=("arbitrary", "arbitrary"), vmem_limit_bytes=VMEM_LIMIT),
        name="in_proj",
    )(h, g, w, wa2, ba)


def _gla_constants():
    n = MACRO
    i = np.arange(n)[:, None]
    t = np.arange(n)[None, :]
    blocks = [t <= i, t > i]
    for l in range(GLA_LEVELS):
        ref = ((i >> (l + 1)) << (l + 1)) + (1 << l) - 1
        odd = ((i >> l) & 1) == 1
        blocks.append(np.where(odd, (t > ref) & (t <= i), (t > i) & (t <= ref)))
    p = np.concatenate(blocks, axis=0).astype(np.float32)
    j = np.arange(n)[None, :]
    lvl = np.where(j < i, np.floor(np.log2(np.maximum(i ^ j, 1))).astype(np.int32),
                   np.where(j == i, GLA_LEVELS, -1)).astype(np.int32)
    return p, np.concatenate([lvl, lvl], axis=0)


def _split_heads(x, lo_half):
    zero = jnp.zeros_like(x)
    return jnp.concatenate([jnp.where(lo_half, x, zero), jnp.where(lo_half, zero, x)], axis=0)


def _dot_nt(a, b):
    return lax.dot_general(a, b, (((1,), (1,)), ((), ())), preferred_element_type=F32)


def _dot_tn(a, b):
    return lax.dot_general(a, b, (((0,), (0,)), ((), ())), preferred_element_type=F32)


def _gla_kernel(q_ref, k_ref, la_ref, v_ref, gg_ref, p_ref, lvl_ref, gn_ref, o_ref, st_ref,
                *, n_macro, unroll):
    st_ref[...] = jnp.zeros_like(st_ref)
    lo_half = lax.broadcasted_iota(jnp.int32, (MACRO, LANES), 1) < (LANES // 2)

    def body(m, carry):
        r0 = pl.multiple_of(m * MACRO, MACRO)
        rows = pl.ds(r0, MACRO)
        q = q_ref[rows, :].astype(F32)
        k = k_ref[rows, :].astype(F32)
        la = la_ref[rows, :]
        la_hi = la.astype(BF16)
        la2 = jnp.concatenate([la_hi, (la - la_hi.astype(F32)).astype(BF16)], axis=1)

        def decay(blk):
            r = jnp.dot(p_ref[blk * MACRO:(blk + 1) * MACRO, :], la2, preferred_element_type=F32)
            return jnp.exp(r[:, :GLA_QK] + r[:, GLA_QK:])

        x_start = decay(0)
        qg = q * x_start
        kg = k * decay(1)
        a_last = x_start[MACRO - 1:MACRO, :]
        lvl = lvl_ref[...]
        xs = [decay(2 + l) for l in range(GLA_LEVELS)]
        for pair in range(GLA_HEADS // 2):
            sl = slice(LANES * pair, LANES * (pair + 1))
            att = _dot_nt(_split_heads(q[:, sl], lo_half).astype(BF16), k[:, sl].astype(BF16))
            att = jnp.where(lvl == GLA_LEVELS, att, 0.0)
            for l in range(GLA_LEVELS):
                xl = xs[l][:, sl]
                a = _dot_nt(_split_heads(q[:, sl] * xl, lo_half).astype(BF16),
                            (k[:, sl] * xl).astype(BF16))
                att = jnp.where(lvl == l, a, att)
            att = att.astype(BF16)
            st = st_ref[pair]
            inter = _dot_nt(_split_heads(qg[:, sl], lo_half).astype(BF16), st.astype(BF16))
            kgp = kg[:, sl].astype(BF16)
            upd = []
            for e in range(2):
                h = 2 * pair + e
                hs = slice(GLA_DV * h, GLA_DV * (h + 1))
                vh = v_ref[rows, hs]
                o = (jnp.dot(att[e * MACRO:(e + 1) * MACRO], vh, preferred_element_type=F32)
                     + inter[e * MACRO:(e + 1) * MACRO])
                gate = gg_ref[rows, hs].astype(F32)
                o = _rms(o, gn_ref[...]) * (gate / (1.0 + jnp.exp(-gate)))
                o_ref[rows, hs] = o.astype(o_ref.dtype)
                upd.append(_dot_tn(vh, kgp))
            st_ref[pair] = st * a_last[:, sl] + jnp.where(lo_half, upd[0], upd[1])
        return carry

    lax.fori_loop(0, n_macro, body, 0, unroll=unroll)


def _gla(gq, gk, la, gv, gg, gn, *, batch, seq, unroll=2):
    p_np, lvl_np = _gla_constants()
    pm = jnp.asarray(p_np, BF16)
    lvl = jnp.asarray(lvl_np)
    seq_spec = lambda width: pl.BlockSpec((None, seq, width), lambda b: (b, 0, 0))
    const = lambda shape: pl.BlockSpec(shape, lambda b: (0,) * len(shape))
    r3 = lambda a: a.reshape(batch, seq, a.shape[-1])
    return pl.pallas_call(
        functools.partial(_gla_kernel, n_macro=seq // MACRO, unroll=unroll),
        grid=(batch,),
        in_specs=[seq_spec(GLA_QK), seq_spec(GLA_QK), seq_spec(GLA_QK), seq_spec(GLA_W),
                  seq_spec(GLA_W), const(pm.shape), const(lvl.shape), const(gn.shape)],
        out_specs=seq_spec(GLA_W),
        out_shape=jax.ShapeDtypeStruct((batch, seq, GLA_W), BF16),
        scratch_shapes=[pltpu.VMEM((GLA_HEADS // 2, GLA_DV, LANES), F32)],
        compiler_params=pltpu.CompilerParams(
            dimension_semantics=("arbitrary",), vmem_limit_bytes=VMEM_LIMIT),
        name="gla",
    )(r3(gq), r3(gk), r3(la), r3(gv), r3(gg), pm, lvl, gn)


def _t5_bucket_np(dist):
    max_exact = REL_BUCKETS // 2
    d = np.maximum(dist, 1).astype(np.float32)
    large = max_exact + (np.log(d / np.float32(max_exact)) / np.float32(math.log(REL_MAX_DIST / max_exact))
                         * np.float32(REL_BUCKETS - max_exact)).astype(np.int32)
    large = np.minimum(large, REL_BUCKETS - 1)
    return np.where(dist < max_exact, dist, large).astype(np.int32)


def _bucket_table():
    qi = np.arange(BLK)[:, None]
    ki = np.arange(2 * BLK)[None, :]
    j = qi + BLK - ki
    valid = (j >= 0) & (j <= BAND)
    return np.stack([np.where(valid, _t5_bucket_np(np.maximum(j, 0) * d), -1) for d in DILATIONS]).astype(np.int32)


def _bias_kernel(rel_ref, bkt_ref, o_ref):
    pair = pl.program_id(1)
    bkt = bkt_ref[...]
    for e in range(2):
        acc = jnp.full(bkt.shape, NEG, F32)
        for u in range(REL_BUCKETS):
            acc = jnp.where(bkt == u, rel_ref[u, 2 * pair + e] * LOG2E, acc)
        o_ref[e * BLK:(e + 1) * BLK, :] = acc


def _bias_table(rel_bias):
    bkt = jnp.asarray(_bucket_table())
    n_pat = len(DILATIONS)
    return pl.pallas_call(
        _bias_kernel,
        grid=(n_pat, DIL_HEADS // 2),
        in_specs=[pl.BlockSpec(memory_space=pltpu.SMEM),
                  pl.BlockSpec((None, BLK, 2 * BLK), lambda p, h: (p, 0, 0))],
        out_specs=pl.BlockSpec((None, None, 2 * BLK, 2 * BLK), lambda p, h: (p, h, 0, 0)),
        out_shape=jax.ShapeDtypeStruct((n_pat, DIL_HEADS // 2, 2 * BLK, 2 * BLK), F32),
        compiler_params=pltpu.CompilerParams(dimension_semantics=("arbitrary", "arbitrary")),
        name="bias_table",
    )(rel_bias, bkt)


def _attn_kernel(q1, k1, v1, q4, k4, v4, q16, k16, v16, bias_ref, o_ref,
                 s_ref, mb_ref, u_ref, w_ref, m_ref, *, seq):
    lo_half = lax.broadcasted_iota(jnp.int32, (BLK, LANES), 1) < DIL_HD
    refs = ((q1, k1, v1), (q4, k4, v4), (q16, k16, v16))
    nblk = seq // BLK

    def rows_of(ref, p, g, with_prev=False):
        r, n = divmod(g, nblk // DILATIONS[p])
        return ref[r, pl.ds((n - 1) * BLK, 2 * BLK) if with_prev else pl.ds(n * BLK, BLK), :]

    one_trip = jnp.minimum(pl.program_id(0) + 1, 1)

    def region(fn):
        lax.fori_loop(0, one_trip, lambda i, c: (fn(), c)[1], 0)

    def natural_rows(p, g):
        d = DILATIONS[p]
        r, n = divmod(g, nblk // d)
        return pl.ds(n * (BLK * d) + r, BLK, stride=d) if d > 1 else pl.ds(g * BLK, BLK)

    def scores(p):
        q_ref, k_ref, _ = refs[p]
        nb = nblk // DILATIONS[p]
        for g in range(nblk):
            has_prev = g % nb > 0
            nk = 2 * BLK if has_prev else BLK
            bias = bias_ref[p] if has_prev else bias_ref[p, :, BLK:]
            q = rows_of(q_ref, p, g)
            s = _dot_nt(_split_heads(q, lo_half), rows_of(k_ref, p, g, has_prev)) + bias
            s_ref[p * nblk + g, :, :nk] = s
            mb = jnp.broadcast_to(jnp.max(s, axis=-1, keepdims=True), (2 * BLK, LANES))
            mb_ref[p * nblk + g] = mb
            m_ref[p, natural_rows(p, g), :] = jnp.where(lo_half, mb[:BLK], mb[BLK:])

    def outputs(p):
        _, _, v_ref = refs[p]
        nb = nblk // DILATIONS[p]
        for g in range(nblk):
            has_prev = g % nb > 0
            nk = 2 * BLK if has_prev else BLK
            idx = p * nblk + g
            e = jnp.concatenate(
                [jnp.exp2(s_ref[idx, pl.ds(h * BLK, BLK), c * LANES:(c + 1) * LANES]
                          - mb_ref[idx, pl.ds(h * BLK, BLK), :])
                 for h in range(2) for c in range(nk // LANES)], axis=1).astype(BF16)
            vv = rows_of(v_ref, p, g, has_prev)
            lo_k = lax.broadcasted_iota(jnp.int32, vv.shape, 1) < DIL_HD
            zero = jnp.zeros_like(vv)
            lane = lax.broadcasted_iota(jnp.int32, vv.shape, 1)
            sum_a = jnp.where(lane < DIL_HD, 1.0, 0.0).astype(BF16)
            sum_b = jnp.where(lane < DIL_HD, 0.0, 1.0).astype(BF16)
            rhs = jnp.concatenate(
                [jnp.concatenate([jnp.where(lo_k, vv, zero), sum_a], axis=1),
                 jnp.concatenate([jnp.where(lo_k, zero, vv), sum_b], axis=1)], axis=0)
            uw = jnp.dot(e, rhs, preferred_element_type=F32)
            dst = natural_rows(p, g)
            u_ref[p, dst, :] = uw[:, :LANES]
            w_ref[p, dst, :] = uw[:, LANES:]

    for p in range(len(DILATIONS)):
        region(functools.partial(scores, p))
        region(functools.partial(outputs, p))

    chunk = 2 * BLK

    def combine(i, carry):
        rows = pl.ds(pl.multiple_of(i * chunk, chunk), chunk)
        ms = [m_ref[p, rows, :] for p in range(len(DILATIONS))]
        mmax = functools.reduce(jnp.maximum, ms)
        cs = [jnp.exp2(mp - mmax) for mp in ms]
        num = sum(c * u_ref[p, rows, :] for p, c in enumerate(cs))
        den = sum(c * w_ref[p, rows, :] for p, c in enumerate(cs))
        o_ref[rows, :] = (num / den).astype(o_ref.dtype)
        return carry

    lax.fori_loop(0, seq // chunk, combine, 0)


def _dil_attn(a1, a4, a16, bias, *, batch, seq):
    n_pat = len(DILATIONS)
    n_pair = DIL_HEADS // 2
    arrs = (a1, a4, a16)

    def spec(d, part):
        return pl.BlockSpec((None, d, None, seq // d, LANES),
                            lambda h, b: (b, 0, part * n_pair + h, 0, 0))

    in_specs = [spec(d, part) for d in DILATIONS for part in range(3)]
    in_specs.append(pl.BlockSpec((n_pat, None, 2 * BLK, 2 * BLK), lambda h, b: (0, h, 0, 0)))
    args = [a for a in arrs for _ in range(3)] + [bias]
    return pl.pallas_call(
        functools.partial(_attn_kernel, seq=seq),
        grid=(n_pair, batch),
        in_specs=in_specs,
        out_specs=pl.BlockSpec((None, seq, LANES), lambda h, b: (b, 0, h)),
        out_shape=jax.ShapeDtypeStruct((batch, seq, DIL_W), BF16),
        scratch_shapes=[pltpu.VMEM((n_pat * seq // BLK, 2 * BLK, 2 * BLK), F32),
                        pltpu.VMEM((n_pat * seq // BLK, 2 * BLK, LANES), F32)]
        + [pltpu.VMEM((n_pat, seq, LANES), F32) for _ in range(3)],
        compiler_params=pltpu.CompilerParams(
            dimension_semantics=("arbitrary", "arbitrary"), vmem_limit_bytes=VMEM_LIMIT),
        name="dil_attn",
    )(*args)


def _post_kernel(h_ref, og_ref, od_ref, p_ref, wo_ref, w1_ref, w2_ref, wg_ref, wp_ref,
                 g_mix_ref, g_pre_ref, g_post_ref, o_ref, *, ff_chunk):
    mix = (jnp.dot(og_ref[...], wo_ref[:GLA_W, :], preferred_element_type=F32)
           + jnp.dot(od_ref[...], wo_ref[GLA_W:, :], preferred_element_type=F32))
    emb = jnp.dot(p_ref[...].astype(BF16), wp_ref[...], preferred_element_type=F32)
    h1 = h_ref[...] + _rms(mix, g_mix_ref[...])
    xn = _rms(h1, g_pre_ref[...]).astype(BF16)
    f = jnp.zeros_like(h1)
    for c in range(w1_ref.shape[1] // ff_chunk):
        cols = slice(c * ff_chunk, (c + 1) * ff_chunk)
        a = jnp.maximum(jnp.dot(xn, w1_ref[:, cols], preferred_element_type=F32), 0.0)
        f = f + jnp.dot((a * a).astype(BF16), w2_ref[cols, :], preferred_element_type=F32)
    h2 = h1 + _rms(f, g_post_ref[...])
    gate = jnp.dot(h2.astype(BF16), wg_ref[...], preferred_element_type=F32)
    o_ref[...] = h2 + emb / (1.0 + jnp.exp(-gate))


def _post(h, og, od, p, wo, w1, w2, wg, wp, g_mix, g_pre, g_post, *, layer, tm, ff_chunk=1024):
    t, dm = h.shape
    row = lambda width: pl.BlockSpec((tm, width), lambda i: (i, 0))
    const = lambda a: pl.BlockSpec((None,) + a.shape[1:], lambda i: (layer, 0, 0),
                                   pipeline_mode=pl.Buffered(1))
    consts = (wo, w1, w2, wg, wp, g_mix, g_pre, g_post)
    return pl.pallas_call(
        functools.partial(_post_kernel, ff_chunk=ff_chunk),
        grid=(t // tm,),
        in_specs=[row(dm), row(GLA_W), row(DIL_W),
                  pl.BlockSpec((None, tm, p.shape[2]), lambda i: (layer, i, 0))]
        + [const(a) for a in consts],
        out_specs=row(dm),
        out_shape=jax.ShapeDtypeStruct((t, dm), F32),
        compiler_params=pltpu.CompilerParams(
            dimension_semantics=("arbitrary",), vmem_limit_bytes=VMEM_LIMIT),
        name="post",
    )(h, og, od, p, *consts)


def kernel(x, p, w_in, w_gla_a2, b_gla_a, gla_norm_g, w_out, rel_bias, pre_mix_g, post_mix_g,
           pre_mlp_g, post_mlp_g, w_mlp_in, w_mlp_out, w_ple_gate, w_ple_proj):
    batch, seq, dm = x.shape
    depth = w_in.shape[0]
    assert seq % (max(DILATIONS) * BLK) == 0 and seq % MACRO == 0
    t = batch * seq
    rows1 = lambda a: a.reshape(depth, 1, -1)
    bias = _bias_table(rel_bias)
    h = x.reshape(t, dm)
    w = _prep_w_in(w_in)
    wa2 = jnp.pad(w_gla_a2, ((0, 0), (0, C_END - C_LR - GLA_LOWRANK), (0, 0))).astype(BF16)
    wo, w1, w2, wg, wp = (a.astype(BF16) for a in (w_out, w_mlp_in, w_mlp_out, w_ple_gate, w_ple_proj))
    p2 = p.reshape(depth, t, -1)
    for i in range(depth):
        gq, gk, la, gv, gg, a1, a4, a16 = _in_proj(
            h, rows1(pre_mix_g), w, wa2, rows1(b_gla_a), layer=i, batch=batch, seq=seq, tm=512)
        og = _gla(gq, gk, la, gv, gg, gla_norm_g[i].reshape(1, -1), batch=batch, seq=seq)
        od = _dil_attn(a1, a4, a16, bias, batch=batch, seq=seq)
        h = _post(h, og.reshape(t, GLA_W), od.reshape(t, DIL_W), p2, wo, w1, w2, wg, wp,
                  rows1(post_mix_g), rows1(pre_mlp_g), rows1(post_mlp_g), layer=i, tm=512)
    return h.reshape(batch, seq, dm)
```

```python
import functools
import math

import numpy as np
import jax
import jax.numpy as jnp
from jax import lax
from jax.experimental import pallas as pl
from jax.experimental.pallas import tpu as pltpu

F32 = jnp.float32
BF16 = jnp.bfloat16

EPS = 1e-6
GLA_HEADS = 4
GLA_DK = 64
GLA_DV = 128
GLA_QK = GLA_HEADS * GLA_DK
GLA_W = GLA_HEADS * GLA_DV
GLA_LOWRANK = 16
GLA_TAU = 16.0
DIL_HEADS = 8
DIL_HD = 64
DIL_W = DIL_HEADS * DIL_HD
DILATIONS = (1, 4, 16)
BAND = 128
BLK = 128
REL_BUCKETS = 32
REL_MAX_DIST = 2048
NEG = -1e30
LOG2E = math.log2(math.e)

LANES = 128
MACRO = 128
GLA_LEVELS = 7
VMEM_LIMIT = 56 * 1024 * 1024

C_GQ, C_GK, C_GV, C_GG, C_DQ, C_LR, C_END = 0, 256, 512, 1024, 1536, 3072, 3200


def _rms(x, g):
    return x * lax.rsqrt(jnp.mean(x * x, axis=-1, keepdims=True) + EPS) * g


def _w_in_kernel(wt_ref, o_ref, *, lr0):
    lr1 = lr0 + GLA_LOWRANK
    tk = wt_ref.shape[1]
    o_ref[:, :lr0] = wt_ref[:lr0, :].T.astype(BF16)
    o_ref[:, lr0:C_LR] = wt_ref[lr1:, :].T.astype(BF16)
    o_ref[:, C_LR:] = jnp.concatenate(
        [wt_ref[lr0:lr1, :].T, jnp.zeros((tk, C_END - C_LR - GLA_LOWRANK), F32)], axis=1).astype(BF16)


def _prep_w_in(w_in, *, tk=256):
    depth, dm, width = w_in.shape
    lr0 = C_DQ
    assert width - GLA_LOWRANK == C_LR
    return pl.pallas_call(
        functools.partial(_w_in_kernel, lr0=lr0),
        grid=(depth, dm // tk),
        in_specs=[pl.BlockSpec((None, width, tk), lambda l, i: (l, 0, i))],
        out_specs=pl.BlockSpec((None, tk, C_END), lambda l, i: (l, i, 0)),
        out_shape=jax.ShapeDtypeStruct((depth, dm, C_END), BF16),
        compiler_params=pltpu.CompilerParams(dimension_semantics=("arbitrary", "arbitrary")),
        name="prep_w_in",
    )(jnp.swapaxes(w_in, 1, 2))


def _in_proj_kernel(h_ref, g_ref, w_ref, wa2_ref, ba_ref,
                    gq_ref, gk_ref, la_ref, gv_ref, gg_ref, a1_ref, a4_ref, a16_ref,
                    slab_ref, slab4_ref, *, tm):
    xn = _rms(h_ref[...], g_ref[...]).astype(BF16)

    def proj(lo, hi):
        return jnp.dot(xn, w_ref[:, lo:hi], preferred_element_type=F32)

    for c in range(3):
        y = proj(C_DQ + DIL_W * c, C_DQ + DIL_W * (c + 1))
        if c == 0:
            y = y * (LOG2E * DIL_HD ** -0.5)
        for s in range(DIL_W // LANES):
            ys = y[:, LANES * s:LANES * (s + 1)]
            idx = c * (DIL_W // LANES) + s
            a1_ref[0, idx] = ys.astype(BF16)
            slab_ref[idx] = ys
            for r in range(4):
                y4 = slab_ref[idx, pl.ds(r, tm // 4, stride=4), :]
                a4_ref[r, idx] = y4.astype(BF16)
                slab4_ref[idx, r] = y4
            for r in range(16):
                a16_ref[r, idx] = slab4_ref[idx, r % 4, pl.ds(r // 4, tm // 16, stride=4), :].astype(BF16)

    gq_ref[...] = (proj(C_GQ, C_GK) * (GLA_DK ** -0.5)).astype(BF16)
    gk_ref[...] = proj(C_GK, C_GV).astype(BF16)
    gv_ref[...] = proj(C_GV, C_GG).astype(BF16)
    gg_ref[...] = proj(C_GG, C_DQ).astype(BF16)
    z = jnp.dot(proj(C_LR, C_END).astype(BF16), wa2_ref[...],
                preferred_element_type=F32) + ba_ref[...]
    la_ref[...] = (jnp.minimum(z, 0.0) - jnp.log1p(jnp.exp(-jnp.abs(z)))) * (1.0 / GLA_TAU)


def _in_proj(h, g, w, wa2, ba, *, layer, batch, seq, tm):
    t = batch * seq
    nt = seq // tm
    row = lambda width: pl.BlockSpec((tm, width), lambda b, i: (b * nt + i, 0))
    const = lambda shape: pl.BlockSpec((None,) + shape[1:], lambda b, i: (layer, 0, 0))
    n_slab = 3 * DIL_W // LANES
    dil_spec = lambda d: pl.BlockSpec((None, d, n_slab, tm // d, LANES), lambda b, i: (b, 0, 0, i, 0))
    out_shape = (
        jax.ShapeDtypeStruct((t, GLA_QK), BF16), jax.ShapeDtypeStruct((t, GLA_QK), BF16),
        jax.ShapeDtypeStruct((t, GLA_QK), F32),
        jax.ShapeDtypeStruct((t, GLA_W), BF16), jax.ShapeDtypeStruct((t, GLA_W), BF16),
    ) + tuple(jax.ShapeDtypeStruct((batch, d, n_slab, seq // d, LANES), BF16) for d in DILATIONS)
    return pl.pallas_call(
        functools.partial(_in_proj_kernel, tm=tm),
        grid=(batch, nt),
        in_specs=[row(h.shape[1]), const(g.shape), const(w.shape), const(wa2.shape), const(ba.shape)],
        out_specs=(row(GLA_QK), row(GLA_QK), row(GLA_QK), row(GLA_W), row(GLA_W))
        + tuple(dil_spec(d) for d in DILATIONS),
        out_shape=out_shape,
        scratch_shapes=[pltpu.VMEM((n_slab, tm, LANES), F32),
                        pltpu.VMEM((n_slab, 4, tm // 4, LANES), F32)],
        compiler_params=pltpu.CompilerParams(
            dimension_semantics=("arbitrary", "arbitrary"), vmem_limit_bytes=VMEM_LIMIT),
        name="in_proj",
    )(h, g, w, wa2, ba)


def _gla_constants():
    n = MACRO
    i = np.arange(n)[:, None]
    t = np.arange(n)[None, :]
    blocks = [t <= i, t > i]
    for l in range(GLA_LEVELS):
        ref = ((i >> (l + 1)) << (l + 1)) + (1 << l) - 1
        odd = ((i >> l) & 1) == 1
        blocks.append(np.where(odd, (t > ref) & (t <= i), (t > i) & (t <= ref)))
    p = np.concatenate(blocks, axis=0).astype(np.float32)
    j = np.arange(n)[None, :]
    lvl = np.where(j < i, np.floor(np.log2(np.maximum(i ^ j, 1))).astype(np.int32),
                   np.where(j == i, GLA_LEVELS, -1)).astype(np.int32)
    return p, np.concatenate([lvl, lvl], axis=0)


def _split_heads(x, lo_half):
    zero = jnp.zeros_like(x)
    return jnp.concatenate([jnp.where(lo_half, x, zero), jnp.where(lo_half, zero, x)], axis=0)


def _dot_nt(a, b):
    return lax.dot_general(a, b, (((1,), (1,)), ((), ())), preferred_element_type=F32)


def _dot_tn(a, b):
    return lax.dot_general(a, b, (((0,), (0,)), ((), ())), preferred_element_type=F32)


def _gla_kernel(q_ref, k_ref, la_ref, v_ref, gg_ref, p_ref, lvl_ref, gn_ref, o_ref, st_ref,
                *, n_macro, unroll):
    st_ref[...] = jnp.zeros_like(st_ref)
    lo_half = lax.broadcasted_iota(jnp.int32, (MACRO, LANES), 1) < (LANES // 2)

    def body(m, carry):
        r0 = pl.multiple_of(m * MACRO, MACRO)
        rows = pl.ds(r0, MACRO)
        q = q_ref[rows, :].astype(F32)
        k = k_ref[rows, :].astype(F32)
        la = la_ref[rows, :]
        la_hi = la.astype(BF16)
        la2 = jnp.concatenate([la_hi, (la - la_hi.astype(F32)).astype(BF16)], axis=1)

        def decay(blk):
            r = jnp.dot(p_ref[blk * MACRO:(blk + 1) * MACRO, :], la2, preferred_element_type=F32)
            return jnp.exp(r[:, :GLA_QK] + r[:, GLA_QK:])

        x_start = decay(0)
        qg = q * x_start
        kg = k * decay(1)
        a_last = x_start[MACRO - 1:MACRO, :]
        lvl = lvl_ref[...]
        xs = [decay(2 + l) for l in range(GLA_LEVELS)]
        for pair in range(GLA_HEADS // 2):
            sl = slice(LANES * pair, LANES * (pair + 1))
            att = _dot_nt(_split_heads(q[:, sl], lo_half).astype(BF16), k[:, sl].astype(BF16))
            att = jnp.where(lvl == GLA_LEVELS, att, 0.0)
            for l in range(GLA_LEVELS):
                xl = xs[l][:, sl]
                a = _dot_nt(_split_heads(q[:, sl] * xl, lo_half).astype(BF16),
                            (k[:, sl] * xl).astype(BF16))
                att = jnp.where(lvl == l, a, att)
            att = att.astype(BF16)
            st = st_ref[pair]
            inter = _dot_nt(_split_heads(qg[:, sl], lo_half).astype(BF16), st.astype(BF16))
            kgp = kg[:, sl].astype(BF16)
            upd = []
            for e in range(2):
                h = 2 * pair + e
                hs = slice(GLA_DV * h, GLA_DV * (h + 1))
                vh = v_ref[rows, hs]
                o = (jnp.dot(att[e * MACRO:(e + 1) * MACRO], vh, preferred_element_type=F32)
                     + inter[e * MACRO:(e + 1) * MACRO])
                gate = gg_ref[rows, hs].astype(F32)
                o = _rms(o, gn_ref[...]) * (gate / (1.0 + jnp.exp(-gate)))
                o_ref[rows, hs] = o.astype(o_ref.dtype)
                upd.append(_dot_tn(vh, kgp))
            st_ref[pair] = st * a_last[:, sl] + jnp.where(lo_half, upd[0], upd[1])
        return carry

    lax.fori_loop(0, n_macro, body, 0, unroll=unroll)


def _gla(gq, gk, la, gv, gg, gn, *, batch, seq, unroll=2):
    p_np, lvl_np = _gla_constants()
    pm = jnp.asarray(p_np, BF16)
    lvl = jnp.asarray(lvl_np)
    seq_spec = lambda width: pl.BlockSpec((None, seq, width), lambda b: (b, 0, 0))
    const = lambda shape: pl.BlockSpec(shape, lambda b: (0,) * len(shape))
    r3 = lambda a: a.reshape(batch, seq, a.shape[-1])
    return pl.pallas_call(
        functools.partial(_gla_kernel, n_macro=seq // MACRO, unroll=unroll),
        grid=(batch,),
        in_specs=[seq_spec(GLA_QK), seq_spec(GLA_QK), seq_spec(GLA_QK), seq_spec(GLA_W),
                  seq_spec(GLA_W), const(pm.shape), const(lvl.shape), const(gn.shape)],
        out_specs=seq_spec(GLA_W),
        out_shape=jax.ShapeDtypeStruct((batch, seq, GLA_W), BF16),
        scratch_shapes=[pltpu.VMEM((GLA_HEADS // 2, GLA_DV, LANES), F32)],
        compiler_params=pltpu.CompilerParams(
            dimension_semantics=("arbitrary",), vmem_limit_bytes=VMEM_LIMIT),
        name="gla",
    )(r3(gq), r3(gk), r3(la), r3(gv), r3(gg), pm, lvl, gn)


def _t5_bucket_np(dist):
    max_exact = REL_BUCKETS // 2
    d = np.maximum(dist, 1).astype(np.float32)
    large = max_exact + (np.log(d / np.float32(max_exact)) / np.float32(math.log(REL_MAX_DIST / max_exact))
                         * np.float32(REL_BUCKETS - max_exact)).astype(np.int32)
    large = np.minimum(large, REL_BUCKETS - 1)
    return np.where(dist < max_exact, dist, large).astype(np.int32)


def _bucket_table():
    qi = np.arange(BLK)[:, None]
    ki = np.arange(2 * BLK)[None, :]
    j = qi + BLK - ki
    valid = (j >= 0) & (j <= BAND)
    return np.stack([np.where(valid, _t5_bucket_np(np.maximum(j, 0) * d), -1) for d in DILATIONS]).astype(np.int32)


def _bias_kernel(rel_ref, bkt_ref, o_ref):
    pair = pl.program_id(1)
    bkt = bkt_ref[...]
    for e in range(2):
        acc = jnp.full(bkt.shape, NEG, F32)
        for u in range(REL_BUCKETS):
            acc = jnp.where(bkt == u, rel_ref[u, 2 * pair + e] * LOG2E, acc)
        o_ref[e * BLK:(e + 1) * BLK, :] = acc


def _bias_table(rel_bias):
    bkt = jnp.asarray(_bucket_table())
    n_pat = len(DILATIONS)
    return pl.pallas_call(
        _bias_kernel,
        grid=(n_pat, DIL_HEADS // 2),
        in_specs=[pl.BlockSpec(memory_space=pltpu.SMEM),
                  pl.BlockSpec((None, BLK, 2 * BLK), lambda p, h: (p, 0, 0))],
        out_specs=pl.BlockSpec((None, None, 2 * BLK, 2 * BLK), lambda p, h: (p, h, 0, 0)),
        out_shape=jax.ShapeDtypeStruct((n_pat, DIL_HEADS // 2, 2 * BLK, 2 * BLK), F32),
        compiler_params=pltpu.CompilerParams(dimension_semantics=("arbitrary", "arbitrary")),
        name="bias_table",
    )(rel_bias, bkt)


def _attn_kernel(q1, k1, v1, q4, k4, v4, q16, k16, v16, bias_ref, o_ref,
                 s_ref, mb_ref, u_ref, w_ref, m_ref, *, seq):
    lo_half = lax.broadcasted_iota(jnp.int32, (BLK, LANES), 1) < DIL_HD
    refs = ((q1, k1, v1), (q4, k4, v4), (q16, k16, v16))
    nblk = seq // BLK

    def rows_of(ref, p, g, with_prev=False):
        r, n = divmod(g, nblk // DILATIONS[p])
        return ref[r, pl.ds((n - 1) * BLK, 2 * BLK) if with_prev else pl.ds(n * BLK, BLK), :]

    one_trip = jnp.minimum(pl.program_id(0) + 1, 1)

    def region(fn):
        lax.fori_loop(0, one_trip, lambda i, c: (fn(), c)[1], 0)

    def natural_rows(p, g):
        d = DILATIONS[p]
        r, n = divmod(g, nblk // d)
        return pl.ds(n * (BLK * d) + r, BLK, stride=d) if d > 1 else pl.ds(g * BLK, BLK)

    def scores(p):
        q_ref, k_ref, _ = refs[p]
        nb = nblk // DILATIONS[p]
        for g in range(nblk):
            has_prev = g % nb > 0
            nk = 2 * BLK if has_prev else BLK
            bias = bias_ref[p] if has_prev else bias_ref[p, :, BLK:]
            q = rows_of(q_ref, p, g)
            s = _dot_nt(_split_heads(q, lo_half), rows_of(k_ref, p, g, has_prev)) + bias
            s_ref[p * nblk + g, :, :nk] = s
            mb = jnp.broadcast_to(jnp.max(s, axis=-1, keepdims=True), (2 * BLK, LANES))
            mb_ref[p * nblk + g] = mb
            m_ref[p, natural_rows(p, g), :] = jnp.where(lo_half, mb[:BLK], mb[BLK:])

    def outputs(p):
        _, _, v_ref = refs[p]
        nb = nblk // DILATIONS[p]
        for g in range(nblk):
            has_prev = g % nb > 0
            nk = 2 * BLK if has_prev else BLK
            idx = p * nblk + g
            e = jnp.concatenate(
                [jnp.exp2(s_ref[idx, pl.ds(h * BLK, BLK), c * LANES:(c + 1) * LANES]
                          - mb_ref[idx, pl.ds(h * BLK, BLK), :])
                 for h in range(2) for c in range(nk // LANES)], axis=1).astype(BF16)
            vv = rows_of(v_ref, p, g, has_prev)
            lo_k = lax.broadcasted_iota(jnp.int32, vv.shape, 1) < DIL_HD
            zero = jnp.zeros_like(vv)
            lane = lax.broadcasted_iota(jnp.int32, vv.shape, 1)
            sum_a = jnp.where(lane < DIL_HD, 1.0, 0.0).astype(BF16)
            sum_b = jnp.where(lane < DIL_HD, 0.0, 1.0).astype(BF16)
            rhs = jnp.concatenate(
                [jnp.concatenate([jnp.where(lo_k, vv, zero), sum_a], axis=1),
                 jnp.concatenate([jnp.where(lo_k, zero, vv), sum_b], axis=1)], axis=0)
            uw = jnp.dot(e, rhs, preferred_element_type=F32)
            dst = natural_rows(p, g)
            u_ref[p, dst, :] = uw[:, :LANES]
            w_ref[p, dst, :] = uw[:, LANES:]

    order = tuple(range(len(DILATIONS)))
    region(functools.partial(scores, order[0]))
    for cur, nxt in zip(order, order[1:] + (None,)):
        def both(cur=cur, nxt=nxt):
            outputs(cur)
            if nxt is not None:
                scores(nxt)
        region(both)

    chunk = 2 * BLK

    def combine(i, carry):
        rows = pl.ds(pl.multiple_of(i * chunk, chunk), chunk)
        ms = [m_ref[p, rows, :] for p in range(len(DILATIONS))]
        mmax = functools.reduce(jnp.maximum, ms)
        cs = [jnp.exp2(mp - mmax) for mp in ms]
        num = sum(c * u_ref[p, rows, :] for p, c in enumerate(cs))
        den = sum(c * w_ref[p, rows, :] for p, c in enumerate(cs))
        o_ref[rows, :] = (num / den).astype(o_ref.dtype)
        return carry

    lax.fori_loop(0, seq // chunk, combine, 0)


def _dil_attn(a1, a4, a16, bias, *, batch, seq):
    n_pat = len(DILATIONS)
    n_pair = DIL_HEADS // 2
    arrs = (a1, a4, a16)

    def spec(d, part):
        return pl.BlockSpec((None, d, None, seq // d, LANES),
                            lambda h, b: (b, 0, part * n_pair + h, 0, 0))

    in_specs = [spec(d, part) for d in DILATIONS for part in range(3)]
    in_specs.append(pl.BlockSpec((n_pat, None, 2 * BLK, 2 * BLK), lambda h, b: (0, h, 0, 0)))
    args = [a for a in arrs for _ in range(3)] + [bias]
    return pl.pallas_call(
        functools.partial(_attn_kernel, seq=seq),
        grid=(n_pair, batch),
        in_specs=in_specs,
        out_specs=pl.BlockSpec((None, seq, LANES), lambda h, b: (b, 0, h)),
        out_shape=jax.ShapeDtypeStruct((batch, seq, DIL_W), BF16),
        scratch_shapes=[pltpu.VMEM((n_pat * seq // BLK, 2 * BLK, 2 * BLK), F32),
                        pltpu.VMEM((n_pat * seq // BLK, 2 * BLK, LANES), F32)]
        + [pltpu.VMEM((n_pat, seq, LANES), F32) for _ in range(3)],
        compiler_params=pltpu.CompilerParams(
            dimension_semantics=("arbitrary", "arbitrary"), vmem_limit_bytes=VMEM_LIMIT),
        name="dil_attn",
    )(*args)


def _post_kernel(h_ref, og_ref, od_ref, p_ref, wo_ref, w1_ref, w2_ref, wg_ref, wp_ref,
                 g_mix_ref, g_pre_ref, g_post_ref, o_ref, *, ff_chunk):
    mix = (jnp.dot(og_ref[...], wo_ref[:GLA_W, :], preferred_element_type=F32)
           + jnp.dot(od_ref[...], wo_ref[GLA_W:, :], preferred_element_type=F32))
    emb = jnp.dot(p_ref[...].astype(BF16), wp_ref[...], preferred_element_type=F32)
    h1 = h_ref[...] + _rms(mix, g_mix_ref[...])
    xn = _rms(h1, g_pre_ref[...]).astype(BF16)
    f = jnp.zeros_like(h1)
    for c in range(w1_ref.shape[1] // ff_chunk):
        cols = slice(c * ff_chunk, (c + 1) * ff_chunk)
        a = jnp.maximum(jnp.dot(xn, w1_ref[:, cols], preferred_element_type=F32), 0.0)
        f = f + jnp.dot((a * a).astype(BF16), w2_ref[cols, :], preferred_element_type=F32)
    h2 = h1 + _rms(f, g_post_ref[...])
    gate = jnp.dot(h2.astype(BF16), wg_ref[...], preferred_element_type=F32)
    o_ref[...] = h2 + emb / (1.0 + jnp.exp(-gate))


def _post(h, og, od, p, wo, w1, w2, wg, wp, g_mix, g_pre, g_post, *, layer, tm, ff_chunk=1024):
    t, dm = h.shape
    row = lambda width: pl.BlockSpec((tm, width), lambda i: (i, 0))
    const = lambda a: pl.BlockSpec((None,) + a.shape[1:], lambda i: (layer, 0, 0),
                                   pipeline_mode=pl.Buffered(1))
    consts = (wo, w1, w2, wg, wp, g_mix, g_pre, g_post)
    return pl.pallas_call(
        functools.partial(_post_kernel, ff_chunk=ff_chunk),
        grid=(t // tm,),
        in_specs=[row(dm), row(GLA_W), row(DIL_W),
                  pl.BlockSpec((None, tm, p.shape[2]), lambda i: (layer, i, 0))]
        + [const(a) for a in consts],
        out_specs=row(dm),
        out_shape=jax.ShapeDtypeStruct((t, dm), F32),
        compiler_params=pltpu.CompilerParams(
            dimension_semantics=("arbitrary",), vmem_limit_bytes=VMEM_LIMIT),
        name="post",
    )(h, og, od, p, *consts)


def kernel(x, p, w_in, w_gla_a2, b_gla_a, gla_norm_g, w_out, rel_bias, pre_mix_g, post_mix_g,
           pre_mlp_g, post_mlp_g, w_mlp_in, w_mlp_out, w_ple_gate, w_ple_proj):
    batch, seq, dm = x.shape
    depth = w_in.shape[0]
    assert seq % (max(DILATIONS) * BLK) == 0 and seq % MACRO == 0
    t = batch * seq
    rows1 = lambda a: a.reshape(depth, 1, -1)
    bias = _bias_table(rel_bias)
    h = x.reshape(t, dm)
    w = _prep_w_in(w_in)
    wa2 = jnp.pad(w_gla_a2, ((0, 0), (0, C_END - C_LR - GLA_LOWRANK), (0, 0))).astype(BF16)
    wo, w1, w2, wg, wp = (a.astype(BF16) for a in (w_out, w_mlp_in, w_mlp_out, w_ple_gate, w_ple_proj))
    p2 = p.reshape(depth, t, -1)
    for i in range(depth):
        gq, gk, la, gv, gg, a1, a4, a16 = _in_proj(
            h, rows1(pre_mix_g), w, wa2, rows1(b_gla_a), layer=i, batch=batch, seq=seq, tm=512)
        og = _gla(gq, gk, la, gv, gg, gla_norm_g[i].reshape(1, -1), batch=batch, seq=seq)
        od = _dil_attn(a1, a4, a16, bias, batch=batch, seq=seq)
        h = _post(h, og.reshape(t, GLA_W), od.reshape(t, DIL_W), p2, wo, w1, w2, wg, wp,
                  rows1(post_mix_g), rows1(pre_mlp_g), rows1(post_mlp_g), layer=i, tm=512)
    return h.reshape(batch, seq, dm)
```

```python
import functools
import math

import numpy as np
import jax
import jax.numpy as jnp
from jax import lax
from jax.experimental import pallas as pl
from jax.experimental.pallas import tpu as pltpu

F32 = jnp.float32
BF16 = jnp.bfloat16

EPS = 1e-6
GLA_HEADS = 4
GLA_DK = 64
GLA_DV = 128
GLA_QK = GLA_HEADS * GLA_DK
GLA_W = GLA_HEADS * GLA_DV
GLA_LOWRANK = 16
GLA_TAU = 16.0
DIL_HEADS = 8
DIL_HD = 64
DIL_W = DIL_HEADS * DIL_HD
DILATIONS = (1, 4, 16)
BAND = 128
BLK = 128
REL_BUCKETS = 32
REL_MAX_DIST = 2048
NEG = -1e30
LOG2E = math.log2(math.e)

LANES = 128
MACRO = 128
GLA_LEVELS = 7
VMEM_LIMIT = 56 * 1024 * 1024

C_GQ, C_GK, C_GV, C_GG, C_DQ, C_LR, C_END = 0, 256, 512, 1024, 1536, 3072, 3200


def _rms(x, g):
    return x * lax.rsqrt(jnp.mean(x * x, axis=-1, keepdims=True) + EPS) * g


def _w_in_kernel(wt_ref, o_ref, *, lr0):
    lr1 = lr0 + GLA_LOWRANK
    tk = wt_ref.shape[1]
    o_ref[:, :lr0] = wt_ref[:lr0, :].T.astype(BF16)
    o_ref[:, lr0:C_LR] = wt_ref[lr1:, :].T.astype(BF16)
    o_ref[:, C_LR:] = jnp.concatenate(
        [wt_ref[lr0:lr1, :].T, jnp.zeros((tk, C_END - C_LR - GLA_LOWRANK), F32)], axis=1).astype(BF16)


def _prep_w_in(w_in, *, tk=256):
    depth, dm, width = w_in.shape
    lr0 = C_DQ
    assert width - GLA_LOWRANK == C_LR
    return pl.pallas_call(
        functools.partial(_w_in_kernel, lr0=lr0),
        grid=(depth, dm // tk),
        in_specs=[pl.BlockSpec((None, width, tk), lambda l, i: (l, 0, i))],
        out_specs=pl.BlockSpec((None, tk, C_END), lambda l, i: (l, i, 0)),
        out_shape=jax.ShapeDtypeStruct((depth, dm, C_END), BF16),
        compiler_params=pltpu.CompilerParams(dimension_semantics=("arbitrary", "arbitrary")),
        name="prep_w_in",
    )(jnp.swapaxes(w_in, 1, 2))


def _in_proj_kernel(h_ref, g_ref, w_ref, wa2_ref, ba_ref,
                    gq_ref, gk_ref, la_ref, gv_ref, gg_ref, a1_ref, a4_ref, a16_ref,
                    slab_ref, slab4_ref, *, tm):
    xn = _rms(h_ref[...], g_ref[...]).astype(BF16)

    def proj(lo, hi):
        return jnp.dot(xn, w_ref[:, lo:hi], preferred_element_type=F32)

    for c in range(3):
        y = proj(C_DQ + DIL_W * c, C_DQ + DIL_W * (c + 1))
        if c == 0:
            y = y * (LOG2E * DIL_HD ** -0.5)
        for s in range(DIL_W // LANES):
            ys = y[:, LANES * s:LANES * (s + 1)]
            idx = c * (DIL_W // LANES) + s
            a1_ref[0, idx] = ys.astype(BF16)
            slab_ref[idx] = ys
            for r in range(4):
                y4 = slab_ref[idx, pl.ds(r, tm // 4, stride=4), :]
                a4_ref[r, idx] = y4.astype(BF16)
                slab4_ref[idx, r] = y4
            for r in range(16):
                a16_ref[r, idx] = slab4_ref[idx, r % 4, pl.ds(r // 4, tm // 16, stride=4), :].astype(BF16)

    gq_ref[...] = (proj(C_GQ, C_GK) * (GLA_DK ** -0.5)).astype(BF16)
    gk_ref[...] = proj(C_GK, C_GV).astype(BF16)
    gv_ref[...] = proj(C_GV, C_GG).astype(BF16)
    gg_ref[...] = proj(C_GG, C_DQ).astype(BF16)
    z = jnp.dot(proj(C_LR, C_END).astype(BF16), wa2_ref[...],
                preferred_element_type=F32) + ba_ref[...]
    la_ref[...] = (jnp.minimum(z, 0.0) - jnp.log1p(jnp.exp(-jnp.abs(z)))) * (1.0 / GLA_TAU)


def _in_proj(h, g, w, wa2, ba, *, layer, batch, seq, tm):
    t = batch * seq
    nt = seq // tm
    row = lambda width: pl.BlockSpec((tm, width), lambda b, i: (b * nt + i, 0))
    const = lambda shape: pl.BlockSpec((None,) + shape[1:], lambda b, i: (layer, 0, 0))
    n_slab = 3 * DIL_W // LANES
    dil_spec = lambda d: pl.BlockSpec((None, d, n_slab, tm // d, LANES), lambda b, i: (b, 0, 0, i, 0))
    out_shape = (
        jax.ShapeDtypeStruct((t, GLA_QK), BF16), jax.ShapeDtypeStruct((t, GLA_QK), BF16),
        jax.ShapeDtypeStruct((t, GLA_QK), F32),
        jax.ShapeDtypeStruct((t, GLA_W), BF16), jax.ShapeDtypeStruct((t, GLA_W), BF16),
    ) + tuple(jax.ShapeDtypeStruct((batch, d, n_slab, seq // d, LANES), BF16) for d in DILATIONS)
    return pl.pallas_call(
        functools.partial(_in_proj_kernel, tm=tm),
        grid=(batch, nt),
        in_specs=[row(h.shape[1]), const(g.shape), const(w.shape), const(wa2.shape), const(ba.shape)],
        out_specs=(row(GLA_QK), row(GLA_QK), row(GLA_QK), row(GLA_W), row(GLA_W))
        + tuple(dil_spec(d) for d in DILATIONS),
        out_shape=out_shape,
        scratch_shapes=[pltpu.VMEM((n_slab, tm, LANES), F32),
                        pltpu.VMEM((n_slab, 4, tm // 4, LANES), F32)],
        compiler_params=pltpu.CompilerParams(
            dimension_semantics=("arbitrary", "arbitrary"), vmem_limit_bytes=VMEM_LIMIT),
        name="in_proj",
    )(h, g, w, wa2, ba)


def _gla_constants():
    n = MACRO
    i = np.arange(n)[:, None]
    t = np.arange(n)[None, :]
    p = np.concatenate([t <= i, t > i], axis=0).astype(np.float32)
    j = np.arange(n)[None, :]
    lvl = np.where(j < i, np.floor(np.log2(np.maximum(i ^ j, 1))).astype(np.int32),
                   np.where(j == i, GLA_LEVELS, -1)).astype(np.int32)
    return p, np.concatenate([lvl, lvl], axis=0)


def _split_heads(x, lo_half):
    zero = jnp.zeros_like(x)
    return jnp.concatenate([jnp.where(lo_half, x, zero), jnp.where(lo_half, zero, x)], axis=0)


def _dot_nt(a, b):
    return lax.dot_general(a, b, (((1,), (1,)), ((), ())), preferred_element_type=F32)


def _dot_tn(a, b):
    return lax.dot_general(a, b, (((0,), (0,)), ((), ())), preferred_element_type=F32)


def _gla_kernel(q_ref, k_ref, la_ref, v_ref, gg_ref, p_ref, lvl_ref, gn_ref, o_ref, st_ref,
                *, n_macro, unroll):
    st_ref[...] = jnp.zeros_like(st_ref)
    lo_half = lax.broadcasted_iota(jnp.int32, (MACRO, LANES), 1) < (LANES // 2)

    def body(m, carry):
        r0 = pl.multiple_of(m * MACRO, MACRO)
        rows = pl.ds(r0, MACRO)
        q = q_ref[rows, :].astype(F32)
        k = k_ref[rows, :].astype(F32)
        la = la_ref[rows, :]
        la_hi = la.astype(BF16)
        la2 = jnp.concatenate([la_hi, (la - la_hi.astype(F32)).astype(BF16)], axis=1)

        def range_sum(blk):
            r = jnp.dot(p_ref[blk * MACRO:(blk + 1) * MACRO, :], la2, preferred_element_type=F32)
            return r[:, :GLA_QK] + r[:, GLA_QK:]

        b = range_sum(0)
        x_start = jnp.exp(b)
        qg = q * x_start
        kg = k * jnp.exp(range_sum(1))
        a_last = x_start[MACRO - 1:MACRO, :]
        lvl = lvl_ref[...]

        @functools.cache
        def row_bcast(r):
            return jnp.broadcast_to(b[r:r + 1, :], (8, GLA_QK))

        sub = lax.broadcasted_iota(jnp.int32, (8, GLA_QK), 0)

        def ref_rows(l):
            span = 2 << l
            pieces = []
            for g in range(MACRO // 8):
                refs = sorted({(i // span) * span + span // 2 - 1 for i in range(8 * g, 8 * g + 8)})
                piece = row_bcast(refs[-1])
                for c in reversed(range(len(refs) - 1)):
                    piece = jnp.where(sub < (c + 1) * span, row_bcast(refs[c]), piece)
                pieces.append(piece)
            return jnp.concatenate(pieces, axis=0)

        xs = [jnp.exp(-jnp.abs(b - ref_rows(l))) for l in range(GLA_LEVELS)]
        for pair in range(GLA_HEADS // 2):
            sl = slice(LANES * pair, LANES * (pair + 1))
            att = _dot_nt(_split_heads(q[:, sl], lo_half).astype(BF16), k[:, sl].astype(BF16))
            att = jnp.where(lvl == GLA_LEVELS, att, 0.0)
            for l in range(GLA_LEVELS):
                xl = xs[l][:, sl]
                a = _dot_nt(_split_heads(q[:, sl] * xl, lo_half).astype(BF16),
                            (k[:, sl] * xl).astype(BF16))
                att = jnp.where(lvl == l, a, att)
            att = att.astype(BF16)
            st = st_ref[pair]
            inter = _dot_nt(_split_heads(qg[:, sl], lo_half).astype(BF16), st.astype(BF16))
            kgp = kg[:, sl].astype(BF16)
            upd = []
            for e in range(2):
                h = 2 * pair + e
                hs = slice(GLA_DV * h, GLA_DV * (h + 1))
                vh = v_ref[rows, hs]
                o = (jnp.dot(att[e * MACRO:(e + 1) * MACRO], vh, preferred_element_type=F32)
                     + inter[e * MACRO:(e + 1) * MACRO])
                gate = gg_ref[rows, hs].astype(F32)
                o = _rms(o, gn_ref[...]) * (gate / (1.0 + jnp.exp(-gate)))
                o_ref[rows, hs] = o.astype(o_ref.dtype)
                upd.append(_dot_tn(vh, kgp))
            st_ref[pair] = st * a_last[:, sl] + jnp.where(lo_half, upd[0], upd[1])
        return carry

    lax.fori_loop(0, n_macro, body, 0, unroll=unroll)


def _gla(gq, gk, la, gv, gg, gn, *, batch, seq, unroll=2):
    p_np, lvl_np = _gla_constants()
    pm = jnp.asarray(p_np, BF16)
    lvl = jnp.asarray(lvl_np)
    seq_spec = lambda width: pl.BlockSpec((None, seq, width), lambda b: (b, 0, 0))
    const = lambda shape: pl.BlockSpec(shape, lambda b: (0,) * len(shape))
    r3 = lambda a: a.reshape(batch, seq, a.shape[-1])
    return pl.pallas_call(
        functools.partial(_gla_kernel, n_macro=seq // MACRO, unroll=unroll),
        grid=(batch,),
        in_specs=[seq_spec(GLA_QK), seq_spec(GLA_QK), seq_spec(GLA_QK), seq_spec(GLA_W),
                  seq_spec(GLA_W), const(pm.shape), const(lvl.shape), const(gn.shape)],
        out_specs=seq_spec(GLA_W),
        out_shape=jax.ShapeDtypeStruct((batch, seq, GLA_W), BF16),
        scratch_shapes=[pltpu.VMEM((GLA_HEADS // 2, GLA_DV, LANES), F32)],
        compiler_params=pltpu.CompilerParams(
            dimension_semantics=("arbitrary",), vmem_limit_bytes=VMEM_LIMIT),
        name="gla",
    )(r3(gq), r3(gk), r3(la), r3(gv), r3(gg), pm, lvl, gn)


def _t5_bucket_np(dist):
    max_exact = REL_BUCKETS // 2
    d = np.maximum(dist, 1).astype(np.float32)
    large = max_exact + (np.log(d / np.float32(max_exact)) / np.float32(math.log(REL_MAX_DIST / max_exact))
                         * np.float32(REL_BUCKETS - max_exact)).astype(np.int32)
    large = np.minimum(large, REL_BUCKETS - 1)
    return np.where(dist < max_exact, dist, large).astype(np.int32)


def _bucket_table():
    qi = np.arange(BLK)[:, None]
    ki = np.arange(2 * BLK)[None, :]
    j = qi + BLK - ki
    valid = (j >= 0) & (j <= BAND)
    return np.stack([np.where(valid, _t5_bucket_np(np.maximum(j, 0) * d), -1) for d in DILATIONS]).astype(np.int32)


def _bias_kernel(rel_ref, bkt_ref, o_ref):
    pair = pl.program_id(1)
    bkt = bkt_ref[...]
    for e in range(2):
        acc = jnp.full(bkt.shape, NEG, F32)
        for u in range(REL_BUCKETS):
            acc = jnp.where(bkt == u, rel_ref[u, 2 * pair + e] * LOG2E, acc)
        o_ref[e * BLK:(e + 1) * BLK, :] = acc


def _bias_table(rel_bias):
    bkt = jnp.asarray(_bucket_table())
    n_pat = len(DILATIONS)
    return pl.pallas_call(
        _bias_kernel,
        grid=(n_pat, DIL_HEADS // 2),
        in_specs=[pl.BlockSpec(memory_space=pltpu.SMEM),
                  pl.BlockSpec((None, BLK, 2 * BLK), lambda p, h: (p, 0, 0))],
        out_specs=pl.BlockSpec((None, None, 2 * BLK, 2 * BLK), lambda p, h: (p, h, 0, 0)),
        out_shape=jax.ShapeDtypeStruct((n_pat, DIL_HEADS // 2, 2 * BLK, 2 * BLK), F32),
        compiler_params=pltpu.CompilerParams(dimension_semantics=("arbitrary", "arbitrary")),
        name="bias_table",
    )(rel_bias, bkt)


def _attn_kernel(q1, k1, v1, q4, k4, v4, q16, k16, v16, bias_ref, o_ref,
                 s_ref, mb_ref, u_ref, w_ref, m_ref, *, seq):
    lo_half = lax.broadcasted_iota(jnp.int32, (BLK, LANES), 1) < DIL_HD
    refs = ((q1, k1, v1), (q4, k4, v4), (q16, k16, v16))
    nblk = seq // BLK

    def rows_of(ref, p, g, with_prev=False):
        r, n = divmod(g, nblk // DILATIONS[p])
        return ref[r, pl.ds((n - 1) * BLK, 2 * BLK) if with_prev else pl.ds(n * BLK, BLK), :]

    one_trip = jnp.minimum(pl.program_id(0) + 1, 1)

    def region(fn):
        lax.fori_loop(0, one_trip, lambda i, c: (fn(), c)[1], 0)

    def natural_rows(p, g):
        d = DILATIONS[p]
        r, n = divmod(g, nblk // d)
        return pl.ds(n * (BLK * d) + r, BLK, stride=d) if d > 1 else pl.ds(g * BLK, BLK)

    def scores(p):
        q_ref, k_ref, _ = refs[p]
        nb = nblk // DILATIONS[p]
        for g in range(nblk):
            has_prev = g % nb > 0
            nk = 2 * BLK if has_prev else BLK
            bias = bias_ref[p] if has_prev else bias_ref[p, :, BLK:]
            q = rows_of(q_ref, p, g)
            s = _dot_nt(_split_heads(q, lo_half), rows_of(k_ref, p, g, has_prev)) + bias
            s_ref[p * nblk + g, :, :nk] = s
            mb = jnp.broadcast_to(jnp.max(s, axis=-1, keepdims=True), (2 * BLK, LANES))
            mb_ref[p * nblk + g] = mb
            m_ref[p, natural_rows(p, g), :] = jnp.where(lo_half, mb[:BLK], mb[BLK:])

    def outputs(p):
        _, _, v_ref = refs[p]
        nb = nblk // DILATIONS[p]
        for g in range(nblk):
            has_prev = g % nb > 0
            nk = 2 * BLK if has_prev else BLK
            idx = p * nblk + g
            e = jnp.concatenate(
                [jnp.exp2(s_ref[idx, pl.ds(h * BLK, BLK), c * LANES:(c + 1) * LANES]
                          - mb_ref[idx, pl.ds(h * BLK, BLK), :])
                 for h in range(2) for c in range(nk // LANES)], axis=1).astype(BF16)
            vv = rows_of(v_ref, p, g, has_prev)
            lo_k = lax.broadcasted_iota(jnp.int32, vv.shape, 1) < DIL_HD
            zero = jnp.zeros_like(vv)
            lane = lax.broadcasted_iota(jnp.int32, vv.shape, 1)
            sum_a = jnp.where(lane < DIL_HD, 1.0, 0.0).astype(BF16)
            sum_b = jnp.where(lane < DIL_HD, 0.0, 1.0).astype(BF16)
            rhs = jnp.concatenate(
                [jnp.concatenate([jnp.where(lo_k, vv, zero), sum_a], axis=1),
                 jnp.concatenate([jnp.where(lo_k, zero, vv), sum_b], axis=1)], axis=0)
            uw = jnp.dot(e, rhs, preferred_element_type=F32)
            dst = natural_rows(p, g)
            u_ref[p, dst, :] = uw[:, :LANES]
            w_ref[p, dst, :] = uw[:, LANES:]

    order = tuple(range(len(DILATIONS)))
    region(functools.partial(scores, order[0]))
    for cur, nxt in zip(order, order[1:] + (None,)):
        def both(cur=cur, nxt=nxt):
            outputs(cur)
            if nxt is not None:
                scores(nxt)
        region(both)

    chunk = 2 * BLK

    def combine(i, carry):
        rows = pl.ds(pl.multiple_of(i * chunk, chunk), chunk)
        ms = [m_ref[p, rows, :] for p in range(len(DILATIONS))]
        mmax = functools.reduce(jnp.maximum, ms)
        cs = [jnp.exp2(mp - mmax) for mp in ms]
        num = sum(c * u_ref[p, rows, :] for p, c in enumerate(cs))
        den = sum(c * w_ref[p, rows, :] for p, c in enumerate(cs))
        o_ref[rows, :] = (num / den).astype(o_ref.dtype)
        return carry

    lax.fori_loop(0, seq // chunk, combine, 0)


def _dil_attn(a1, a4, a16, bias, *, batch, seq):
    n_pat = len(DILATIONS)
    n_pair = DIL_HEADS // 2
    arrs = (a1, a4, a16)

    def spec(d, part):
        return pl.BlockSpec((None, d, None, seq // d, LANES),
                            lambda h, b: (b, 0, part * n_pair + h, 0, 0))

    in_specs = [spec(d, part) for d in DILATIONS for part in range(3)]
    in_specs.append(pl.BlockSpec((n_pat, None, 2 * BLK, 2 * BLK), lambda h, b: (0, h, 0, 0)))
    args = [a for a in arrs for _ in range(3)] + [bias]
    return pl.pallas_call(
        functools.partial(_attn_kernel, seq=seq),
        grid=(n_pair, batch),
        in_specs=in_specs,
        out_specs=pl.BlockSpec((None, seq, LANES), lambda h, b: (b, 0, h)),
        out_shape=jax.ShapeDtypeStruct((batch, seq, DIL_W), BF16),
        scratch_shapes=[pltpu.VMEM((n_pat * seq // BLK, 2 * BLK, 2 * BLK), F32),
                        pltpu.VMEM((n_pat * seq // BLK, 2 * BLK, LANES), F32)]
        + [pltpu.VMEM((n_pat, seq, LANES), F32) for _ in range(3)],
        compiler_params=pltpu.CompilerParams(
            dimension_semantics=("arbitrary", "arbitrary"), vmem_limit_bytes=VMEM_LIMIT),
        name="dil_attn",
    )(*args)


def _post_kernel(h_ref, og_ref, od_ref, p_ref, wo_ref, w1_ref, w2_ref, wg_ref, wp_ref,
                 g_mix_ref, g_pre_ref, g_post_ref, o_ref, *, ff_chunk):
    mix = (jnp.dot(og_ref[...], wo_ref[:GLA_W, :], preferred_element_type=F32)
           + jnp.dot(od_ref[...], wo_ref[GLA_W:, :], preferred_element_type=F32))
    emb = jnp.dot(p_ref[...].astype(BF16), wp_ref[...], preferred_element_type=F32)
    h1 = h_ref[...] + _rms(mix, g_mix_ref[...])
    xn = _rms(h1, g_pre_ref[...]).astype(BF16)
    f = jnp.zeros_like(h1)
    for c in range(w1_ref.shape[1] // ff_chunk):
        cols = slice(c * ff_chunk, (c + 1) * ff_chunk)
        a = jnp.maximum(jnp.dot(xn, w1_ref[:, cols], preferred_element_type=F32), 0.0)
        f = f + jnp.dot((a * a).astype(BF16), w2_ref[cols, :], preferred_element_type=F32)
    h2 = h1 + _rms(f, g_post_ref[...])
    gate = jnp.dot(h2.astype(BF16), wg_ref[...], preferred_element_type=F32)
    o_ref[...] = h2 + emb / (1.0 + jnp.exp(-gate))


def _post(h, og, od, p, wo, w1, w2, wg, wp, g_mix, g_pre, g_post, *, layer, tm, ff_chunk=1024):
    t, dm = h.shape
    row = lambda width: pl.BlockSpec((tm, width), lambda i: (i, 0))
    const = lambda a: pl.BlockSpec((None,) + a.shape[1:], lambda i: (layer, 0, 0),
                                   pipeline_mode=pl.Buffered(1))
    consts = (wo, w1, w2, wg, wp, g_mix, g_pre, g_post)
    return pl.pallas_call(
        functools.partial(_post_kernel, ff_chunk=ff_chunk),
        grid=(t // tm,),
        in_specs=[row(dm), row(GLA_W), row(DIL_W),
                  pl.BlockSpec((None, tm, p.shape[2]), lambda i: (layer, i, 0))]
        + [const(a) for a in consts],
        out_specs=row(dm),
        out_shape=jax.ShapeDtypeStruct((t, dm), F32),
        compiler_params=pltpu.CompilerParams(
            dimension_semantics=("arbitrary",), vmem_limit_bytes=VMEM_LIMIT),
        name="post",
    )(h, og, od, p, *consts)


def kernel(x, p, w_in, w_gla_a2, b_gla_a, gla_norm_g, w_out, rel_bias, pre_mix_g, post_mix_g,
           pre_mlp_g, post_mlp_g, w_mlp_in, w_mlp_out, w_ple_gate, w_ple_proj):
    batch, seq, dm = x.shape
    depth = w_in.shape[0]
    assert seq % (max(DILATIONS) * BLK) == 0 and seq % MACRO == 0
    t = batch * seq
    rows1 = lambda a: a.reshape(depth, 1, -1)
    bias = _bias_table(rel_bias)
    h = x.reshape(t, dm)
    w = _prep_w_in(w_in)
    wa2 = jnp.pad(w_gla_a2, ((0, 0), (0, C_END - C_LR - GLA_LOWRANK), (0, 0))).astype(BF16)
    wo, w1, w2, wg, wp = (a.astype(BF16) for a in (w_out, w_mlp_in, w_mlp_out, w_ple_gate, w_ple_proj))
    p2 = p.reshape(depth, t, -1)
    for i in range(depth):
        gq, gk, la, gv, gg, a1, a4, a16 = _in_proj(
            h, rows1(pre_mix_g), w, wa2, rows1(b_gla_a), layer=i, batch=batch, seq=seq, tm=512)
        og = _gla(gq, gk, la, gv, gg, gla_norm_g[i].reshape(1, -1), batch=batch, seq=seq)
        od = _dil_attn(a1, a4, a16, bias, batch=batch, seq=seq)
        h = _post(h, og.reshape(t, GLA_W), od.reshape(t, DIL_W), p2, wo, w1, w2, wg, wp,
                  rows1(post_mix_g), rows1(pre_mlp_g), rows1(post_mlp_g), layer=i, tm=512)
    return h.reshape(batch, seq, dm)
```

```python
import functools
import math

import numpy as np
import jax
import jax.numpy as jnp
from jax import lax
from jax.experimental import pallas as pl
from jax.experimental.pallas import tpu as pltpu

F32 = jnp.float32
BF16 = jnp.bfloat16

EPS = 1e-6
GLA_HEADS = 4
GLA_DK = 64
GLA_DV = 128
GLA_QK = GLA_HEADS * GLA_DK
GLA_W = GLA_HEADS * GLA_DV
GLA_LOWRANK = 16
GLA_TAU = 16.0
DIL_HEADS = 8
DIL_HD = 64
DIL_W = DIL_HEADS * DIL_HD
DILATIONS = (1, 4, 16)
BAND = 128
BLK = 128
REL_BUCKETS = 32
REL_MAX_DIST = 2048
NEG = -1e30
LOG2E = math.log2(math.e)

LANES = 128
MACRO = 128
GLA_LEVELS = 7
VMEM_LIMIT = 56 * 1024 * 1024

C_GQ, C_GK, C_GV, C_GG, C_DQ, C_LR, C_END = 0, 256, 512, 1024, 1536, 3072, 3200


def _rms(x, g):
    return x * lax.rsqrt(jnp.mean(x * x, axis=-1, keepdims=True) + EPS) * g


def _w_in_kernel(wt_ref, o_ref, *, lr0):
    lr1 = lr0 + GLA_LOWRANK
    tk = wt_ref.shape[1]
    o_ref[:, :lr0] = wt_ref[:lr0, :].T.astype(BF16)
    o_ref[:, lr0:C_LR] = wt_ref[lr1:, :].T.astype(BF16)
    o_ref[:, C_LR:] = jnp.concatenate(
        [wt_ref[lr0:lr1, :].T, jnp.zeros((tk, C_END - C_LR - GLA_LOWRANK), F32)], axis=1).astype(BF16)


def _prep_w_in(w_in, *, tk=256):
    depth, dm, width = w_in.shape
    lr0 = C_DQ
    assert width - GLA_LOWRANK == C_LR
    return pl.pallas_call(
        functools.partial(_w_in_kernel, lr0=lr0),
        grid=(depth, dm // tk),
        in_specs=[pl.BlockSpec((None, width, tk), lambda l, i: (l, 0, i))],
        out_specs=pl.BlockSpec((None, tk, C_END), lambda l, i: (l, i, 0)),
        out_shape=jax.ShapeDtypeStruct((depth, dm, C_END), BF16),
        compiler_params=pltpu.CompilerParams(dimension_semantics=("arbitrary", "arbitrary")),
        name="prep_w_in",
    )(jnp.swapaxes(w_in, 1, 2))


def _in_proj_kernel(h_ref, g_ref, w_ref, wa2_ref, ba_ref,
                    gq_ref, gk_ref, la_ref, gv_ref, gg_ref, a1_ref, a4_ref, a16_ref,
                    slab_ref, slab4_ref, *, tm):
    xn = _rms(h_ref[...], g_ref[...]).astype(BF16)

    def proj(lo, hi):
        return jnp.dot(xn, w_ref[:, lo:hi], preferred_element_type=F32)

    z = jnp.dot(proj(C_LR, C_END).astype(BF16), wa2_ref[...],
                preferred_element_type=F32) + ba_ref[...]
    la_ref[...] = (jnp.minimum(z, 0.0) - jnp.log1p(jnp.exp(-jnp.abs(z)))) * (1.0 / GLA_TAU)

    for c in range(3):
        y = proj(C_DQ + DIL_W * c, C_DQ + DIL_W * (c + 1))
        if c == 0:
            y = y * (LOG2E * DIL_HD ** -0.5)
        for s in range(DIL_W // LANES):
            ys = y[:, LANES * s:LANES * (s + 1)]
            idx = c * (DIL_W // LANES) + s
            a1_ref[0, idx] = ys.astype(BF16)
            slab_ref[idx] = ys
            for r in range(4):
                y4 = slab_ref[idx, pl.ds(r, tm // 4, stride=4), :]
                a4_ref[r, idx] = y4.astype(BF16)
                slab4_ref[idx, r] = y4
            for r in range(16):
                a16_ref[r, idx] = slab4_ref[idx, r % 4, pl.ds(r // 4, tm // 16, stride=4), :].astype(BF16)

    gq_ref[...] = (proj(C_GQ, C_GK) * (GLA_DK ** -0.5)).astype(BF16)
    gk_ref[...] = proj(C_GK, C_GV).astype(BF16)
    gv_ref[...] = proj(C_GV, C_GG).astype(BF16)
    gg_ref[...] = proj(C_GG, C_DQ).astype(BF16)


def _in_proj(h, g, w, wa2, ba, *, layer, batch, seq, tm):
    t = batch * seq
    nt = seq // tm
    row = lambda width: pl.BlockSpec((tm, width), lambda b, i: (b * nt + i, 0))
    const = lambda shape: pl.BlockSpec((None,) + shape[1:], lambda b, i: (layer, 0, 0))
    n_slab = 3 * DIL_W // LANES
    dil_spec = lambda d: pl.BlockSpec((None, d, n_slab, tm // d, LANES), lambda b, i: (b, 0, 0, i, 0))
    out_shape = (
        jax.ShapeDtypeStruct((t, GLA_QK), BF16), jax.ShapeDtypeStruct((t, GLA_QK), BF16),
        jax.ShapeDtypeStruct((t, GLA_QK), F32),
        jax.ShapeDtypeStruct((t, GLA_W), BF16), jax.ShapeDtypeStruct((t, GLA_W), BF16),
    ) + tuple(jax.ShapeDtypeStruct((batch, d, n_slab, seq // d, LANES), BF16) for d in DILATIONS)
    return pl.pallas_call(
        functools.partial(_in_proj_kernel, tm=tm),
        grid=(batch, nt),
        in_specs=[row(h.shape[1]), const(g.shape), const(w.shape), const(wa2.shape), const(ba.shape)],
        out_specs=(row(GLA_QK), row(GLA_QK), row(GLA_QK), row(GLA_W), row(GLA_W))
        + tuple(dil_spec(d) for d in DILATIONS),
        out_shape=out_shape,
        scratch_shapes=[pltpu.VMEM((n_slab, tm, LANES), F32),
                        pltpu.VMEM((n_slab, 4, tm // 4, LANES), F32)],
        compiler_params=pltpu.CompilerParams(
            dimension_semantics=("arbitrary", "arbitrary"), vmem_limit_bytes=VMEM_LIMIT),
        name="in_proj",
    )(h, g, w, wa2, ba)


def _gla_constants():
    n = MACRO
    i = np.arange(n)[:, None]
    t = np.arange(n)[None, :]
    p = np.concatenate([t <= i, t > i], axis=0).astype(np.float32)
    j = np.arange(n)[None, :]
    lvl = np.where(j < i, np.floor(np.log2(np.maximum(i ^ j, 1))).astype(np.int32),
                   np.where(j == i, GLA_LEVELS, -1)).astype(np.int32)
    return p, np.concatenate([lvl, lvl], axis=0)


def _split_heads(x, lo_half):
    zero = jnp.zeros_like(x)
    return jnp.concatenate([jnp.where(lo_half, x, zero), jnp.where(lo_half, zero, x)], axis=0)


def _dot_nt(a, b):
    return lax.dot_general(a, b, (((1,), (1,)), ((), ())), preferred_element_type=F32)


def _dot_tn(a, b):
    return lax.dot_general(a, b, (((0,), (0,)), ((), ())), preferred_element_type=F32)


def _gla_kernel(q_ref, k_ref, la_ref, v_ref, gg_ref, p_ref, lvl_ref, gn_ref, o_ref, st_ref,
                *, n_macro, unroll):
    st_ref[...] = jnp.zeros_like(st_ref)
    lo_half = lax.broadcasted_iota(jnp.int32, (MACRO, LANES), 1) < (LANES // 2)

    def body(m, carry):
        r0 = pl.multiple_of(m * MACRO, MACRO)
        rows = pl.ds(r0, MACRO)
        q = q_ref[rows, :].astype(F32)
        k = k_ref[rows, :].astype(F32)
        la = la_ref[rows, :]
        la_hi = la.astype(BF16)
        la2 = jnp.concatenate([la_hi, (la - la_hi.astype(F32)).astype(BF16)], axis=1)

        def range_sum(blk):
            r = jnp.dot(p_ref[blk * MACRO:(blk + 1) * MACRO, :], la2, preferred_element_type=F32)
            return r[:, :GLA_QK] + r[:, GLA_QK:]

        b = range_sum(0)
        x_start = jnp.exp(b)
        qg = q * x_start
        kg = k * jnp.exp(range_sum(1))
        a_last = x_start[MACRO - 1:MACRO, :]
        lvl = lvl_ref[...]

        @functools.cache
        def row_bcast(r):
            return jnp.broadcast_to(b[r:r + 1, :], (8, GLA_QK))

        sub = lax.broadcasted_iota(jnp.int32, (8, GLA_QK), 0)

        def ref_rows(l):
            span = 2 << l
            pieces = []
            for g in range(MACRO // 8):
                refs = sorted({(i // span) * span + span // 2 - 1 for i in range(8 * g, 8 * g + 8)})
                piece = row_bcast(refs[-1])
                for c in reversed(range(len(refs) - 1)):
                    piece = jnp.where(sub < (c + 1) * span, row_bcast(refs[c]), piece)
                pieces.append(piece)
            return jnp.concatenate(pieces, axis=0)

        xs = [jnp.exp(-jnp.abs(b - ref_rows(l))) for l in range(GLA_LEVELS)]
        for pair in range(GLA_HEADS // 2):
            sl = slice(LANES * pair, LANES * (pair + 1))
            att = _dot_nt(_split_heads(q[:, sl], lo_half).astype(BF16), k[:, sl].astype(BF16))
            att = jnp.where(lvl == GLA_LEVELS, att, 0.0)
            for l in range(GLA_LEVELS):
                xl = xs[l][:, sl]
                a = _dot_nt(_split_heads(q[:, sl] * xl, lo_half).astype(BF16),
                            (k[:, sl] * xl).astype(BF16))
                att = jnp.where(lvl == l, a, att)
            att = att.astype(BF16)
            st = st_ref[pair]
            inter = _dot_nt(_split_heads(qg[:, sl], lo_half).astype(BF16), st.astype(BF16))
            kgp = kg[:, sl].astype(BF16)
            upd = []
            for e in range(2):
                h = 2 * pair + e
                hs = slice(GLA_DV * h, GLA_DV * (h + 1))
                vh = v_ref[rows, hs]
                o = (jnp.dot(att[e * MACRO:(e + 1) * MACRO], vh, preferred_element_type=F32)
                     + inter[e * MACRO:(e + 1) * MACRO])
                gate = gg_ref[rows, hs].astype(F32)
                o = _rms(o, gn_ref[...]) * (gate / (1.0 + jnp.exp(-gate)))
                o_ref[rows, hs] = o.astype(o_ref.dtype)
                upd.append(_dot_tn(vh, kgp))
            st_ref[pair] = st * a_last[:, sl] + jnp.where(lo_half, upd[0], upd[1])
        return carry

    lax.fori_loop(0, n_macro, body, 0, unroll=unroll)


def _gla(gq, gk, la, gv, gg, gn, *, batch, seq, unroll=8):
    p_np, lvl_np = _gla_constants()
    pm = jnp.asarray(p_np, BF16)
    lvl = jnp.asarray(lvl_np)
    seq_spec = lambda width: pl.BlockSpec((None, seq, width), lambda b: (b, 0, 0))
    const = lambda shape: pl.BlockSpec(shape, lambda b: (0,) * len(shape))
    r3 = lambda a: a.reshape(batch, seq, a.shape[-1])
    return pl.pallas_call(
        functools.partial(_gla_kernel, n_macro=seq // MACRO, unroll=unroll),
        grid=(batch,),
        in_specs=[seq_spec(GLA_QK), seq_spec(GLA_QK), seq_spec(GLA_QK), seq_spec(GLA_W),
                  seq_spec(GLA_W), const(pm.shape), const(lvl.shape), const(gn.shape)],
        out_specs=seq_spec(GLA_W),
        out_shape=jax.ShapeDtypeStruct((batch, seq, GLA_W), BF16),
        scratch_shapes=[pltpu.VMEM((GLA_HEADS // 2, GLA_DV, LANES), F32)],
        compiler_params=pltpu.CompilerParams(
            dimension_semantics=("arbitrary",), vmem_limit_bytes=VMEM_LIMIT),
        name="gla",
    )(r3(gq), r3(gk), r3(la), r3(gv), r3(gg), pm, lvl, gn)


def _t5_bucket_np(dist):
    max_exact = REL_BUCKETS // 2
    d = np.maximum(dist, 1).astype(np.float32)
    large = max_exact + (np.log(d / np.float32(max_exact)) / np.float32(math.log(REL_MAX_DIST / max_exact))
                         * np.float32(REL_BUCKETS - max_exact)).astype(np.int32)
    large = np.minimum(large, REL_BUCKETS - 1)
    return np.where(dist < max_exact, dist, large).astype(np.int32)


def _bucket_table():
    qi = np.arange(BLK)[:, None]
    ki = np.arange(2 * BLK)[None, :]
    j = qi + BLK - ki
    valid = (j >= 0) & (j <= BAND)
    return np.stack([np.where(valid, _t5_bucket_np(np.maximum(j, 0) * d), -1) for d in DILATIONS]).astype(np.int32)


def _bias_kernel(rel_ref, bkt_ref, o_ref, *, buckets):
    for p, used in enumerate(buckets):
        bkt = bkt_ref[p]
        accs = [jnp.full(bkt.shape, NEG, F32) for _ in range(DIL_HEADS)]
        for u in used:
            hit = bkt == u
            accs = [jnp.where(hit, rel_ref[u, h] * LOG2E, a) for h, a in enumerate(accs)]
        for h, a in enumerate(accs):
            o_ref[p, h // 2, (h % 2) * BLK:(h % 2 + 1) * BLK, :] = a


def _bias_table(rel_bias):
    bkt_np = _bucket_table()
    buckets = tuple(tuple(int(u) for u in np.unique(b) if u >= 0) for b in bkt_np)
    n_pat = len(DILATIONS)
    return pl.pallas_call(
        functools.partial(_bias_kernel, buckets=buckets),
        in_specs=[pl.BlockSpec(memory_space=pltpu.SMEM), pl.BlockSpec(memory_space=pltpu.VMEM)],
        out_specs=pl.BlockSpec(memory_space=pltpu.VMEM),
        out_shape=jax.ShapeDtypeStruct((n_pat, DIL_HEADS // 2, 2 * BLK, 2 * BLK), F32),
        name="bias_table",
    )(rel_bias, jnp.asarray(bkt_np))


def _attn_kernel(q1, k1, v1, q4, k4, v4, q16, k16, v16, bias_ref, o_ref,
                 s_ref, mb_ref, u_ref, w_ref, m_ref, *, seq):
    lo_half = lax.broadcasted_iota(jnp.int32, (BLK, LANES), 1) < DIL_HD
    refs = ((q1, k1, v1), (q4, k4, v4), (q16, k16, v16))
    nblk = seq // BLK

    def rows_of(ref, p, g, with_prev=False):
        r, n = divmod(g, nblk // DILATIONS[p])
        return ref[r, pl.ds((n - 1) * BLK, 2 * BLK) if with_prev else pl.ds(n * BLK, BLK), :]

    one_trip = jnp.minimum(pl.program_id(0) + 1, 1)

    def region(fn):
        lax.fori_loop(0, one_trip, lambda i, c: (fn(), c)[1], 0)

    def natural_rows(p, g):
        d = DILATIONS[p]
        r, n = divmod(g, nblk // d)
        return pl.ds(n * (BLK * d) + r, BLK, stride=d) if d > 1 else pl.ds(g * BLK, BLK)

    def scores(p):
        q_ref, k_ref, _ = refs[p]
        nb = nblk // DILATIONS[p]
        for g in range(nblk):
            has_prev = g % nb > 0
            nk = 2 * BLK if has_prev else BLK
            bias = bias_ref[p] if has_prev else bias_ref[p, :, BLK:]
            q = rows_of(q_ref, p, g)
            s = _dot_nt(_split_heads(q, lo_half), rows_of(k_ref, p, g, has_prev)) + bias
            s_ref[p * nblk + g, :, :nk] = s
            mb = jnp.broadcast_to(jnp.max(s, axis=-1, keepdims=True), (2 * BLK, LANES))
            mb_ref[p * nblk + g] = mb
            m_ref[p, natural_rows(p, g), :] = jnp.where(lo_half, mb[:BLK], mb[BLK:])

    def outputs(p):
        _, _, v_ref = refs[p]
        nb = nblk // DILATIONS[p]
        for g in range(nblk):
            has_prev = g % nb > 0
            nk = 2 * BLK if has_prev else BLK
            idx = p * nblk + g
            e = jnp.concatenate(
                [jnp.exp2(s_ref[idx, pl.ds(h * BLK, BLK), c * LANES:(c + 1) * LANES]
                          - mb_ref[idx, pl.ds(h * BLK, BLK), :])
                 for h in range(2) for c in range(nk // LANES)], axis=1).astype(BF16)
            vv = rows_of(v_ref, p, g, has_prev)
            lo_k = lax.broadcasted_iota(jnp.int32, vv.shape, 1) < DIL_HD
            zero = jnp.zeros_like(vv)
            lane = lax.broadcasted_iota(jnp.int32, vv.shape, 1)
            sum_a = jnp.where(lane < DIL_HD, 1.0, 0.0).astype(BF16)
            sum_b = jnp.where(lane < DIL_HD, 0.0, 1.0).astype(BF16)
            rhs = jnp.concatenate(
                [jnp.concatenate([jnp.where(lo_k, vv, zero), sum_a], axis=1),
                 jnp.concatenate([jnp.where(lo_k, zero, vv), sum_b], axis=1)], axis=0)
            uw = jnp.dot(e, rhs, preferred_element_type=F32)
            dst = natural_rows(p, g)
            u_ref[p, dst, :] = uw[:, :LANES]
            w_ref[p, dst, :] = uw[:, LANES:]

    order = tuple(range(len(DILATIONS)))
    region(functools.partial(scores, order[0]))
    for cur, nxt in zip(order, order[1:] + (None,)):
        def both(cur=cur, nxt=nxt):
            outputs(cur)
            if nxt is not None:
                scores(nxt)
        region(both)

    chunk = 2 * BLK

    def combine(i, carry):
        rows = pl.ds(pl.multiple_of(i * chunk, chunk), chunk)
        ms = [m_ref[p, rows, :] for p in range(len(DILATIONS))]
        mmax = functools.reduce(jnp.maximum, ms)
        cs = [jnp.exp2(mp - mmax) for mp in ms]
        num = sum(c * u_ref[p, rows, :] for p, c in enumerate(cs))
        den = sum(c * w_ref[p, rows, :] for p, c in enumerate(cs))
        o_ref[rows, :] = (num / den).astype(o_ref.dtype)
        return carry

    lax.fori_loop(0, seq // chunk, combine, 0, unroll=2)


def _dil_attn(a1, a4, a16, bias, *, batch, seq):
    n_pat = len(DILATIONS)
    n_pair = DIL_HEADS // 2
    arrs = (a1, a4, a16)

    def spec(d, part):
        return pl.BlockSpec((None, d, None, seq // d, LANES),
                            lambda h, b: (b, 0, part * n_pair + h, 0, 0))

    in_specs = [spec(d, part) for d in DILATIONS for part in range(3)]
    in_specs.append(pl.BlockSpec((n_pat, None, 2 * BLK, 2 * BLK), lambda h, b: (0, h, 0, 0)))
    args = [a for a in arrs for _ in range(3)] + [bias]
    return pl.pallas_call(
        functools.partial(_attn_kernel, seq=seq),
        grid=(n_pair, batch),
        in_specs=in_specs,
        out_specs=pl.BlockSpec((None, seq, LANES), lambda h, b: (b, 0, h)),
        out_shape=jax.ShapeDtypeStruct((batch, seq, DIL_W), BF16),
        scratch_shapes=[pltpu.VMEM((n_pat * seq // BLK, 2 * BLK, 2 * BLK), F32),
                        pltpu.VMEM((n_pat * seq // BLK, 2 * BLK, LANES), F32)]
        + [pltpu.VMEM((n_pat, seq, LANES), F32) for _ in range(3)],
        compiler_params=pltpu.CompilerParams(
            dimension_semantics=("arbitrary", "arbitrary"), vmem_limit_bytes=VMEM_LIMIT),
        name="dil_attn",
    )(*args)


def _post_kernel(h_ref, og_ref, od_ref, p_ref, wo_ref, w1_ref, w2_ref, wg_ref, wp_ref,
                 g_mix_ref, g_pre_ref, g_post_ref, o_ref, *, ff_chunk):
    mix = (jnp.dot(og_ref[...], wo_ref[:GLA_W, :], preferred_element_type=F32)
           + jnp.dot(od_ref[...], wo_ref[GLA_W:, :], preferred_element_type=F32))
    emb = jnp.dot(p_ref[...].astype(BF16), wp_ref[...], preferred_element_type=F32)
    h1 = h_ref[...] + _rms(mix, g_mix_ref[...])
    xn = _rms(h1, g_pre_ref[...]).astype(BF16)
    f = jnp.zeros_like(h1)
    for c in range(w1_ref.shape[1] // ff_chunk):
        cols = slice(c * ff_chunk, (c + 1) * ff_chunk)
        a = jnp.maximum(jnp.dot(xn, w1_ref[:, cols], preferred_element_type=F32), 0.0)
        f = f + jnp.dot((a * a).astype(BF16), w2_ref[cols, :], preferred_element_type=F32)
    h2 = h1 + _rms(f, g_post_ref[...])
    gate = jnp.dot(h2.astype(BF16), wg_ref[...], preferred_element_type=F32)
    o_ref[...] = h2 + emb / (1.0 + jnp.exp(-gate))


def _post(h, og, od, p, wo, w1, w2, wg, wp, g_mix, g_pre, g_post, *, layer, tm, ff_chunk=1024):
    t, dm = h.shape
    row = lambda width: pl.BlockSpec((tm, width), lambda i: (i, 0))
    const = lambda a: pl.BlockSpec((None,) + a.shape[1:], lambda i: (layer, 0, 0),
                                   pipeline_mode=pl.Buffered(1))
    consts = (wo, w1, w2, wg, wp, g_mix, g_pre, g_post)
    return pl.pallas_call(
        functools.partial(_post_kernel, ff_chunk=ff_chunk),
        grid=(t // tm,),
        in_specs=[row(dm), row(GLA_W), row(DIL_W),
                  pl.BlockSpec((None, tm, p.shape[2]), lambda i: (layer, i, 0))]
        + [const(a) for a in consts],
        out_specs=row(dm),
        out_shape=jax.ShapeDtypeStruct((t, dm), F32),
        compiler_params=pltpu.CompilerParams(
            dimension_semantics=("arbitrary",), vmem_limit_bytes=VMEM_LIMIT),
        name="post",
    )(h, og, od, p, *consts)


def kernel(x, p, w_in, w_gla_a2, b_gla_a, gla_norm_g, w_out, rel_bias, pre_mix_g, post_mix_g,
           pre_mlp_g, post_mlp_g, w_mlp_in, w_mlp_out, w_ple_gate, w_ple_proj):
    batch, seq, dm = x.shape
    depth = w_in.shape[0]
    assert seq % (max(DILATIONS) * BLK) == 0 and seq % MACRO == 0
    t = batch * seq
    rows1 = lambda a: a.reshape(depth, 1, -1)
    bias = _bias_table(rel_bias)
    h = x.reshape(t, dm)
    w = _prep_w_in(w_in)
    wa2 = jnp.pad(w_gla_a2, ((0, 0), (0, C_END - C_LR - GLA_LOWRANK), (0, 0))).astype(BF16)
    wo, w1, w2, wg, wp = (a.astype(BF16) for a in (w_out, w_mlp_in, w_mlp_out, w_ple_gate, w_ple_proj))
    p2 = p.reshape(depth, t, -1)
    for i in range(depth):
        gq, gk, la, gv, gg, a1, a4, a16 = _in_proj(
            h, rows1(pre_mix_g), w, wa2, rows1(b_gla_a), layer=i, batch=batch, seq=seq, tm=512)
        og = _gla(gq, gk, la, gv, gg, gla_norm_g[i].reshape(1, -1), batch=batch, seq=seq)
        od = _dil_attn(a1, a4, a16, bias, batch=batch, seq=seq)
        h = _post(h, og.reshape(t, GLA_W), od.reshape(t, DIL_W), p2, wo, w1, w2, wg, wp,
                  rows1(post_mix_g), rows1(pre_mlp_g), rows1(post_mlp_g), layer=i, tm=512)
    return h.reshape(batch, seq, dm)
```

```python
import functools
import math

import numpy as np
import jax
import jax.numpy as jnp
from jax import lax
from jax.experimental import pallas as pl
from jax.experimental.pallas import tpu as pltpu

F32 = jnp.float32
BF16 = jnp.bfloat16

EPS = 1e-6
GLA_HEADS = 4
GLA_DK = 64
GLA_DV = 128
GLA_QK = GLA_HEADS * GLA_DK
GLA_W = GLA_HEADS * GLA_DV
GLA_LOWRANK = 16
GLA_TAU = 16.0
DIL_HEADS = 8
DIL_HD = 64
DIL_W = DIL_HEADS * DIL_HD
DILATIONS = (1, 4, 16)
BAND = 128
BLK = 128
REL_BUCKETS = 32
REL_MAX_DIST = 2048
NEG = -1e30
LOG2E = math.log2(math.e)

LANES = 128
MACRO = 128
GLA_LEVELS = 7
VMEM_LIMIT = 56 * 1024 * 1024

C_GQ, C_GK, C_GV, C_GG, C_DQ, C_LR, C_END = 0, 256, 512, 1024, 1536, 3072, 3200


def _rms(x, g):
    return x * lax.rsqrt(jnp.mean(x * x, axis=-1, keepdims=True) + EPS) * g


def _w_in_kernel(wt_ref, o_ref, *, lr0):
    lr1 = lr0 + GLA_LOWRANK
    tk = wt_ref.shape[1]
    o_ref[:, :lr0] = wt_ref[:lr0, :].T.astype(BF16)
    o_ref[:, lr0:C_LR] = wt_ref[lr1:, :].T.astype(BF16)
    o_ref[:, C_LR:] = jnp.concatenate(
        [wt_ref[lr0:lr1, :].T, jnp.zeros((tk, C_END - C_LR - GLA_LOWRANK), F32)], axis=1).astype(BF16)


def _prep_w_in(w_in, *, tk=256):
    depth, dm, width = w_in.shape
    lr0 = C_DQ
    assert width - GLA_LOWRANK == C_LR
    return pl.pallas_call(
        functools.partial(_w_in_kernel, lr0=lr0),
        grid=(depth, dm // tk),
        in_specs=[pl.BlockSpec((None, width, tk), lambda l, i: (l, 0, i))],
        out_specs=pl.BlockSpec((None, tk, C_END), lambda l, i: (l, i, 0)),
        out_shape=jax.ShapeDtypeStruct((depth, dm, C_END), BF16),
        compiler_params=pltpu.CompilerParams(dimension_semantics=("arbitrary", "arbitrary")),
        name="prep_w_in",
    )(jnp.swapaxes(w_in, 1, 2))


def _in_proj_kernel(h_ref, g_ref, w_ref, wa2_ref, ba_ref,
                    gq_ref, gk_ref, la_ref, gv_ref, gg_ref, a1_ref, a4_ref, a16_ref,
                    slab_ref, slab4_ref, *, tm):
    xn = _rms(h_ref[...], g_ref[...]).astype(BF16)

    def proj(lo, hi):
        return jnp.dot(xn, w_ref[:, lo:hi], preferred_element_type=F32)

    z = jnp.dot(proj(C_LR, C_END).astype(BF16), wa2_ref[...],
                preferred_element_type=F32) + ba_ref[...]
    la_ref[...] = (jnp.minimum(z, 0.0) - jnp.log1p(jnp.exp(-jnp.abs(z)))) * (1.0 / GLA_TAU)

    n_grp = DIL_W // LANES

    def project_dilated(c):
        y = proj(C_DQ + DIL_W * c, C_DQ + DIL_W * (c + 1))
        if c == 0:
            y = y * (LOG2E * DIL_HD ** -0.5)
        for s in range(n_grp):
            ys = y[:, LANES * s:LANES * (s + 1)]
            idx = c * n_grp + s
            a1_ref[0, idx] = ys.astype(BF16)
            slab_ref[idx] = ys

    def deinterleave(c):
        for idx in range(c * n_grp, (c + 1) * n_grp):
            for r in range(4):
                y4 = slab_ref[idx, pl.ds(r, tm // 4, stride=4), :]
                a4_ref[r, idx] = y4.astype(BF16)
                slab4_ref[idx, r] = y4
            for r in range(16):
                a16_ref[r, idx] = slab4_ref[idx, r % 4, pl.ds(r // 4, tm // 16, stride=4), :].astype(BF16)

    project_dilated(0)
    project_dilated(1)
    deinterleave(0)
    project_dilated(2)
    deinterleave(1)
    gq_ref[...] = (proj(C_GQ, C_GK) * (GLA_DK ** -0.5)).astype(BF16)
    gk_ref[...] = proj(C_GK, C_GV).astype(BF16)
    deinterleave(2)
    gv_ref[...] = proj(C_GV, C_GG).astype(BF16)
    gg_ref[...] = proj(C_GG, C_DQ).astype(BF16)


def _in_proj(h, g, w, wa2, ba, *, layer, batch, seq, tm):
    t = batch * seq
    nt = seq // tm
    row = lambda width: pl.BlockSpec((tm, width), lambda b, i: (b * nt + i, 0))
    const = lambda shape: pl.BlockSpec((None,) + shape[1:], lambda b, i: (layer, 0, 0))
    n_slab = 3 * DIL_W // LANES
    dil_spec = lambda d: pl.BlockSpec((None, d, n_slab, tm // d, LANES), lambda b, i: (b, 0, 0, i, 0))
    out_shape = (
        jax.ShapeDtypeStruct((t, GLA_QK), BF16), jax.ShapeDtypeStruct((t, GLA_QK), BF16),
        jax.ShapeDtypeStruct((t, GLA_QK), F32),
        jax.ShapeDtypeStruct((t, GLA_W), BF16), jax.ShapeDtypeStruct((t, GLA_W), BF16),
    ) + tuple(jax.ShapeDtypeStruct((batch, d, n_slab, seq // d, LANES), BF16) for d in DILATIONS)
    return pl.pallas_call(
        functools.partial(_in_proj_kernel, tm=tm),
        grid=(batch, nt),
        in_specs=[row(h.shape[1]), const(g.shape), const(w.shape), const(wa2.shape), const(ba.shape)],
        out_specs=(row(GLA_QK), row(GLA_QK), row(GLA_QK), row(GLA_W), row(GLA_W))
        + tuple(dil_spec(d) for d in DILATIONS),
        out_shape=out_shape,
        scratch_shapes=[pltpu.VMEM((n_slab, tm, LANES), F32),
                        pltpu.VMEM((n_slab, 4, tm // 4, LANES), F32)],
        compiler_params=pltpu.CompilerParams(
            dimension_semantics=("arbitrary", "arbitrary"), vmem_limit_bytes=VMEM_LIMIT),
        name="in_proj",
    )(h, g, w, wa2, ba)


def _gla_constants():
    n = MACRO
    i = np.arange(n)[:, None]
    t = np.arange(n)[None, :]
    p = np.concatenate([t <= i, t > i], axis=0).astype(np.float32)
    j = np.arange(n)[None, :]
    lvl = np.where(j < i, np.floor(np.log2(np.maximum(i ^ j, 1))).astype(np.int32),
                   np.where(j == i, GLA_LEVELS, -1)).astype(np.int32)
    return p, np.concatenate([lvl, lvl], axis=0)


def _split_heads(x, lo_half):
    zero = jnp.zeros_like(x)
    return jnp.concatenate([jnp.where(lo_half, x, zero), jnp.where(lo_half, zero, x)], axis=0)


def _dot_nt(a, b):
    return lax.dot_general(a, b, (((1,), (1,)), ((), ())), preferred_element_type=F32)


def _dot_tn(a, b):
    return lax.dot_general(a, b, (((0,), (0,)), ((), ())), preferred_element_type=F32)


def _gla_kernel(q_ref, k_ref, la_ref, v_ref, gg_ref, p_ref, lvl_ref, gn_ref, o_ref, st_ref,
                *, n_macro, group, unroll):
    st_ref[...] = jnp.zeros_like(st_ref)
    lo_half = lax.broadcasted_iota(jnp.int32, (MACRO, LANES), 1) < (LANES // 2)

    lvl = lvl_ref[...]
    n_pair = GLA_HEADS // 2

    def body(i, carry):
        streams = []
        for mm in range(group):
            rows = pl.ds(pl.multiple_of((i * group + mm) * MACRO, MACRO), MACRO)
            for pair in range(n_pair):
                qk = pl.ds(LANES * pair, LANES)
                vg = pl.ds(2 * GLA_DV * pair, 2 * GLA_DV)
                streams.append(_gla_macro(
                    rows, q_ref.at[:, qk], k_ref.at[:, qk], la_ref.at[:, qk], v_ref.at[:, vg],
                    gg_ref.at[:, vg], p_ref, lvl, gn_ref, o_ref.at[:, vg], st_ref.at[pair], lo_half))
        _interleave(*streams)
        return carry

    lax.fori_loop(0, n_macro // group, body, 0, unroll=unroll)


def _gla(gq, gk, la, gv, gg, gn, *, batch, seq, group=2, unroll=4):
    p_np, lvl_np = _gla_constants()
    pm = jnp.asarray(p_np, BF16)
    lvl = jnp.asarray(lvl_np)
    seq_spec = lambda width: pl.BlockSpec((None, seq, width), lambda b: (b, 0, 0))
    const = lambda shape: pl.BlockSpec(shape, lambda b: (0,) * len(shape))
    r3 = lambda a: a.reshape(batch, seq, a.shape[-1])
    return pl.pallas_call(
        functools.partial(_gla_kernel, n_macro=seq // MACRO, group=group, unroll=unroll),
        grid=(batch,),
        in_specs=[seq_spec(GLA_QK), seq_spec(GLA_QK), seq_spec(GLA_QK), seq_spec(GLA_W),
                  seq_spec(GLA_W), const(pm.shape), const(lvl.shape), const(gn.shape)],
        out_specs=seq_spec(GLA_W),
        out_shape=jax.ShapeDtypeStruct((batch, seq, GLA_W), BF16),
        scratch_shapes=[pltpu.VMEM((GLA_HEADS // 2, GLA_DV, LANES), F32)],
        compiler_params=pltpu.CompilerParams(
            dimension_semantics=("arbitrary",), vmem_limit_bytes=VMEM_LIMIT),
        name="gla",
    )(r3(gq), r3(gk), r3(la), r3(gv), r3(gg), pm, lvl, gn)


def _t5_bucket_np(dist):
    max_exact = REL_BUCKETS // 2
    d = np.maximum(dist, 1).astype(np.float32)
    large = max_exact + (np.log(d / np.float32(max_exact)) / np.float32(math.log(REL_MAX_DIST / max_exact))
                         * np.float32(REL_BUCKETS - max_exact)).astype(np.int32)
    large = np.minimum(large, REL_BUCKETS - 1)
    return np.where(dist < max_exact, dist, large).astype(np.int32)


def _bucket_table():
    qi = np.arange(BLK)[:, None]
    ki = np.arange(2 * BLK)[None, :]
    j = qi + BLK - ki
    valid = (j >= 0) & (j <= BAND)
    return np.stack([np.where(valid, _t5_bucket_np(np.maximum(j, 0) * d), -1) for d in DILATIONS]).astype(np.int32)


def _bias_kernel(rel_ref, bkt_ref, o_ref, *, buckets):
    for p, used in enumerate(buckets):
        bkt = bkt_ref[p]
        accs = [jnp.full(bkt.shape, NEG, F32) for _ in range(DIL_HEADS)]
        for u in used:
            hit = bkt == u
            accs = [jnp.where(hit, rel_ref[u, h] * LOG2E, a) for h, a in enumerate(accs)]
        for h, a in enumerate(accs):
            o_ref[p, h // 2, (h % 2) * BLK:(h % 2 + 1) * BLK, :] = a


def _bias_table(rel_bias):
    bkt_np = _bucket_table()
    buckets = tuple(tuple(int(u) for u in np.unique(b) if u >= 0) for b in bkt_np)
    n_pat = len(DILATIONS)
    return pl.pallas_call(
        functools.partial(_bias_kernel, buckets=buckets),
        in_specs=[pl.BlockSpec(memory_space=pltpu.SMEM), pl.BlockSpec(memory_space=pltpu.VMEM)],
        out_specs=pl.BlockSpec(memory_space=pltpu.VMEM),
        out_shape=jax.ShapeDtypeStruct((n_pat, DIL_HEADS // 2, 2 * BLK, 2 * BLK), F32),
        name="bias_table",
    )(rel_bias, jnp.asarray(bkt_np))


def _attn_kernel(q1, k1, v1, q4, k4, v4, q16, k16, v16, bias_ref, o_ref,
                 s_ref, mb_ref, u_ref, w_ref, m_ref, *, seq):
    lo_half = lax.broadcasted_iota(jnp.int32, (BLK, LANES), 1) < DIL_HD
    refs = ((q1, k1, v1), (q4, k4, v4), (q16, k16, v16))
    nblk = seq // BLK

    def rows_of(ref, p, g, with_prev=False):
        r, n = divmod(g, nblk // DILATIONS[p])
        return ref[r, pl.ds((n - 1) * BLK, 2 * BLK) if with_prev else pl.ds(n * BLK, BLK), :]

    one_trip = jnp.minimum(pl.program_id(0) + 1, 1)

    def region(fn):
        lax.fori_loop(0, one_trip, lambda i, c: (fn(), c)[1], 0)

    def natural_rows(p, g):
        d = DILATIONS[p]
        r, n = divmod(g, nblk // d)
        return pl.ds(n * (BLK * d) + r, BLK, stride=d) if d > 1 else pl.ds(g * BLK, BLK)

    def scores(p):
        q_ref, k_ref, _ = refs[p]
        nb = nblk // DILATIONS[p]
        for g in range(nblk):
            has_prev = g % nb > 0
            nk = 2 * BLK if has_prev else BLK
            bias = bias_ref[p] if has_prev else bias_ref[p, :, BLK:]
            q = rows_of(q_ref, p, g)
            s = _dot_nt(_split_heads(q, lo_half), rows_of(k_ref, p, g, has_prev)) + bias
            s_ref[p * nblk + g, :, :nk] = s
            mb = jnp.broadcast_to(jnp.max(s, axis=-1, keepdims=True), (2 * BLK, LANES))
            mb_ref[p * nblk + g] = mb
            m_ref[p, natural_rows(p, g), :] = jnp.where(lo_half, mb[:BLK], mb[BLK:])
            yield

    def outputs(p):
        _, _, v_ref = refs[p]
        nb = nblk // DILATIONS[p]
        for g in range(nblk):
            has_prev = g % nb > 0
            nk = 2 * BLK if has_prev else BLK
            idx = p * nblk + g
            e = jnp.concatenate(
                [jnp.exp2(s_ref[idx, pl.ds(h * BLK, BLK), c * LANES:(c + 1) * LANES]
                          - mb_ref[idx, pl.ds(h * BLK, BLK), :])
                 for h in range(2) for c in range(nk // LANES)], axis=1).astype(BF16)
            vv = rows_of(v_ref, p, g, has_prev)
            lo_k = lax.broadcasted_iota(jnp.int32, vv.shape, 1) < DIL_HD
            zero = jnp.zeros_like(vv)
            lane = lax.broadcasted_iota(jnp.int32, vv.shape, 1)
            sum_a = jnp.where(lane < DIL_HD, 1.0, 0.0).astype(BF16)
            sum_b = jnp.where(lane < DIL_HD, 0.0, 1.0).astype(BF16)
            rhs = jnp.concatenate(
                [jnp.concatenate([jnp.where(lo_k, vv, zero), sum_a], axis=1),
                 jnp.concatenate([jnp.where(lo_k, zero, vv), sum_b], axis=1)], axis=0)
            uw = jnp.dot(e, rhs, preferred_element_type=F32)
            dst = natural_rows(p, g)
            u_ref[p, dst, :] = uw[:, :LANES]
            w_ref[p, dst, :] = uw[:, LANES:]
            yield

    order = tuple(range(len(DILATIONS)))
    region(lambda: _interleave(scores(order[0])))
    for cur, nxt in zip(order, order[1:] + (None,)):
        def both(cur=cur, nxt=nxt):
            _interleave(outputs(cur), *([scores(nxt)] if nxt is not None else []))
        region(both)

    chunk = 2 * BLK

    def combine(i, carry):
        rows = pl.ds(pl.multiple_of(i * chunk, chunk), chunk)
        ms = [m_ref[p, rows, :] for p in range(len(DILATIONS))]
        mmax = functools.reduce(jnp.maximum, ms)
        cs = [jnp.exp2(mp - mmax) for mp in ms]
        num = sum(c * u_ref[p, rows, :] for p, c in enumerate(cs))
        den = sum(c * w_ref[p, rows, :] for p, c in enumerate(cs))
        o_ref[rows, :] = (num / den).astype(o_ref.dtype)
        return carry

    lax.fori_loop(0, seq // chunk, combine, 0, unroll=2)


def _dil_attn(a1, a4, a16, bias, *, batch, seq):
    n_pat = len(DILATIONS)
    n_pair = DIL_HEADS // 2
    arrs = (a1, a4, a16)

    def spec(d, part):
        return pl.BlockSpec((None, d, None, seq // d, LANES),
                            lambda h, b: (b, 0, part * n_pair + h, 0, 0))

    in_specs = [spec(d, part) for d in DILATIONS for part in range(3)]
    in_specs.append(pl.BlockSpec((n_pat, None, 2 * BLK, 2 * BLK), lambda h, b: (0, h, 0, 0)))
    args = [a for a in arrs for _ in range(3)] + [bias]
    return pl.pallas_call(
        functools.partial(_attn_kernel, seq=seq),
        grid=(n_pair, batch),
        in_specs=in_specs,
        out_specs=pl.BlockSpec((None, seq, LANES), lambda h, b: (b, 0, h)),
        out_shape=jax.ShapeDtypeStruct((batch, seq, DIL_W), BF16),
        scratch_shapes=[pltpu.VMEM((n_pat * seq // BLK, 2 * BLK, 2 * BLK), F32),
                        pltpu.VMEM((n_pat * seq // BLK, 2 * BLK, LANES), F32)]
        + [pltpu.VMEM((n_pat, seq, LANES), F32) for _ in range(3)],
        compiler_params=pltpu.CompilerParams(
            dimension_semantics=("arbitrary", "arbitrary"), vmem_limit_bytes=VMEM_LIMIT),
        name="dil_attn",
    )(*args)


def _gla_macro(rows, q_ref, k_ref, la_ref, v_ref, gg_ref, p_ref, lvl, gn_ref, o_ref, st_ref, lo_half):
    q = q_ref[rows, :].astype(F32)
    k = k_ref[rows, :].astype(F32)
    la = la_ref[rows, :]
    la_hi = la.astype(BF16)
    la2 = jnp.concatenate([la_hi, (la - la_hi.astype(F32)).astype(BF16)], axis=1)

    def range_sum(blk):
        r = jnp.dot(p_ref[blk * MACRO:(blk + 1) * MACRO, :], la2, preferred_element_type=F32)
        return r[:, :LANES] + r[:, LANES:]

    b = range_sum(0)
    x_start = jnp.exp(b)
    qg = q * x_start
    kg = k * jnp.exp(range_sum(1))
    a_last = x_start[MACRO - 1:MACRO, :]

    @functools.cache
    def row_bcast(r):
        return jnp.broadcast_to(b[r:r + 1, :], (8, LANES))

    sub = lax.broadcasted_iota(jnp.int32, (8, LANES), 0)

    def ref_rows(l):
        span = 2 << l
        pieces = []
        for g in range(MACRO // 8):
            refs = sorted({(i // span) * span + span // 2 - 1 for i in range(8 * g, 8 * g + 8)})
            piece = row_bcast(refs[-1])
            for c in reversed(range(len(refs) - 1)):
                piece = jnp.where(sub < (c + 1) * span, row_bcast(refs[c]), piece)
            pieces.append(piece)
        return jnp.concatenate(pieces, axis=0)

    att = _dot_nt(_split_heads(q, lo_half).astype(BF16), k.astype(BF16))
    att = jnp.where(lvl == GLA_LEVELS, att, 0.0)
    yield
    for l in range(GLA_LEVELS):
        xl = jnp.exp(-jnp.abs(b - ref_rows(l)))
        a = _dot_nt(_split_heads(q * xl, lo_half).astype(BF16), (k * xl).astype(BF16))
        att = jnp.where(lvl == l, a, att)
        yield
    att = att.astype(BF16)
    st = st_ref[...]
    inter = _dot_nt(_split_heads(qg, lo_half).astype(BF16), st.astype(BF16))
    kgp = kg.astype(BF16)
    upd = []
    for e in range(2):
        hs = slice(GLA_DV * e, GLA_DV * (e + 1))
        vh = v_ref[rows, hs]
        o = (jnp.dot(att[e * MACRO:(e + 1) * MACRO], vh, preferred_element_type=F32)
             + inter[e * MACRO:(e + 1) * MACRO])
        gate = gg_ref[rows, hs].astype(F32)
        o = _rms(o, gn_ref[...]) * (gate / (1.0 + jnp.exp(-gate)))
        o_ref[rows, hs] = o.astype(o_ref.dtype)
        upd.append(_dot_tn(vh, kgp))
    st_ref[...] = st * a_last + jnp.where(lo_half, upd[0], upd[1])
    yield


def _interleave(*gens):
    live = [iter(g) for g in gens]
    while live:
        for g in list(live):
            try:
                next(g)
            except StopIteration:
                live.remove(g)


def _post_kernel(h_ref, og_ref, od_ref, p_ref, wo_ref, w1_ref, w2_ref, wg_ref, wp_ref,
                 g_mix_ref, g_pre_ref, g_post_ref, o_ref, *, ff_chunk):
    mix = (jnp.dot(og_ref[...], wo_ref[:GLA_W, :], preferred_element_type=F32)
           + jnp.dot(od_ref[...], wo_ref[GLA_W:, :], preferred_element_type=F32))
    emb = jnp.dot(p_ref[...].astype(BF16), wp_ref[...], preferred_element_type=F32)
    h1 = h_ref[...] + _rms(mix, g_mix_ref[...])
    xn = _rms(h1, g_pre_ref[...]).astype(BF16)
    f = jnp.zeros_like(h1)
    for c in range(w1_ref.shape[1] // ff_chunk):
        cols = slice(c * ff_chunk, (c + 1) * ff_chunk)
        a = jnp.maximum(jnp.dot(xn, w1_ref[:, cols], preferred_element_type=F32), 0.0)
        f = f + jnp.dot((a * a).astype(BF16), w2_ref[cols, :], preferred_element_type=F32)
    h2 = h1 + _rms(f, g_post_ref[...])
    gate = jnp.dot(h2.astype(BF16), wg_ref[...], preferred_element_type=F32)
    o_ref[...] = h2 + emb / (1.0 + jnp.exp(-gate))


def _post(h, og, od, p, wo, w1, w2, wg, wp, g_mix, g_pre, g_post, *, layer, tm, ff_chunk=1024):
    t, dm = h.shape
    row = lambda width: pl.BlockSpec((tm, width), lambda i: (i, 0))
    const = lambda a: pl.BlockSpec((None,) + a.shape[1:], lambda i: (layer, 0, 0),
                                   pipeline_mode=pl.Buffered(1))
    consts = (wo, w1, w2, wg, wp, g_mix, g_pre, g_post)
    return pl.pallas_call(
        functools.partial(_post_kernel, ff_chunk=ff_chunk),
        grid=(t // tm,),
        in_specs=[row(dm), row(GLA_W), row(DIL_W),
                  pl.BlockSpec((None, tm, p.shape[2]), lambda i: (layer, i, 0))]
        + [const(a) for a in consts],
        out_specs=row(dm),
        out_shape=jax.ShapeDtypeStruct((t, dm), F32),
        compiler_params=pltpu.CompilerParams(
            dimension_semantics=("arbitrary",), vmem_limit_bytes=VMEM_LIMIT),
        name="post",
    )(h, og, od, p, *consts)


def kernel(x, p, w_in, w_gla_a2, b_gla_a, gla_norm_g, w_out, rel_bias, pre_mix_g, post_mix_g,
           pre_mlp_g, post_mlp_g, w_mlp_in, w_mlp_out, w_ple_gate, w_ple_proj):
    batch, seq, dm = x.shape
    depth = w_in.shape[0]
    assert seq % (max(DILATIONS) * BLK) == 0 and seq % MACRO == 0
    t = batch * seq
    rows1 = lambda a: a.reshape(depth, 1, -1)
    bias = _bias_table(rel_bias)
    h = x.reshape(t, dm)
    w = _prep_w_in(w_in)
    wa2 = jnp.pad(w_gla_a2, ((0, 0), (0, C_END - C_LR - GLA_LOWRANK), (0, 0))).astype(BF16)
    wo, w1, w2, wg, wp = (a.astype(BF16) for a in (w_out, w_mlp_in, w_mlp_out, w_ple_gate, w_ple_proj))
    p2 = p.reshape(depth, t, -1)
    for i in range(depth):
        gq, gk, la, gv, gg, a1, a4, a16 = _in_proj(
            h, rows1(pre_mix_g), w, wa2, rows1(b_gla_a), layer=i, batch=batch, seq=seq, tm=512)
        og = _gla(gq, gk, la, gv, gg, gla_norm_g[i].reshape(1, -1), batch=batch, seq=seq)
        od = _dil_attn(a1, a4, a16, bias, batch=batch, seq=seq)
        h = _post(h, og.reshape(t, GLA_W), od.reshape(t, DIL_W), p2, wo, w1, w2, wg, wp,
                  rows1(post_mix_g), rows1(pre_mlp_g), rows1(post_mlp_g), layer=i, tm=512)
    return h.reshape(batch, seq, dm)
```

```python
import functools
import math

import numpy as np
import jax
import jax.numpy as jnp
from jax import lax
from jax.experimental import pallas as pl
from jax.experimental.pallas import tpu as pltpu

F32 = jnp.float32
BF16 = jnp.bfloat16

EPS = 1e-6
GLA_HEADS = 4
GLA_DK = 64
GLA_DV = 128
GLA_QK = GLA_HEADS * GLA_DK
GLA_W = GLA_HEADS * GLA_DV
GLA_LOWRANK = 16
GLA_TAU = 16.0
DIL_HEADS = 8
DIL_HD = 64
DIL_W = DIL_HEADS * DIL_HD
DILATIONS = (1, 4, 16)
BAND = 128
BLK = 128
REL_BUCKETS = 32
REL_MAX_DIST = 2048
NEG = -1e30
LOG2E = math.log2(math.e)

LANES = 128
SUBLANES = 8
MACRO = 128
GLA_LEVELS = 7
VMEM_LIMIT = 56 * 1024 * 1024
IN_PROJ_ROWS = 512
POST_ROWS = 512

C_GQ, C_GK, C_GV, C_GG, C_DQ, C_LR, C_END = 0, 256, 512, 1024, 1536, 3072, 3200


def _rms(x, g):
    return x * lax.rsqrt(jnp.mean(x * x, axis=-1, keepdims=True) + EPS) * g


def _w_in_kernel(wt_ref, o_ref, *, lr0):
    lr1 = lr0 + GLA_LOWRANK
    tk = wt_ref.shape[1]
    o_ref[:, :lr0] = wt_ref[:lr0, :].T.astype(BF16)
    o_ref[:, lr0:C_LR] = wt_ref[lr1:, :].T.astype(BF16)
    o_ref[:, C_LR:] = jnp.concatenate(
        [wt_ref[lr0:lr1, :].T, jnp.zeros((tk, C_END - C_LR - GLA_LOWRANK), F32)], axis=1).astype(BF16)


def _prep_w_in(w_in, *, tk=256):
    depth, dm, width = w_in.shape
    lr0 = C_DQ
    assert width - GLA_LOWRANK == C_LR
    return pl.pallas_call(
        functools.partial(_w_in_kernel, lr0=lr0),
        grid=(depth, dm // tk),
        in_specs=[pl.BlockSpec((None, width, tk), lambda l, i: (l, 0, i))],
        out_specs=pl.BlockSpec((None, tk, C_END), lambda l, i: (l, i, 0)),
        out_shape=jax.ShapeDtypeStruct((depth, dm, C_END), BF16),
        compiler_params=pltpu.CompilerParams(dimension_semantics=("arbitrary", "arbitrary")),
        name="prep_w_in",
    )(jnp.swapaxes(w_in, 1, 2))


def _in_proj_kernel(h_ref, g_ref, w_ref, wa2_ref, ba_ref,
                    gq_ref, gk_ref, la_ref, gv_ref, gg_ref, a1_ref, a4_ref, a16_ref,
                    slab_ref, slab4_ref, *, tm):
    xn = _rms(h_ref[...], g_ref[...]).astype(BF16)

    def proj(lo, hi):
        return jnp.dot(xn, w_ref[:, lo:hi], preferred_element_type=F32)

    z = jnp.dot(proj(C_LR, C_END).astype(BF16), wa2_ref[...],
                preferred_element_type=F32) + ba_ref[...]
    la_ref[...] = (jnp.minimum(z, 0.0) - jnp.log1p(jnp.exp(-jnp.abs(z)))) * (1.0 / GLA_TAU)

    n_grp = DIL_W // LANES

    def project_dilated(c):
        y = proj(C_DQ + DIL_W * c, C_DQ + DIL_W * (c + 1))
        if c == 0:
            y = y * (LOG2E * DIL_HD ** -0.5)
        for s in range(n_grp):
            ys = y[:, LANES * s:LANES * (s + 1)]
            idx = c * n_grp + s
            a1_ref[0, idx] = ys.astype(BF16)
            slab_ref[idx] = ys

    def deinterleave(c):
        for idx in range(c * n_grp, (c + 1) * n_grp):
            for r in range(4):
                y4 = slab_ref[idx, pl.ds(r, tm // 4, stride=4), :]
                a4_ref[r, idx] = y4.astype(BF16)
                slab4_ref[idx, r] = y4
            for r in range(16):
                a16_ref[r, idx] = slab4_ref[idx, r % 4, pl.ds(r // 4, tm // 16, stride=4), :].astype(BF16)

    project_dilated(0)
    project_dilated(1)
    deinterleave(0)
    project_dilated(2)
    deinterleave(1)
    gq_ref[...] = (proj(C_GQ, C_GK) * (GLA_DK ** -0.5)).astype(BF16)
    gk_ref[...] = proj(C_GK, C_GV).astype(BF16)
    deinterleave(2)
    gv_ref[...] = proj(C_GV, C_GG).astype(BF16)
    gg_ref[...] = proj(C_GG, C_DQ).astype(BF16)


def _in_proj(h, g, w, wa2, ba, *, layer, batch, seq, tm):
    t = batch * seq
    nt = seq // tm
    row = lambda width: pl.BlockSpec((tm, width), lambda b, i: (b * nt + i, 0))
    const = lambda shape: pl.BlockSpec((None,) + shape[1:], lambda b, i: (layer, 0, 0))
    n_slab = 3 * DIL_W // LANES
    dil_spec = lambda d: pl.BlockSpec((None, d, n_slab, tm // d, LANES), lambda b, i: (b, 0, 0, i, 0))
    out_shape = (
        jax.ShapeDtypeStruct((t, GLA_QK), BF16), jax.ShapeDtypeStruct((t, GLA_QK), BF16),
        jax.ShapeDtypeStruct((t, GLA_QK), F32),
        jax.ShapeDtypeStruct((t, GLA_W), BF16), jax.ShapeDtypeStruct((t, GLA_W), BF16),
    ) + tuple(jax.ShapeDtypeStruct((batch, d, n_slab, seq // d, LANES), BF16) for d in DILATIONS)
    return pl.pallas_call(
        functools.partial(_in_proj_kernel, tm=tm),
        grid=(batch, nt),
        in_specs=[row(h.shape[1]), const(g.shape), const(w.shape), const(wa2.shape), const(ba.shape)],
        out_specs=(row(GLA_QK), row(GLA_QK), row(GLA_QK), row(GLA_W), row(GLA_W))
        + tuple(dil_spec(d) for d in DILATIONS),
        out_shape=out_shape,
        scratch_shapes=[pltpu.VMEM((n_slab, tm, LANES), F32),
                        pltpu.VMEM((n_slab, 4, tm // 4, LANES), F32)],
        compiler_params=pltpu.CompilerParams(
            dimension_semantics=("arbitrary", "arbitrary"), vmem_limit_bytes=VMEM_LIMIT),
        name="in_proj",
    )(h, g, w, wa2, ba)


def _gla_constants():
    n = MACRO
    i = np.arange(n)[:, None]
    t = np.arange(n)[None, :]
    p = np.concatenate([t <= i, t > i], axis=0).astype(np.float32)
    j = np.arange(n)[None, :]
    lvl = np.where(j < i, np.floor(np.log2(np.maximum(i ^ j, 1))).astype(np.int32),
                   np.where(j == i, GLA_LEVELS, -1)).astype(np.int32)
    return p, np.concatenate([lvl, lvl], axis=0)


def _split_heads(x, lo_half):
    zero = jnp.zeros_like(x)
    return jnp.concatenate([jnp.where(lo_half, x, zero), jnp.where(lo_half, zero, x)], axis=0)


def _dot_nt(a, b):
    return lax.dot_general(a, b, (((1,), (1,)), ((), ())), preferred_element_type=F32)


def _dot_tn(a, b):
    return lax.dot_general(a, b, (((0,), (0,)), ((), ())), preferred_element_type=F32)


def _interleave(*gens):
    live = [iter(g) for g in gens]
    while live:
        for g in list(live):
            try:
                next(g)
            except StopIteration:
                live.remove(g)


def _gla_macro(rows, q_ref, k_ref, la_ref, v_ref, gg_ref, p_ref, lvl, gn_ref, o_ref, st_ref, lo_half):
    q = q_ref[rows, :].astype(F32)
    k = k_ref[rows, :].astype(F32)
    la = la_ref[rows, :]
    la_hi = la.astype(BF16)
    la2 = jnp.concatenate([la_hi, (la - la_hi.astype(F32)).astype(BF16)], axis=1)

    def range_sum(blk):
        r = jnp.dot(p_ref[blk * MACRO:(blk + 1) * MACRO, :], la2, preferred_element_type=F32)
        return r[:, :LANES] + r[:, LANES:]

    b = range_sum(0)
    x_start = jnp.exp(b)
    qg = q * x_start
    kg = k * jnp.exp(range_sum(1))
    a_last = x_start[MACRO - 1:MACRO, :]

    @functools.cache
    def row_bcast(r):
        return jnp.broadcast_to(b[r:r + 1, :], (SUBLANES, LANES))

    sub = lax.broadcasted_iota(jnp.int32, (SUBLANES, LANES), 0)

    def ref_rows(l):
        span = 2 << l
        pieces = []
        for g in range(MACRO // SUBLANES):
            group_rows = range(SUBLANES * g, SUBLANES * (g + 1))
            refs = sorted({(i // span) * span + span // 2 - 1 for i in group_rows})
            piece = row_bcast(refs[-1])
            for c in reversed(range(len(refs) - 1)):
                piece = jnp.where(sub < (c + 1) * span, row_bcast(refs[c]), piece)
            pieces.append(piece)
        return jnp.concatenate(pieces, axis=0)

    att = _dot_nt(_split_heads(q, lo_half).astype(BF16), k.astype(BF16))
    att = jnp.where(lvl == GLA_LEVELS, att, 0.0)
    yield
    for l in range(GLA_LEVELS):
        xl = jnp.exp(-jnp.abs(b - ref_rows(l)))
        a = _dot_nt(_split_heads(q * xl, lo_half).astype(BF16), (k * xl).astype(BF16))
        att = jnp.where(lvl == l, a, att)
        yield
    att = att.astype(BF16)
    st = st_ref[...]
    inter = _dot_nt(_split_heads(qg, lo_half).astype(BF16), st.astype(BF16))
    kgp = kg.astype(BF16)
    upd = []
    for e in range(2):
        hs = slice(GLA_DV * e, GLA_DV * (e + 1))
        vh = v_ref[rows, hs]
        o = (jnp.dot(att[e * MACRO:(e + 1) * MACRO], vh, preferred_element_type=F32)
             + inter[e * MACRO:(e + 1) * MACRO])
        gate = gg_ref[rows, hs].astype(F32)
        o = _rms(o, gn_ref[...]) * (gate / (1.0 + jnp.exp(-gate)))
        o_ref[rows, hs] = o.astype(o_ref.dtype)
        upd.append(_dot_tn(vh, kgp))
    st_ref[...] = st * a_last + jnp.where(lo_half, upd[0], upd[1])
    yield


def _gla_kernel(q_ref, k_ref, la_ref, v_ref, gg_ref, p_ref, lvl_ref, gn_ref, o_ref, st_ref,
                *, n_macro, group, unroll):
    st_ref[...] = jnp.zeros_like(st_ref)
    lo_half = lax.broadcasted_iota(jnp.int32, (MACRO, LANES), 1) < (LANES // 2)

    lvl = lvl_ref[...]
    n_pair = GLA_HEADS // 2

    def body(i, carry):
        streams = []
        for mm in range(group):
            rows = pl.ds(pl.multiple_of((i * group + mm) * MACRO, MACRO), MACRO)
            for pair in range(n_pair):
                qk = pl.ds(LANES * pair, LANES)
                vg = pl.ds(2 * GLA_DV * pair, 2 * GLA_DV)
                streams.append(_gla_macro(
                    rows, q_ref.at[:, qk], k_ref.at[:, qk], la_ref.at[:, qk], v_ref.at[:, vg],
                    gg_ref.at[:, vg], p_ref, lvl, gn_ref, o_ref.at[:, vg], st_ref.at[pair], lo_half))
        _interleave(*streams)
        return carry

    lax.fori_loop(0, n_macro // group, body, 0, unroll=unroll)


def _gla(gq, gk, la, gv, gg, gn, *, batch, seq, group=2, unroll=4):
    p_np, lvl_np = _gla_constants()
    pm = jnp.asarray(p_np, BF16)
    lvl = jnp.asarray(lvl_np)
    seq_spec = lambda width: pl.BlockSpec((None, seq, width), lambda b: (b, 0, 0))
    const = lambda shape: pl.BlockSpec(shape, lambda b: (0,) * len(shape))
    r3 = lambda a: a.reshape(batch, seq, a.shape[-1])
    return pl.pallas_call(
        functools.partial(_gla_kernel, n_macro=seq // MACRO, group=group, unroll=unroll),
        grid=(batch,),
        in_specs=[seq_spec(GLA_QK), seq_spec(GLA_QK), seq_spec(GLA_QK), seq_spec(GLA_W),
                  seq_spec(GLA_W), const(pm.shape), const(lvl.shape), const(gn.shape)],
        out_specs=seq_spec(GLA_W),
        out_shape=jax.ShapeDtypeStruct((batch, seq, GLA_W), BF16),
        scratch_shapes=[pltpu.VMEM((GLA_HEADS // 2, GLA_DV, LANES), F32)],
        compiler_params=pltpu.CompilerParams(
            dimension_semantics=("arbitrary",), vmem_limit_bytes=VMEM_LIMIT),
        name="gla",
    )(r3(gq), r3(gk), r3(la), r3(gv), r3(gg), pm, lvl, gn)


def _t5_bucket_np(dist):
    max_exact = REL_BUCKETS // 2
    d = np.maximum(dist, 1).astype(np.float32)
    large = max_exact + (np.log(d / np.float32(max_exact)) / np.float32(math.log(REL_MAX_DIST / max_exact))
                         * np.float32(REL_BUCKETS - max_exact)).astype(np.int32)
    large = np.minimum(large, REL_BUCKETS - 1)
    return np.where(dist < max_exact, dist, large).astype(np.int32)


def _bucket_table():
    qi = np.arange(BLK)[:, None]
    ki = np.arange(2 * BLK)[None, :]
    j = qi + BLK - ki
    valid = (j >= 0) & (j <= BAND)
    return np.stack([np.where(valid, _t5_bucket_np(np.maximum(j, 0) * d), -1) for d in DILATIONS]).astype(np.int32)


def _bias_kernel(rel_ref, bkt_ref, o_ref, *, buckets):
    for p, used in enumerate(buckets):
        bkt = bkt_ref[p]
        accs = [jnp.full(bkt.shape, NEG, F32) for _ in range(DIL_HEADS)]
        for u in used:
            hit = bkt == u
            accs = [jnp.where(hit, rel_ref[u, h] * LOG2E, a) for h, a in enumerate(accs)]
        for h, a in enumerate(accs):
            o_ref[p, h // 2, (h % 2) * BLK:(h % 2 + 1) * BLK, :] = a


def _bias_table(rel_bias):
    bkt_np = _bucket_table()
    buckets = tuple(tuple(int(u) for u in np.unique(b) if u >= 0) for b in bkt_np)
    n_pat = len(DILATIONS)
    return pl.pallas_call(
        functools.partial(_bias_kernel, buckets=buckets),
        in_specs=[pl.BlockSpec(memory_space=pltpu.SMEM), pl.BlockSpec(memory_space=pltpu.VMEM)],
        out_specs=pl.BlockSpec(memory_space=pltpu.VMEM),
        out_shape=jax.ShapeDtypeStruct((n_pat, DIL_HEADS // 2, 2 * BLK, 2 * BLK), F32),
        name="bias_table",
    )(rel_bias, jnp.asarray(bkt_np))


def _attn_kernel(q1, k1, v1, q4, k4, v4, q16, k16, v16, bias_ref, o_ref,
                 s_ref, mb_ref, u_ref, w_ref, m_ref, *, seq):
    lo_half = lax.broadcasted_iota(jnp.int32, (BLK, LANES), 1) < DIL_HD
    refs = ((q1, k1, v1), (q4, k4, v4), (q16, k16, v16))
    nblk = seq // BLK

    def rows_of(ref, p, g, with_prev=False):
        r, n = divmod(g, nblk // DILATIONS[p])
        return ref[r, pl.ds((n - 1) * BLK, 2 * BLK) if with_prev else pl.ds(n * BLK, BLK), :]

    one_trip = jnp.minimum(pl.program_id(0) + 1, 1)

    def region(fn):
        lax.fori_loop(0, one_trip, lambda i, c: (fn(), c)[1], 0)

    def natural_rows(p, g):
        d = DILATIONS[p]
        r, n = divmod(g, nblk // d)
        return pl.ds(n * (BLK * d) + r, BLK, stride=d) if d > 1 else pl.ds(g * BLK, BLK)

    def scores(p):
        q_ref, k_ref, _ = refs[p]
        nb = nblk // DILATIONS[p]
        for g in range(nblk):
            has_prev = g % nb > 0
            nk = 2 * BLK if has_prev else BLK
            bias = bias_ref[p] if has_prev else bias_ref[p, :, BLK:]
            q = rows_of(q_ref, p, g)
            s = _dot_nt(_split_heads(q, lo_half), rows_of(k_ref, p, g, has_prev)) + bias
            s_ref[p * nblk + g, :, :nk] = s
            mb = jnp.broadcast_to(jnp.max(s, axis=-1, keepdims=True), (2 * BLK, LANES))
            mb_ref[p * nblk + g] = mb
            m_ref[p, natural_rows(p, g), :] = jnp.where(lo_half, mb[:BLK], mb[BLK:])
            yield

    def outputs(p):
        _, _, v_ref = refs[p]
        nb = nblk // DILATIONS[p]
        for g in range(nblk):
            has_prev = g % nb > 0
            nk = 2 * BLK if has_prev else BLK
            idx = p * nblk + g
            e = jnp.concatenate(
                [jnp.exp2(s_ref[idx, pl.ds(h * BLK, BLK), c * LANES:(c + 1) * LANES]
                          - mb_ref[idx, pl.ds(h * BLK, BLK), :])
                 for h in range(2) for c in range(nk // LANES)], axis=1).astype(BF16)
            vv = rows_of(v_ref, p, g, has_prev)
            lo_k = lax.broadcasted_iota(jnp.int32, vv.shape, 1) < DIL_HD
            zero = jnp.zeros_like(vv)
            lane = lax.broadcasted_iota(jnp.int32, vv.shape, 1)
            sum_a = jnp.where(lane < DIL_HD, 1.0, 0.0).astype(BF16)
            sum_b = jnp.where(lane < DIL_HD, 0.0, 1.0).astype(BF16)
            rhs = jnp.concatenate(
                [jnp.concatenate([jnp.where(lo_k, vv, zero), sum_a], axis=1),
                 jnp.concatenate([jnp.where(lo_k, zero, vv), sum_b], axis=1)], axis=0)
            uw = jnp.dot(e, rhs, preferred_element_type=F32)
            dst = natural_rows(p, g)
            u_ref[p, dst, :] = uw[:, :LANES]
            w_ref[p, dst, :] = uw[:, LANES:]
            yield

    order = tuple(range(len(DILATIONS)))
    region(lambda: _interleave(scores(order[0])))
    for cur, nxt in zip(order, order[1:] + (None,)):
        def both(cur=cur, nxt=nxt):
            _interleave(outputs(cur), *([scores(nxt)] if nxt is not None else []))
        region(both)

    chunk = 2 * BLK

    def combine(i, carry):
        rows = pl.ds(pl.multiple_of(i * chunk, chunk), chunk)
        ms = [m_ref[p, rows, :] for p in range(len(DILATIONS))]
        mmax = functools.reduce(jnp.maximum, ms)
        cs = [jnp.exp2(mp - mmax) for mp in ms]
        num = sum(c * u_ref[p, rows, :] for p, c in enumerate(cs))
        den = sum(c * w_ref[p, rows, :] for p, c in enumerate(cs))
        o_ref[rows, :] = (num / den).astype(o_ref.dtype)
        return carry

    lax.fori_loop(0, seq // chunk, combine, 0, unroll=2)


def _dil_attn(a1, a4, a16, bias, *, batch, seq):
    n_pat = len(DILATIONS)
    n_pair = DIL_HEADS // 2
    arrs = (a1, a4, a16)

    def spec(d, part):
        return pl.BlockSpec((None, d, None, seq // d, LANES),
                            lambda h, b: (b, 0, part * n_pair + h, 0, 0))

    in_specs = [spec(d, part) for d in DILATIONS for part in range(3)]
    in_specs.append(pl.BlockSpec((n_pat, None, 2 * BLK, 2 * BLK), lambda h, b: (0, h, 0, 0)))
    args = [a for a in arrs for _ in range(3)] + [bias]
    return pl.pallas_call(
        functools.partial(_attn_kernel, seq=seq),
        grid=(n_pair, batch),
        in_specs=in_specs,
        out_specs=pl.BlockSpec((None, seq, LANES), lambda h, b: (b, 0, h)),
        out_shape=jax.ShapeDtypeStruct((batch, seq, DIL_W), BF16),
        scratch_shapes=[pltpu.VMEM((n_pat * seq // BLK, 2 * BLK, 2 * BLK), F32),
                        pltpu.VMEM((n_pat * seq // BLK, 2 * BLK, LANES), F32)]
        + [pltpu.VMEM((n_pat, seq, LANES), F32) for _ in range(3)],
        compiler_params=pltpu.CompilerParams(
            dimension_semantics=("arbitrary", "arbitrary"), vmem_limit_bytes=VMEM_LIMIT),
        name="dil_attn",
    )(*args)


def _post_kernel(h_ref, og_ref, od_ref, p_ref, wo_ref, w1_ref, w2_ref, wg_ref, wp_ref,
                 g_mix_ref, g_pre_ref, g_post_ref, o_ref, *, ff_chunk):
    mix = (jnp.dot(og_ref[...], wo_ref[:GLA_W, :], preferred_element_type=F32)
           + jnp.dot(od_ref[...], wo_ref[GLA_W:, :], preferred_element_type=F32))
    emb = jnp.dot(p_ref[...].astype(BF16), wp_ref[...], preferred_element_type=F32)
    h1 = h_ref[...] + _rms(mix, g_mix_ref[...])
    xn = _rms(h1, g_pre_ref[...]).astype(BF16)
    f = jnp.zeros_like(h1)
    for c in range(w1_ref.shape[1] // ff_chunk):
        cols = slice(c * ff_chunk, (c + 1) * ff_chunk)
        a = jnp.maximum(jnp.dot(xn, w1_ref[:, cols], preferred_element_type=F32), 0.0)
        f = f + jnp.dot((a * a).astype(BF16), w2_ref[cols, :], preferred_element_type=F32)
    h2 = h1 + _rms(f, g_post_ref[...])
    gate = jnp.dot(h2.astype(BF16), wg_ref[...], preferred_element_type=F32)
    o_ref[...] = h2 + emb / (1.0 + jnp.exp(-gate))


def _post(h, og, od, p, wo, w1, w2, wg, wp, g_mix, g_pre, g_post, *, layer, tm, ff_chunk=1024):
    t, dm = h.shape
    row = lambda width: pl.BlockSpec((tm, width), lambda i: (i, 0))
    const = lambda a: pl.BlockSpec((None,) + a.shape[1:], lambda i: (layer, 0, 0),
                                   pipeline_mode=pl.Buffered(1))
    consts = (wo, w1, w2, wg, wp, g_mix, g_pre, g_post)
    return pl.pallas_call(
        functools.partial(_post_kernel, ff_chunk=ff_chunk),
        grid=(t // tm,),
        in_specs=[row(dm), row(GLA_W), row(DIL_W),
                  pl.BlockSpec((None, tm, p.shape[2]), lambda i: (layer, i, 0))]
        + [const(a) for a in consts],
        out_specs=row(dm),
        out_shape=jax.ShapeDtypeStruct((t, dm), F32),
        compiler_params=pltpu.CompilerParams(
            dimension_semantics=("arbitrary",), vmem_limit_bytes=VMEM_LIMIT),
        name="post",
    )(h, og, od, p, *consts)


def kernel(x, p, w_in, w_gla_a2, b_gla_a, gla_norm_g, w_out, rel_bias, pre_mix_g, post_mix_g,
           pre_mlp_g, post_mlp_g, w_mlp_in, w_mlp_out, w_ple_gate, w_ple_proj):
    batch, seq, dm = x.shape
    depth = w_in.shape[0]
    assert seq % (max(DILATIONS) * BLK) == 0 and seq % MACRO == 0
    t = batch * seq
    rows1 = lambda a: a.reshape(depth, 1, -1)
    bias = _bias_table(rel_bias)
    h = x.reshape(t, dm)
    w = _prep_w_in(w_in)
    wa2 = jnp.pad(w_gla_a2, ((0, 0), (0, C_END - C_LR - GLA_LOWRANK), (0, 0))).astype(BF16)
    wo, w1, w2, wg, wp = (a.astype(BF16) for a in (w_out, w_mlp_in, w_mlp_out, w_ple_gate, w_ple_proj))
    p2 = p.reshape(depth, t, -1)
    for i in range(depth):
        gq, gk, la, gv, gg, a1, a4, a16 = _in_proj(
            h, rows1(pre_mix_g), w, wa2, rows1(b_gla_a), layer=i, batch=batch, seq=seq, tm=IN_PROJ_ROWS)
        og = _gla(gq, gk, la, gv, gg, gla_norm_g[i].reshape(1, -1), batch=batch, seq=seq)
        od = _dil_attn(a1, a4, a16, bias, batch=batch, seq=seq)
        h = _post(h, og.reshape(t, GLA_W), od.reshape(t, DIL_W), p2, wo, w1, w2, wg, wp,
                  rows1(post_mix_g), rows1(pre_mlp_g), rows1(post_mlp_g), layer=i, tm=POST_ROWS)
    return h.reshape(batch, seq, dm)
```

```python
import functools
import math

import numpy as np
import jax
import jax.numpy as jnp
from jax import lax
from jax.experimental import pallas as pl
from jax.experimental.pallas import tpu as pltpu

F32 = jnp.float32
BF16 = jnp.bfloat16

EPS = 1e-6
GLA_HEADS = 4
GLA_DK = 64
GLA_DV = 128
GLA_QK = GLA_HEADS * GLA_DK
GLA_W = GLA_HEADS * GLA_DV
GLA_LOWRANK = 16
GLA_TAU = 16.0
DIL_HEADS = 8
DIL_HD = 64
DIL_W = DIL_HEADS * DIL_HD
DILATIONS = (1, 4, 16)
BAND = 128
BLK = 128
REL_BUCKETS = 32
REL_MAX_DIST = 2048
NEG = -1e30
LOG2E = math.log2(math.e)

LANES = 128
SUBLANES = 8
MACRO = 128
GLA_LEVELS = 7
VMEM_LIMIT = 56 * 1024 * 1024
IN_PROJ_ROWS = 512
POST_ROWS = 512

C_GQ, C_GK, C_GV, C_GG, C_DQ, C_LR, C_END = 0, 256, 512, 1024, 1536, 3072, 3200


def _rms(x, g):
    return x * lax.rsqrt(jnp.mean(x * x, axis=-1, keepdims=True) + EPS) * g


def _w_in_kernel(wt_ref, o_ref, *, lr0):
    lr1 = lr0 + GLA_LOWRANK
    tk = wt_ref.shape[1]
    o_ref[:, :lr0] = wt_ref[:lr0, :].T.astype(BF16)
    o_ref[:, lr0:C_LR] = wt_ref[lr1:, :].T.astype(BF16)
    o_ref[:, C_LR:] = jnp.concatenate(
        [wt_ref[lr0:lr1, :].T, jnp.zeros((tk, C_END - C_LR - GLA_LOWRANK), F32)], axis=1).astype(BF16)


def _prep_w_in(w_in, *, tk=512):
    depth, dm, width = w_in.shape
    lr0 = C_DQ
    assert width - GLA_LOWRANK == C_LR
    return pl.pallas_call(
        functools.partial(_w_in_kernel, lr0=lr0),
        grid=(depth, dm // tk),
        in_specs=[pl.BlockSpec((None, width, tk), lambda l, i: (l, 0, i))],
        out_specs=pl.BlockSpec((None, tk, C_END), lambda l, i: (l, i, 0)),
        out_shape=jax.ShapeDtypeStruct((depth, dm, C_END), BF16),
        compiler_params=pltpu.CompilerParams(
            dimension_semantics=("arbitrary", "arbitrary"), vmem_limit_bytes=VMEM_LIMIT),
        name="prep_w_in",
    )(jnp.swapaxes(w_in, 1, 2))


def _in_proj_kernel(h_ref, g_ref, w_ref, wa2_ref, ba_ref,
                    gq_ref, gk_ref, la_ref, gv_ref, gg_ref, a1_ref, a4_ref, a16_ref,
                    slab_ref, slab4_ref, *, tm):
    xn = _rms(h_ref[...], g_ref[...]).astype(BF16)

    def proj(lo, hi):
        return jnp.dot(xn, w_ref[:, lo:hi], preferred_element_type=F32)

    z = jnp.dot(proj(C_LR, C_END).astype(BF16), wa2_ref[...],
                preferred_element_type=F32) + ba_ref[...]
    la_ref[...] = (jnp.minimum(z, 0.0) - jnp.log1p(jnp.exp(-jnp.abs(z)))) * (1.0 / GLA_TAU)

    n_grp = DIL_W // LANES

    def project_dilated(c):
        y = proj(C_DQ + DIL_W * c, C_DQ + DIL_W * (c + 1))
        if c == 0:
            y = y * (LOG2E * DIL_HD ** -0.5)
        for s in range(n_grp):
            ys = y[:, LANES * s:LANES * (s + 1)]
            idx = c * n_grp + s
            a1_ref[0, idx] = ys.astype(BF16)
            slab_ref[idx] = ys

    def deinterleave(c):
        for idx in range(c * n_grp, (c + 1) * n_grp):
            for r in range(4):
                y4 = slab_ref[idx, pl.ds(r, tm // 4, stride=4), :]
                a4_ref[r, idx] = y4.astype(BF16)
                slab4_ref[idx, r] = y4
            for r in range(16):
                a16_ref[r, idx] = slab4_ref[idx, r % 4, pl.ds(r // 4, tm // 16, stride=4), :].astype(BF16)

    project_dilated(0)
    project_dilated(1)
    deinterleave(0)
    project_dilated(2)
    deinterleave(1)
    gq_ref[...] = (proj(C_GQ, C_GK) * (GLA_DK ** -0.5)).astype(BF16)
    gk_ref[...] = proj(C_GK, C_GV).astype(BF16)
    deinterleave(2)
    gv_ref[...] = proj(C_GV, C_GG).astype(BF16)
    gg_ref[...] = proj(C_GG, C_DQ).astype(BF16)


def _in_proj(h, g, w, wa2, ba, *, layer, batch, seq, tm):
    t = batch * seq
    nt = seq // tm
    row = lambda width: pl.BlockSpec((tm, width), lambda b, i: (b * nt + i, 0))
    const = lambda shape: pl.BlockSpec((None,) + shape[1:], lambda b, i: (layer, 0, 0))
    n_slab = 3 * DIL_W // LANES
    dil_spec = lambda d: pl.BlockSpec((None, d, n_slab, tm // d, LANES), lambda b, i: (b, 0, 0, i, 0))
    out_shape = (
        jax.ShapeDtypeStruct((t, GLA_QK), BF16), jax.ShapeDtypeStruct((t, GLA_QK), BF16),
        jax.ShapeDtypeStruct((t, GLA_QK), F32),
        jax.ShapeDtypeStruct((t, GLA_W), BF16), jax.ShapeDtypeStruct((t, GLA_W), BF16),
    ) + tuple(jax.ShapeDtypeStruct((batch, d, n_slab, seq // d, LANES), BF16) for d in DILATIONS)
    return pl.pallas_call(
        functools.partial(_in_proj_kernel, tm=tm),
        grid=(batch, nt),
        in_specs=[row(h.shape[1]), const(g.shape), const(w.shape), const(wa2.shape), const(ba.shape)],
        out_specs=(row(GLA_QK), row(GLA_QK), row(GLA_QK), row(GLA_W), row(GLA_W))
        + tuple(dil_spec(d) for d in DILATIONS),
        out_shape=out_shape,
        scratch_shapes=[pltpu.VMEM((n_slab, tm, LANES), F32),
                        pltpu.VMEM((n_slab, 4, tm // 4, LANES), F32)],
        compiler_params=pltpu.CompilerParams(
            dimension_semantics=("arbitrary", "arbitrary"), vmem_limit_bytes=VMEM_LIMIT),
        name="in_proj",
    )(h, g, w, wa2, ba)


def _gla_constants():
    n = MACRO
    i = np.arange(n)[:, None]
    t = np.arange(n)[None, :]
    p = np.concatenate([t <= i, t > i], axis=0).astype(np.float32)
    j = np.arange(n)[None, :]
    lvl = np.where(j < i, np.floor(np.log2(np.maximum(i ^ j, 1))).astype(np.int32),
                   np.where(j == i, GLA_LEVELS, -1)).astype(np.int32)
    return p, np.concatenate([lvl, lvl], axis=0)


def _split_heads(x, lo_half):
    zero = jnp.zeros_like(x)
    return jnp.concatenate([jnp.where(lo_half, x, zero), jnp.where(lo_half, zero, x)], axis=0)


def _dot_nt(a, b):
    return lax.dot_general(a, b, (((1,), (1,)), ((), ())), preferred_element_type=F32)


def _dot_tn(a, b):
    return lax.dot_general(a, b, (((0,), (0,)), ((), ())), preferred_element_type=F32)


def _interleave(*gens):
    live = [iter(g) for g in gens]
    while live:
        for g in list(live):
            try:
                next(g)
            except StopIteration:
                live.remove(g)


def _gla_macro(rows, q_ref, k_ref, la_ref, v_ref, gg_ref, p_ref, lvl, gn_ref, o_ref, st_ref, lo_half):
    q = q_ref[rows, :].astype(F32)
    k = k_ref[rows, :].astype(F32)
    la = la_ref[rows, :]
    la_hi = la.astype(BF16)
    la2 = jnp.concatenate([la_hi, (la - la_hi.astype(F32)).astype(BF16)], axis=1)

    def range_sum(blk):
        r = jnp.dot(p_ref[blk * MACRO:(blk + 1) * MACRO, :], la2, preferred_element_type=F32)
        return r[:, :LANES] + r[:, LANES:]

    b = range_sum(0)
    x_start = jnp.exp(b)
    qg = q * x_start
    kg = k * jnp.exp(range_sum(1))
    a_last = x_start[MACRO - 1:MACRO, :]

    @functools.cache
    def row_bcast(r):
        return jnp.broadcast_to(b[r:r + 1, :], (SUBLANES, LANES))

    sub = lax.broadcasted_iota(jnp.int32, (SUBLANES, LANES), 0)

    def ref_rows(l):
        span = 2 << l
        pieces = []
        for g in range(MACRO // SUBLANES):
            group_rows = range(SUBLANES * g, SUBLANES * (g + 1))
            refs = sorted({(i // span) * span + span // 2 - 1 for i in group_rows})
            piece = row_bcast(refs[-1])
            for c in reversed(range(len(refs) - 1)):
                piece = jnp.where(sub < (c + 1) * span, row_bcast(refs[c]), piece)
            pieces.append(piece)
        return jnp.concatenate(pieces, axis=0)

    att = _dot_nt(_split_heads(q, lo_half).astype(BF16), k.astype(BF16))
    att = jnp.where(lvl == GLA_LEVELS, att, 0.0)
    yield
    for l in range(GLA_LEVELS):
        xl = jnp.exp(-jnp.abs(b - ref_rows(l)))
        a = _dot_nt(_split_heads(q * xl, lo_half).astype(BF16), (k * xl).astype(BF16))
        att = jnp.where(lvl == l, a, att)
        yield
    att = att.astype(BF16)
    st = st_ref[...]
    inter = _dot_nt(_split_heads(qg, lo_half).astype(BF16), st.astype(BF16))
    kgp = kg.astype(BF16)
    upd = []
    for e in range(2):
        hs = slice(GLA_DV * e, GLA_DV * (e + 1))
        vh = v_ref[rows, hs]
        o = (jnp.dot(att[e * MACRO:(e + 1) * MACRO], vh, preferred_element_type=F32)
             + inter[e * MACRO:(e + 1) * MACRO])
        gate = gg_ref[rows, hs].astype(F32)
        o = _rms(o, gn_ref[...]) * (gate / (1.0 + jnp.exp(-gate)))
        o_ref[rows, hs] = o.astype(o_ref.dtype)
        upd.append(_dot_tn(vh, kgp))
    st_ref[...] = st * a_last + jnp.where(lo_half, upd[0], upd[1])
    yield


def _gla_kernel(q_ref, k_ref, la_ref, v_ref, gg_ref, p_ref, lvl_ref, gn_ref, o_ref, st_ref,
                *, n_macro, group, unroll):
    st_ref[...] = jnp.zeros_like(st_ref)
    lo_half = lax.broadcasted_iota(jnp.int32, (MACRO, LANES), 1) < (LANES // 2)

    lvl = lvl_ref[...]
    n_pair = GLA_HEADS // 2

    def body(i, carry):
        streams = []
        for mm in range(group):
            rows = pl.ds(pl.multiple_of((i * group + mm) * MACRO, MACRO), MACRO)
            for pair in range(n_pair):
                qk = pl.ds(LANES * pair, LANES)
                vg = pl.ds(2 * GLA_DV * pair, 2 * GLA_DV)
                streams.append(_gla_macro(
                    rows, q_ref.at[:, qk], k_ref.at[:, qk], la_ref.at[:, qk], v_ref.at[:, vg],
                    gg_ref.at[:, vg], p_ref, lvl, gn_ref, o_ref.at[:, vg], st_ref.at[pair], lo_half))
        _interleave(*streams)
        return carry

    lax.fori_loop(0, n_macro // group, body, 0, unroll=unroll)


def _gla(gq, gk, la, gv, gg, gn, *, batch, seq, group=2, unroll=4):
    p_np, lvl_np = _gla_constants()
    pm = jnp.asarray(p_np, BF16)
    lvl = jnp.asarray(lvl_np)
    seq_spec = lambda width: pl.BlockSpec((None, seq, width), lambda b: (b, 0, 0))
    const = lambda shape: pl.BlockSpec(shape, lambda b: (0,) * len(shape))
    r3 = lambda a: a.reshape(batch, seq, a.shape[-1])
    return pl.pallas_call(
        functools.partial(_gla_kernel, n_macro=seq // MACRO, group=group, unroll=unroll),
        grid=(batch,),
        in_specs=[seq_spec(GLA_QK), seq_spec(GLA_QK), seq_spec(GLA_QK), seq_spec(GLA_W),
                  seq_spec(GLA_W), const(pm.shape), const(lvl.shape), const(gn.shape)],
        out_specs=seq_spec(GLA_W),
        out_shape=jax.ShapeDtypeStruct((batch, seq, GLA_W), BF16),
        scratch_shapes=[pltpu.VMEM((GLA_HEADS // 2, GLA_DV, LANES), F32)],
        compiler_params=pltpu.CompilerParams(
            dimension_semantics=("arbitrary",), vmem_limit_bytes=VMEM_LIMIT),
        name="gla",
    )(r3(gq), r3(gk), r3(la), r3(gv), r3(gg), pm, lvl, gn)


def _t5_bucket_np(dist):
    max_exact = REL_BUCKETS // 2
    d = np.maximum(dist, 1).astype(np.float32)
    large = max_exact + (np.log(d / np.float32(max_exact)) / np.float32(math.log(REL_MAX_DIST / max_exact))
                         * np.float32(REL_BUCKETS - max_exact)).astype(np.int32)
    large = np.minimum(large, REL_BUCKETS - 1)
    return np.where(dist < max_exact, dist, large).astype(np.int32)


def _bucket_table():
    qi = np.arange(BLK)[:, None]
    ki = np.arange(2 * BLK)[None, :]
    j = qi + BLK - ki
    valid = (j >= 0) & (j <= BAND)
    return np.stack([np.where(valid, _t5_bucket_np(np.maximum(j, 0) * d), -1) for d in DILATIONS]).astype(np.int32)


def _bias_kernel(rel_ref, bkt_ref, o_ref, *, buckets):
    for p, used in enumerate(buckets):
        bkt = bkt_ref[p]
        accs = [jnp.full(bkt.shape, NEG, F32) for _ in range(DIL_HEADS)]
        for u in used:
            hit = bkt == u
            accs = [jnp.where(hit, rel_ref[u, h] * LOG2E, a) for h, a in enumerate(accs)]
        for h, a in enumerate(accs):
            o_ref[p, h // 2, (h % 2) * BLK:(h % 2 + 1) * BLK, :] = a


def _bias_table(rel_bias):
    bkt_np = _bucket_table()
    buckets = tuple(tuple(int(u) for u in np.unique(b) if u >= 0) for b in bkt_np)
    n_pat = len(DILATIONS)
    return pl.pallas_call(
        functools.partial(_bias_kernel, buckets=buckets),
        in_specs=[pl.BlockSpec(memory_space=pltpu.SMEM), pl.BlockSpec(memory_space=pltpu.VMEM)],
        out_specs=pl.BlockSpec(memory_space=pltpu.VMEM),
        out_shape=jax.ShapeDtypeStruct((n_pat, DIL_HEADS // 2, 2 * BLK, 2 * BLK), F32),
        name="bias_table",
    )(rel_bias, jnp.asarray(bkt_np))


def _attn_kernel(q1, k1, v1, q4, k4, v4, q16, k16, v16, bias_ref, o_ref,
                 s_ref, mb_ref, u_ref, w_ref, m_ref, *, seq):
    lo_half = lax.broadcasted_iota(jnp.int32, (BLK, LANES), 1) < DIL_HD
    refs = ((q1, k1, v1), (q4, k4, v4), (q16, k16, v16))
    nblk = seq // BLK

    def rows_of(ref, p, g, with_prev=False):
        r, n = divmod(g, nblk // DILATIONS[p])
        return ref[r, pl.ds((n - 1) * BLK, 2 * BLK) if with_prev else pl.ds(n * BLK, BLK), :]

    one_trip = jnp.minimum(pl.program_id(0) + 1, 1)

    def region(fn):
        lax.fori_loop(0, one_trip, lambda i, c: (fn(), c)[1], 0)

    def natural_rows(p, g):
        d = DILATIONS[p]
        r, n = divmod(g, nblk // d)
        return pl.ds(n * (BLK * d) + r, BLK, stride=d) if d > 1 else pl.ds(g * BLK, BLK)

    def scores(p):
        q_ref, k_ref, _ = refs[p]
        nb = nblk // DILATIONS[p]
        for g in range(nblk):
            has_prev = g % nb > 0
            nk = 2 * BLK if has_prev else BLK
            bias = bias_ref[p] if has_prev else bias_ref[p, :, BLK:]
            q = rows_of(q_ref, p, g)
            s = _dot_nt(_split_heads(q, lo_half), rows_of(k_ref, p, g, has_prev)) + bias
            s_ref[p * nblk + g, :, :nk] = s
            mb = jnp.broadcast_to(jnp.max(s, axis=-1, keepdims=True), (2 * BLK, LANES))
            mb_ref[p * nblk + g] = mb
            m_ref[p, natural_rows(p, g), :] = jnp.where(lo_half, mb[:BLK], mb[BLK:])
            yield

    def outputs(p):
        _, _, v_ref = refs[p]
        nb = nblk // DILATIONS[p]
        for g in range(nblk):
            has_prev = g % nb > 0
            nk = 2 * BLK if has_prev else BLK
            idx = p * nblk + g
            e = jnp.concatenate(
                [jnp.exp2(s_ref[idx, pl.ds(h * BLK, BLK), c * LANES:(c + 1) * LANES]
                          - mb_ref[idx, pl.ds(h * BLK, BLK), :])
                 for h in range(2) for c in range(nk // LANES)], axis=1).astype(BF16)
            vv = rows_of(v_ref, p, g, has_prev)
            lo_k = lax.broadcasted_iota(jnp.int32, vv.shape, 1) < DIL_HD
            zero = jnp.zeros_like(vv)
            lane = lax.broadcasted_iota(jnp.int32, vv.shape, 1)
            sum_a = jnp.where(lane < DIL_HD, 1.0, 0.0).astype(BF16)
            sum_b = jnp.where(lane < DIL_HD, 0.0, 1.0).astype(BF16)
            rhs = jnp.concatenate(
                [jnp.concatenate([jnp.where(lo_k, vv, zero), sum_a], axis=1),
                 jnp.concatenate([jnp.where(lo_k, zero, vv), sum_b], axis=1)], axis=0)
            uw = jnp.dot(e, rhs, preferred_element_type=F32)
            dst = natural_rows(p, g)
            u_ref[p, dst, :] = uw[:, :LANES]
            w_ref[p, dst, :] = uw[:, LANES:]
            yield

    order = tuple(range(len(DILATIONS)))
    region(lambda: _interleave(scores(order[0])))
    for cur, nxt in zip(order, order[1:] + (None,)):
        def both(cur=cur, nxt=nxt):
            _interleave(outputs(cur), *([scores(nxt)] if nxt is not None else []))
        region(both)

    chunk = 2 * BLK

    def combine(i, carry):
        rows = pl.ds(pl.multiple_of(i * chunk, chunk), chunk)
        ms = [m_ref[p, rows, :] for p in range(len(DILATIONS))]
        mmax = functools.reduce(jnp.maximum, ms)
        cs = [jnp.exp2(mp - mmax) for mp in ms]
        num = sum(c * u_ref[p, rows, :] for p, c in enumerate(cs))
        den = sum(c * w_ref[p, rows, :] for p, c in enumerate(cs))
        o_ref[rows, :] = (num / den).astype(o_ref.dtype)
        return carry

    lax.fori_loop(0, seq // chunk, combine, 0, unroll=2)


def _dil_attn(a1, a4, a16, bias, *, batch, seq):
    n_pat = len(DILATIONS)
    n_pair = DIL_HEADS // 2
    arrs = (a1, a4, a16)

    def spec(d, part):
        return pl.BlockSpec((None, d, None, seq // d, LANES),
                            lambda h, b: (b, 0, part * n_pair + h, 0, 0))

    in_specs = [spec(d, part) for d in DILATIONS for part in range(3)]
    in_specs.append(pl.BlockSpec((n_pat, None, 2 * BLK, 2 * BLK), lambda h, b: (0, h, 0, 0)))
    args = [a for a in arrs for _ in range(3)] + [bias]
    return pl.pallas_call(
        functools.partial(_attn_kernel, seq=seq),
        grid=(n_pair, batch),
        in_specs=in_specs,
        out_specs=pl.BlockSpec((None, seq, LANES), lambda h, b: (b, 0, h)),
        out_shape=jax.ShapeDtypeStruct((batch, seq, DIL_W), BF16),
        scratch_shapes=[pltpu.VMEM((n_pat * seq // BLK, 2 * BLK, 2 * BLK), F32),
                        pltpu.VMEM((n_pat * seq // BLK, 2 * BLK, LANES), F32)]
        + [pltpu.VMEM((n_pat, seq, LANES), F32) for _ in range(3)],
        compiler_params=pltpu.CompilerParams(
            dimension_semantics=("arbitrary", "arbitrary"), vmem_limit_bytes=VMEM_LIMIT),
        name="dil_attn",
    )(*args)


def _post_kernel(h_ref, og_ref, od_ref, p_ref, wo_ref, w1_ref, w2_ref, wg_ref, wp_ref,
                 g_mix_ref, g_pre_ref, g_post_ref, o_ref, *, ff_chunk):
    mix = (jnp.dot(og_ref[...], wo_ref[:GLA_W, :], preferred_element_type=F32)
           + jnp.dot(od_ref[...], wo_ref[GLA_W:, :], preferred_element_type=F32))
    emb = jnp.dot(p_ref[...].astype(BF16), wp_ref[...], preferred_element_type=F32)
    h1 = h_ref[...] + _rms(mix, g_mix_ref[...])
    xn = _rms(h1, g_pre_ref[...]).astype(BF16)
    f = jnp.zeros_like(h1)
    for c in range(w1_ref.shape[1] // ff_chunk):
        cols = slice(c * ff_chunk, (c + 1) * ff_chunk)
        a = jnp.maximum(jnp.dot(xn, w1_ref[:, cols], preferred_element_type=F32), 0.0)
        f = f + jnp.dot((a * a).astype(BF16), w2_ref[cols, :], preferred_element_type=F32)
    h2 = h1 + _rms(f, g_post_ref[...])
    gate = jnp.dot(h2.astype(BF16), wg_ref[...], preferred_element_type=F32)
    o_ref[...] = h2 + emb / (1.0 + jnp.exp(-gate))


def _post(h, og, od, p, wo, w1, w2, wg, wp, g_mix, g_pre, g_post, *, layer, tm, ff_chunk=1024):
    t, dm = h.shape
    row = lambda width: pl.BlockSpec((tm, width), lambda i: (i, 0))
    const = lambda a: pl.BlockSpec((None,) + a.shape[1:], lambda i: (layer, 0, 0),
                                   pipeline_mode=pl.Buffered(1))
    consts = (wo, w1, w2, wg, wp, g_mix, g_pre, g_post)
    return pl.pallas_call(
        functools.partial(_post_kernel, ff_chunk=ff_chunk),
        grid=(t // tm,),
        in_specs=[row(dm), row(GLA_W), row(DIL_W),
                  pl.BlockSpec((None, tm, p.shape[2]), lambda i: (layer, i, 0))]
        + [const(a) for a in consts],
        out_specs=row(dm),
        out_shape=jax.ShapeDtypeStruct((t, dm), F32),
        compiler_params=pltpu.CompilerParams(
            dimension_semantics=("arbitrary",), vmem_limit_bytes=VMEM_LIMIT),
        name="post",
    )(h, og, od, p, *consts)


def kernel(x, p, w_in, w_gla_a2, b_gla_a, gla_norm_g, w_out, rel_bias, pre_mix_g, post_mix_g,
           pre_mlp_g, post_mlp_g, w_mlp_in, w_mlp_out, w_ple_gate, w_ple_proj):
    batch, seq, dm = x.shape
    depth = w_in.shape[0]
    assert seq % (max(DILATIONS) * BLK) == 0 and seq % MACRO == 0
    t = batch * seq
    rows1 = lambda a: a.reshape(depth, 1, -1)
    bias = _bias_table(rel_bias)
    h = x.reshape(t, dm)
    w = _prep_w_in(w_in)
    wa2 = jnp.pad(w_gla_a2, ((0, 0), (0, C_END - C_LR - GLA_LOWRANK), (0, 0))).astype(BF16)
    wo, w1, w2, wg, wp = (a.astype(BF16) for a in (w_out, w_mlp_in, w_mlp_out, w_ple_gate, w_ple_proj))
    p2 = p.reshape(depth, t, -1)
    for i in range(depth):
        gq, gk, la, gv, gg, a1, a4, a16 = _in_proj(
            h, rows1(pre_mix_g), w, wa2, rows1(b_gla_a), layer=i, batch=batch, seq=seq, tm=IN_PROJ_ROWS)
        og = _gla(gq, gk, la, gv, gg, gla_norm_g[i].reshape(1, -1), batch=batch, seq=seq)
        od = _dil_attn(a1, a4, a16, bias, batch=batch, seq=seq)
        h = _post(h, og.reshape(t, GLA_W), od.reshape(t, DIL_W), p2, wo, w1, w2, wg, wp,
                  rows1(post_mix_g), rows1(pre_mlp_g), rows1(post_mlp_g), layer=i, tm=POST_ROWS)
    return h.reshape(batch, seq, dm)
```

```python
import functools
import math

import numpy as np
import jax
import jax.numpy as jnp
from jax import lax
from jax.experimental import pallas as pl
from jax.experimental.pallas import tpu as pltpu

F32 = jnp.float32
BF16 = jnp.bfloat16

EPS = 1e-6
GLA_HEADS = 4
GLA_DK = 64
GLA_DV = 128
GLA_QK = GLA_HEADS * GLA_DK
GLA_W = GLA_HEADS * GLA_DV
GLA_LOWRANK = 16
GLA_TAU = 16.0
DIL_HEADS = 8
DIL_HD = 64
DIL_W = DIL_HEADS * DIL_HD
DILATIONS = (1, 4, 16)
BAND = 128
BLK = 128
REL_BUCKETS = 32
REL_MAX_DIST = 2048
NEG = -1e30
LOG2E = math.log2(math.e)

LANES = 128
SUBLANES = 8
MACRO = 128
GLA_LEVELS = 7
VMEM_LIMIT = 56 * 1024 * 1024
IN_PROJ_ROWS = 512
POST_ROWS = 512
CAST_ROWS = 256

C_GQ, C_GK, C_GV, C_GG, C_DQ, C_LR, C_END = 0, 256, 512, 1024, 1536, 3072, 3200


def _rms(x, g):
    return x * lax.rsqrt(jnp.mean(x * x, axis=-1, keepdims=True) + EPS) * g


def _w_in_kernel(wt_ref, o_ref, *, lr0):
    lr1 = lr0 + GLA_LOWRANK
    tk = wt_ref.shape[1]
    o_ref[:, :lr0] = wt_ref[:lr0, :].T.astype(BF16)
    o_ref[:, lr0:C_LR] = wt_ref[lr1:, :].T.astype(BF16)
    o_ref[:, C_LR:] = jnp.concatenate(
        [wt_ref[lr0:lr1, :].T, jnp.zeros((tk, C_END - C_LR - GLA_LOWRANK), F32)], axis=1).astype(BF16)


def _prep_w_in(w_in, *, tk=512):
    depth, dm, width = w_in.shape
    lr0 = C_DQ
    assert width - GLA_LOWRANK == C_LR
    return pl.pallas_call(
        functools.partial(_w_in_kernel, lr0=lr0),
        grid=(depth, dm // tk),
        in_specs=[pl.BlockSpec((None, width, tk), lambda l, i: (l, 0, i))],
        out_specs=pl.BlockSpec((None, tk, C_END), lambda l, i: (l, i, 0)),
        out_shape=jax.ShapeDtypeStruct((depth, dm, C_END), BF16),
        compiler_params=pltpu.CompilerParams(
            dimension_semantics=("arbitrary", "arbitrary"), vmem_limit_bytes=VMEM_LIMIT),
        name="prep_w_in",
    )(jnp.swapaxes(w_in, 1, 2))


def _in_proj_kernel(h_ref, g_ref, w_ref, wa2_ref, ba_ref,
                    gq_ref, gk_ref, la_ref, gv_ref, gg_ref, a1_ref, a4_ref, a16_ref,
                    slab_ref, slab4_ref, *, tm):
    xn = _rms(h_ref[...], g_ref[...]).astype(BF16)

    def proj(lo, hi):
        return jnp.dot(xn, w_ref[:, lo:hi], preferred_element_type=F32)

    z = jnp.dot(proj(C_LR, C_END).astype(BF16), wa2_ref[...],
                preferred_element_type=F32) + ba_ref[...]
    la_ref[...] = (jnp.minimum(z, 0.0) - jnp.log1p(jnp.exp(-jnp.abs(z)))) * (1.0 / GLA_TAU)

    n_grp = DIL_W // LANES

    def project_dilated(c):
        y = proj(C_DQ + DIL_W * c, C_DQ + DIL_W * (c + 1))
        if c == 0:
            y = y * (LOG2E * DIL_HD ** -0.5)
        for s in range(n_grp):
            ys = y[:, LANES * s:LANES * (s + 1)]
            idx = c * n_grp + s
            a1_ref[0, idx] = ys.astype(BF16)
            slab_ref[idx] = ys

    def deinterleave(c):
        for idx in range(c * n_grp, (c + 1) * n_grp):
            for r in range(4):
                y4 = slab_ref[idx, pl.ds(r, tm // 4, stride=4), :]
                a4_ref[r, idx] = y4.astype(BF16)
                slab4_ref[idx, r] = y4
            for r in range(16):
                a16_ref[r, idx] = slab4_ref[idx, r % 4, pl.ds(r // 4, tm // 16, stride=4), :].astype(BF16)

    project_dilated(0)
    project_dilated(1)
    deinterleave(0)
    project_dilated(2)
    deinterleave(1)
    gq_ref[...] = (proj(C_GQ, C_GK) * (GLA_DK ** -0.5)).astype(BF16)
    gk_ref[...] = proj(C_GK, C_GV).astype(BF16)
    deinterleave(2)
    gv_ref[...] = proj(C_GV, C_GG).astype(BF16)
    gg_ref[...] = proj(C_GG, C_DQ).astype(BF16)


def _in_proj(h, g, w, wa2, ba, *, layer, batch, seq, tm):
    t = batch * seq
    nt = seq // tm
    row = lambda width: pl.BlockSpec((tm, width), lambda b, i: (b * nt + i, 0))
    const = lambda shape: pl.BlockSpec((None,) + shape[1:], lambda b, i: (layer, 0, 0))
    n_slab = 3 * DIL_W // LANES
    dil_spec = lambda d: pl.BlockSpec((None, d, n_slab, tm // d, LANES), lambda b, i: (b, 0, 0, i, 0))
    out_shape = (
        jax.ShapeDtypeStruct((t, GLA_QK), BF16), jax.ShapeDtypeStruct((t, GLA_QK), BF16),
        jax.ShapeDtypeStruct((t, GLA_QK), F32),
        jax.ShapeDtypeStruct((t, GLA_W), BF16), jax.ShapeDtypeStruct((t, GLA_W), BF16),
    ) + tuple(jax.ShapeDtypeStruct((batch, d, n_slab, seq // d, LANES), BF16) for d in DILATIONS)
    return pl.pallas_call(
        functools.partial(_in_proj_kernel, tm=tm),
        grid=(batch, nt),
        in_specs=[row(h.shape[1]), const(g.shape), const(w.shape), const(wa2.shape), const(ba.shape)],
        out_specs=(row(GLA_QK), row(GLA_QK), row(GLA_QK), row(GLA_W), row(GLA_W))
        + tuple(dil_spec(d) for d in DILATIONS),
        out_shape=out_shape,
        scratch_shapes=[pltpu.VMEM((n_slab, tm, LANES), F32),
                        pltpu.VMEM((n_slab, 4, tm // 4, LANES), F32)],
        compiler_params=pltpu.CompilerParams(
            dimension_semantics=("arbitrary", "arbitrary"), vmem_limit_bytes=VMEM_LIMIT),
        name="in_proj",
    )(h, g, w, wa2, ba)


def _gla_constants():
    n = MACRO
    i = np.arange(n)[:, None]
    t = np.arange(n)[None, :]
    p = np.concatenate([t <= i, t > i], axis=0).astype(np.float32)
    j = np.arange(n)[None, :]
    lvl = np.where(j < i, np.floor(np.log2(np.maximum(i ^ j, 1))).astype(np.int32),
                   np.where(j == i, GLA_LEVELS, -1)).astype(np.int32)
    return p, np.concatenate([lvl, lvl], axis=0)


def _split_heads(x, lo_half):
    zero = jnp.zeros_like(x)
    return jnp.concatenate([jnp.where(lo_half, x, zero), jnp.where(lo_half, zero, x)], axis=0)


def _dot_nt(a, b):
    return lax.dot_general(a, b, (((1,), (1,)), ((), ())), preferred_element_type=F32)


def _dot_tn(a, b):
    return lax.dot_general(a, b, (((0,), (0,)), ((), ())), preferred_element_type=F32)


def _interleave(*gens):
    live = [iter(g) for g in gens]
    while live:
        for g in list(live):
            try:
                next(g)
            except StopIteration:
                live.remove(g)


def _gla_macro(rows, q_ref, k_ref, la_ref, v_ref, gg_ref, p_ref, lvl, gn_ref, o_ref, st_ref, lo_half):
    q = q_ref[rows, :].astype(F32)
    k = k_ref[rows, :].astype(F32)
    la = la_ref[rows, :]
    la_hi = la.astype(BF16)
    la2 = jnp.concatenate([la_hi, (la - la_hi.astype(F32)).astype(BF16)], axis=1)

    def range_sum(blk):
        r = jnp.dot(p_ref[blk * MACRO:(blk + 1) * MACRO, :], la2, preferred_element_type=F32)
        return r[:, :LANES] + r[:, LANES:]

    b = range_sum(0)
    x_start = jnp.exp(b)
    qg = q * x_start
    kg = k * jnp.exp(range_sum(1))
    a_last = x_start[MACRO - 1:MACRO, :]

    @functools.cache
    def row_bcast(r):
        return jnp.broadcast_to(b[r:r + 1, :], (SUBLANES, LANES))

    sub = lax.broadcasted_iota(jnp.int32, (SUBLANES, LANES), 0)

    def ref_rows(l):
        span = 2 << l
        pieces = []
        for g in range(MACRO // SUBLANES):
            group_rows = range(SUBLANES * g, SUBLANES * (g + 1))
            refs = sorted({(i // span) * span + span // 2 - 1 for i in group_rows})
            piece = row_bcast(refs[-1])
            for c in reversed(range(len(refs) - 1)):
                piece = jnp.where(sub < (c + 1) * span, row_bcast(refs[c]), piece)
            pieces.append(piece)
        return jnp.concatenate(pieces, axis=0)

    att = _dot_nt(_split_heads(q, lo_half).astype(BF16), k.astype(BF16))
    att = jnp.where(lvl == GLA_LEVELS, att, 0.0)
    yield
    for l in range(GLA_LEVELS):
        xl = jnp.exp(-jnp.abs(b - ref_rows(l)))
        a = _dot_nt(_split_heads(q * xl, lo_half).astype(BF16), (k * xl).astype(BF16))
        att = jnp.where(lvl == l, a, att)
        yield
    att = att.astype(BF16)
    st = st_ref[...]
    inter = _dot_nt(_split_heads(qg, lo_half).astype(BF16), st.astype(BF16))
    kgp = kg.astype(BF16)
    upd = []
    for e in range(2):
        hs = slice(GLA_DV * e, GLA_DV * (e + 1))
        vh = v_ref[rows, hs]
        o = (jnp.dot(att[e * MACRO:(e + 1) * MACRO], vh, preferred_element_type=F32)
             + inter[e * MACRO:(e + 1) * MACRO])
        gate = gg_ref[rows, hs].astype(F32)
        o = _rms(o, gn_ref[...]) * (gate / (1.0 + jnp.exp(-gate)))
        o_ref[rows, hs] = o.astype(o_ref.dtype)
        upd.append(_dot_tn(vh, kgp))
    st_ref[...] = st * a_last + jnp.where(lo_half, upd[0], upd[1])
    yield


def _gla_kernel(q_ref, k_ref, la_ref, v_ref, gg_ref, p_ref, lvl_ref, gn_ref, o_ref, st_ref,
                *, n_macro, group, unroll):
    st_ref[...] = jnp.zeros_like(st_ref)
    lo_half = lax.broadcasted_iota(jnp.int32, (MACRO, LANES), 1) < (LANES // 2)

    lvl = lvl_ref[...]
    n_pair = GLA_HEADS // 2

    def body(i, carry):
        streams = []
        for mm in range(group):
            rows = pl.ds(pl.multiple_of((i * group + mm) * MACRO, MACRO), MACRO)
            for pair in range(n_pair):
                qk = pl.ds(LANES * pair, LANES)
                vg = pl.ds(2 * GLA_DV * pair, 2 * GLA_DV)
                streams.append(_gla_macro(
                    rows, q_ref.at[:, qk], k_ref.at[:, qk], la_ref.at[:, qk], v_ref.at[:, vg],
                    gg_ref.at[:, vg], p_ref, lvl, gn_ref, o_ref.at[:, vg], st_ref.at[pair], lo_half))
        _interleave(*streams)
        return carry

    lax.fori_loop(0, n_macro // group, body, 0, unroll=unroll)


def _gla(gq, gk, la, gv, gg, gn, *, batch, seq, group=2, unroll=4):
    p_np, lvl_np = _gla_constants()
    pm = jnp.asarray(p_np, BF16)
    lvl = jnp.asarray(lvl_np)
    seq_spec = lambda width: pl.BlockSpec((None, seq, width), lambda b: (b, 0, 0))
    const = lambda shape: pl.BlockSpec(shape, lambda b: (0,) * len(shape))
    r3 = lambda a: a.reshape(batch, seq, a.shape[-1])
    return pl.pallas_call(
        functools.partial(_gla_kernel, n_macro=seq // MACRO, group=group, unroll=unroll),
        grid=(batch,),
        in_specs=[seq_spec(GLA_QK), seq_spec(GLA_QK), seq_spec(GLA_QK), seq_spec(GLA_W),
                  seq_spec(GLA_W), const(pm.shape), const(lvl.shape), const(gn.shape)],
        out_specs=seq_spec(GLA_W),
        out_shape=jax.ShapeDtypeStruct((batch, seq, GLA_W), BF16),
        scratch_shapes=[pltpu.VMEM((GLA_HEADS // 2, GLA_DV, LANES), F32)],
        compiler_params=pltpu.CompilerParams(
            dimension_semantics=("arbitrary",), vmem_limit_bytes=VMEM_LIMIT),
        name="gla",
    )(r3(gq), r3(gk), r3(la), r3(gv), r3(gg), pm, lvl, gn)


def _t5_bucket_np(dist):
    max_exact = REL_BUCKETS // 2
    d = np.maximum(dist, 1).astype(np.float32)
    large = max_exact + (np.log(d / np.float32(max_exact)) / np.float32(math.log(REL_MAX_DIST / max_exact))
                         * np.float32(REL_BUCKETS - max_exact)).astype(np.int32)
    large = np.minimum(large, REL_BUCKETS - 1)
    return np.where(dist < max_exact, dist, large).astype(np.int32)


def _bucket_table():
    qi = np.arange(BLK)[:, None]
    ki = np.arange(2 * BLK)[None, :]
    j = qi + BLK - ki
    valid = (j >= 0) & (j <= BAND)
    return np.stack([np.where(valid, _t5_bucket_np(np.maximum(j, 0) * d), -1) for d in DILATIONS]).astype(np.int32)


def _bias_kernel(rel_ref, bkt_ref, o_ref, *, buckets):
    for p, used in enumerate(buckets):
        bkt = bkt_ref[p]
        accs = [jnp.full(bkt.shape, NEG, F32) for _ in range(DIL_HEADS)]
        for u in used:
            hit = bkt == u
            accs = [jnp.where(hit, rel_ref[u, h] * LOG2E, a) for h, a in enumerate(accs)]
        for h, a in enumerate(accs):
            o_ref[p, h // 2, (h % 2) * BLK:(h % 2 + 1) * BLK, :] = a


def _bias_table(rel_bias):
    bkt_np = _bucket_table()
    buckets = tuple(tuple(int(u) for u in np.unique(b) if u >= 0) for b in bkt_np)
    n_pat = len(DILATIONS)
    return pl.pallas_call(
        functools.partial(_bias_kernel, buckets=buckets),
        in_specs=[pl.BlockSpec(memory_space=pltpu.SMEM), pl.BlockSpec(memory_space=pltpu.VMEM)],
        out_specs=pl.BlockSpec(memory_space=pltpu.VMEM),
        out_shape=jax.ShapeDtypeStruct((n_pat, DIL_HEADS // 2, 2 * BLK, 2 * BLK), F32),
        name="bias_table",
    )(rel_bias, jnp.asarray(bkt_np))


def _attn_kernel(q1, k1, v1, q4, k4, v4, q16, k16, v16, bias_ref, o_ref,
                 s_ref, mb_ref, u_ref, w_ref, m_ref, *, seq):
    lo_half = lax.broadcasted_iota(jnp.int32, (BLK, LANES), 1) < DIL_HD
    refs = ((q1, k1, v1), (q4, k4, v4), (q16, k16, v16))
    nblk = seq // BLK

    def rows_of(ref, p, g, with_prev=False):
        r, n = divmod(g, nblk // DILATIONS[p])
        return ref[r, pl.ds((n - 1) * BLK, 2 * BLK) if with_prev else pl.ds(n * BLK, BLK), :]

    one_trip = jnp.minimum(pl.program_id(0) + 1, 1)

    def region(fn):
        lax.fori_loop(0, one_trip, lambda i, c: (fn(), c)[1], 0)

    def natural_rows(p, g):
        d = DILATIONS[p]
        r, n = divmod(g, nblk // d)
        return pl.ds(n * (BLK * d) + r, BLK, stride=d) if d > 1 else pl.ds(g * BLK, BLK)

    def scores(p):
        q_ref, k_ref, _ = refs[p]
        nb = nblk // DILATIONS[p]
        for g in range(nblk):
            has_prev = g % nb > 0
            nk = 2 * BLK if has_prev else BLK
            bias = bias_ref[p] if has_prev else bias_ref[p, :, BLK:]
            q = rows_of(q_ref, p, g)
            s = _dot_nt(_split_heads(q, lo_half), rows_of(k_ref, p, g, has_prev)) + bias
            s_ref[p * nblk + g, :, :nk] = s
            mb = jnp.broadcast_to(jnp.max(s, axis=-1, keepdims=True), (2 * BLK, LANES))
            mb_ref[p * nblk + g] = mb
            m_ref[p, natural_rows(p, g), :] = jnp.where(lo_half, mb[:BLK], mb[BLK:])
            yield

    def outputs(p):
        _, _, v_ref = refs[p]
        nb = nblk // DILATIONS[p]
        for g in range(nblk):
            has_prev = g % nb > 0
            nk = 2 * BLK if has_prev else BLK
            idx = p * nblk + g
            e = jnp.concatenate(
                [jnp.exp2(s_ref[idx, pl.ds(h * BLK, BLK), c * LANES:(c + 1) * LANES]
                          - mb_ref[idx, pl.ds(h * BLK, BLK), :])
                 for h in range(2) for c in range(nk // LANES)], axis=1).astype(BF16)
            vv = rows_of(v_ref, p, g, has_prev)
            lo_k = lax.broadcasted_iota(jnp.int32, vv.shape, 1) < DIL_HD
            zero = jnp.zeros_like(vv)
            lane = lax.broadcasted_iota(jnp.int32, vv.shape, 1)
            sum_a = jnp.where(lane < DIL_HD, 1.0, 0.0).astype(BF16)
            sum_b = jnp.where(lane < DIL_HD, 0.0, 1.0).astype(BF16)
            rhs = jnp.concatenate(
                [jnp.concatenate([jnp.where(lo_k, vv, zero), sum_a], axis=1),
                 jnp.concatenate([jnp.where(lo_k, zero, vv), sum_b], axis=1)], axis=0)
            uw = jnp.dot(e, rhs, preferred_element_type=F32)
            dst = natural_rows(p, g)
            u_ref[p, dst, :] = uw[:, :LANES]
            w_ref[p, dst, :] = uw[:, LANES:]
            yield

    order = tuple(range(len(DILATIONS)))
    region(lambda: _interleave(scores(order[0])))
    for cur, nxt in zip(order, order[1:] + (None,)):
        def both(cur=cur, nxt=nxt):
            _interleave(outputs(cur), *([scores(nxt)] if nxt is not None else []))
        region(both)

    chunk = 2 * BLK

    def combine(i, carry):
        rows = pl.ds(pl.multiple_of(i * chunk, chunk), chunk)
        ms = [m_ref[p, rows, :] for p in range(len(DILATIONS))]
        mmax = functools.reduce(jnp.maximum, ms)
        cs = [jnp.exp2(mp - mmax) for mp in ms]
        num = sum(c * u_ref[p, rows, :] for p, c in enumerate(cs))
        den = sum(c * w_ref[p, rows, :] for p, c in enumerate(cs))
        o_ref[rows, :] = (num / den).astype(o_ref.dtype)
        return carry

    lax.fori_loop(0, seq // chunk, combine, 0, unroll=2)


def _dil_attn(a1, a4, a16, bias, *, batch, seq):
    n_pat = len(DILATIONS)
    n_pair = DIL_HEADS // 2
    arrs = (a1, a4, a16)

    def spec(d, part):
        return pl.BlockSpec((None, d, None, seq // d, LANES),
                            lambda h, b: (b, 0, part * n_pair + h, 0, 0))

    in_specs = [spec(d, part) for d in DILATIONS for part in range(3)]
    in_specs.append(pl.BlockSpec((n_pat, None, 2 * BLK, 2 * BLK), lambda h, b: (0, h, 0, 0)))
    args = [a for a in arrs for _ in range(3)] + [bias]
    return pl.pallas_call(
        functools.partial(_attn_kernel, seq=seq),
        grid=(n_pair, batch),
        in_specs=in_specs,
        out_specs=pl.BlockSpec((None, seq, LANES), lambda h, b: (b, 0, h)),
        out_shape=jax.ShapeDtypeStruct((batch, seq, DIL_W), BF16),
        scratch_shapes=[pltpu.VMEM((n_pat * seq // BLK, 2 * BLK, 2 * BLK), F32),
                        pltpu.VMEM((n_pat * seq // BLK, 2 * BLK, LANES), F32)]
        + [pltpu.VMEM((n_pat, seq, LANES), F32) for _ in range(3)],
        compiler_params=pltpu.CompilerParams(
            dimension_semantics=("arbitrary", "arbitrary"), vmem_limit_bytes=VMEM_LIMIT),
        name="dil_attn",
    )(*args)


def _stage_weights(pairs, stage_ref, sem_ref, layer):
    chunks = [(src, dst, r0) for src, dst in pairs for r0 in range(0, dst.shape[0], CAST_ROWS)]

    def copy(i):
        src, dst, r0 = chunks[i]
        rows, n = min(CAST_ROWS, dst.shape[0]), dst.shape[1]
        return pltpu.make_async_copy(src.at[layer, pl.ds(r0, rows), :],
                                     stage_ref.at[i % 2, pl.ds(0, rows), pl.ds(0, n)], sem_ref.at[i % 2])

    copy(0).start()
    for i, (_, dst, r0) in enumerate(chunks):
        if i + 1 < len(chunks):
            copy(i + 1).start()
        copy(i).wait()
        rows, n = min(CAST_ROWS, dst.shape[0]), dst.shape[1]
        dst[pl.ds(r0, rows), :] = stage_ref[i % 2, :rows, :n].astype(BF16)


def _post_kernel(h_ref, og_ref, od_ref, p_ref, wo_hbm, w1_hbm, w2_hbm, wg_hbm, wp_hbm,
                 g_mix_ref, g_pre_ref, g_post_ref, o_ref,
                 wo_ref, w1_ref, w2_ref, wg_ref, wp_ref, stage_ref, sem_ref, *, ff_chunk, layer):
    @pl.when(pl.program_id(0) == 0)
    def _():
        _stage_weights(((wo_hbm, wo_ref), (w1_hbm, w1_ref), (w2_hbm, w2_ref), (wg_hbm, wg_ref),
                        (wp_hbm, wp_ref)), stage_ref, sem_ref, layer)

    mix = (jnp.dot(og_ref[...], wo_ref[:GLA_W, :], preferred_element_type=F32)
           + jnp.dot(od_ref[...], wo_ref[GLA_W:, :], preferred_element_type=F32))
    emb = jnp.dot(p_ref[...].astype(BF16), wp_ref[...], preferred_element_type=F32)
    h1 = h_ref[...] + _rms(mix, g_mix_ref[...])
    xn = _rms(h1, g_pre_ref[...]).astype(BF16)
    f = jnp.zeros_like(h1)
    for c in range(w1_ref.shape[1] // ff_chunk):
        cols = slice(c * ff_chunk, (c + 1) * ff_chunk)
        a = jnp.maximum(jnp.dot(xn, w1_ref[:, cols], preferred_element_type=F32), 0.0)
        f = f + jnp.dot((a * a).astype(BF16), w2_ref[cols, :], preferred_element_type=F32)
    h2 = h1 + _rms(f, g_post_ref[...])
    gate = jnp.dot(h2.astype(BF16), wg_ref[...], preferred_element_type=F32)
    o_ref[...] = h2 + emb / (1.0 + jnp.exp(-gate))


def _post(h, og, od, p, wo, w1, w2, wg, wp, g_mix, g_pre, g_post, *, layer, tm, ff_chunk=1024):
    t, dm = h.shape
    row = lambda width: pl.BlockSpec((tm, width), lambda i: (i, 0))
    const = lambda a: pl.BlockSpec((None,) + a.shape[1:], lambda i: (layer, 0, 0),
                                   pipeline_mode=pl.Buffered(1))
    weights = (wo, w1, w2, wg, wp)
    gains = (g_mix, g_pre, g_post)
    widest = max(a.shape[2] for a in weights)
    return pl.pallas_call(
        functools.partial(_post_kernel, ff_chunk=ff_chunk, layer=layer),
        grid=(t // tm,),
        in_specs=[row(dm), row(GLA_W), row(DIL_W),
                  pl.BlockSpec((None, tm, p.shape[2]), lambda i: (layer, i, 0))]
        + [pl.BlockSpec(memory_space=pl.ANY) for _ in weights] + [const(a) for a in gains],
        out_specs=row(dm),
        out_shape=jax.ShapeDtypeStruct((t, dm), F32),
        scratch_shapes=[pltpu.VMEM(a.shape[1:], BF16) for a in weights]
        + [pltpu.VMEM((2, CAST_ROWS, widest), F32), pltpu.SemaphoreType.DMA((2,))],
        compiler_params=pltpu.CompilerParams(
            dimension_semantics=("arbitrary",), vmem_limit_bytes=VMEM_LIMIT),
        name="post",
    )(h, og, od, p, *weights, *gains)


def kernel(x, p, w_in, w_gla_a2, b_gla_a, gla_norm_g, w_out, rel_bias, pre_mix_g, post_mix_g,
           pre_mlp_g, post_mlp_g, w_mlp_in, w_mlp_out, w_ple_gate, w_ple_proj):
    batch, seq, dm = x.shape
    depth = w_in.shape[0]
    assert seq % (max(DILATIONS) * BLK) == 0 and seq % MACRO == 0
    t = batch * seq
    rows1 = lambda a: a.reshape(depth, 1, -1)
    bias = _bias_table(rel_bias)
    h = x.reshape(t, dm)
    w = _prep_w_in(w_in)
    wa2 = jnp.pad(w_gla_a2, ((0, 0), (0, C_END - C_LR - GLA_LOWRANK), (0, 0))).astype(BF16)
    p2 = p.reshape(depth, t, -1)
    for i in range(depth):
        gq, gk, la, gv, gg, a1, a4, a16 = _in_proj(
            h, rows1(pre_mix_g), w, wa2, rows1(b_gla_a), layer=i, batch=batch, seq=seq, tm=IN_PROJ_ROWS)
        og = _gla(gq, gk, la, gv, gg, gla_norm_g[i].reshape(1, -1), batch=batch, seq=seq)
        od = _dil_attn(a1, a4, a16, bias, batch=batch, seq=seq)
        h = _post(h, og.reshape(t, GLA_W), od.reshape(t, DIL_W), p2,
                  w_out, w_mlp_in, w_mlp_out, w_ple_gate, w_ple_proj,
                  rows1(post_mix_g), rows1(pre_mlp_g), rows1(post_mlp_g), layer=i, tm=POST_ROWS)
    return h.reshape(batch, seq, dm)
```

```python
import functools
import math

import numpy as np
import jax
import jax.numpy as jnp
from jax import lax
from jax.experimental import pallas as pl
from jax.experimental.pallas import tpu as pltpu

F32 = jnp.float32
BF16 = jnp.bfloat16

EPS = 1e-6
GLA_HEADS = 4
GLA_DK = 64
GLA_DV = 128
GLA_QK = GLA_HEADS * GLA_DK
GLA_W = GLA_HEADS * GLA_DV
GLA_LOWRANK = 16
GLA_TAU = 16.0
DIL_HEADS = 8
DIL_HD = 64
DIL_W = DIL_HEADS * DIL_HD
DILATIONS = (1, 4, 16)
BAND = 128
BLK = 128
REL_BUCKETS = 32
REL_MAX_DIST = 2048
NEG = -1e30
LOG2E = math.log2(math.e)

LANES = 128
SUBLANES = 8
MACRO = 128
GLA_LEVELS = 7
VMEM_LIMIT = 56 * 1024 * 1024
IN_PROJ_ROWS = 512
POST_ROWS = 512
CAST_ROWS = 128
CAST_SLOTS = 4

C_GQ, C_GK, C_GV, C_GG, C_DQ, C_LR, C_END = 0, 256, 512, 1024, 1536, 3072, 3200


def _rms(x, g):
    return x * lax.rsqrt(jnp.mean(x * x, axis=-1, keepdims=True) + EPS) * g


def _ring(n, n_slot, copy, consume):
    ahead = n_slot - 1
    for i in range(min(ahead, n)):
        copy(i).start()
    for i in range(n):
        if i + ahead < n:
            copy(i + ahead).start()
        copy(i).wait()
        consume(i)


def _stage_w_in(wt_hbm, w_ref, stage_ref, sem_ref, layer):
    lr0, lr1 = C_DQ, C_DQ + GLA_LOWRANK
    rows = stage_ref.shape[1]
    chunks = ([(r, rows, r) for r in range(0, lr0, rows)]
              + [(lr1 + r, rows, lr0 + r) for r in range(0, C_LR - lr0, rows)]
              + [(lr0, GLA_LOWRANK, C_LR)])
    n_slot = stage_ref.shape[0]

    def copy(i):
        r0, n, _ = chunks[i]
        return pltpu.make_async_copy(wt_hbm.at[layer, pl.ds(r0, n), :],
                                     stage_ref.at[i % n_slot, pl.ds(0, n), :], sem_ref.at[i % n_slot])

    def consume(i):
        _, n, c0 = chunks[i]
        x = stage_ref[i % n_slot, :n, :].T
        if n < rows:
            x = jnp.concatenate([x, jnp.zeros((x.shape[0], C_END - C_LR - n), F32)], axis=1)
        w_ref[:, c0:c0 + x.shape[1]] = x.astype(BF16)

    _ring(len(chunks), n_slot, copy, consume)


def _in_proj_kernel(h_ref, g_ref, wt_hbm, wa2_ref, ba_ref,
                    gq_ref, gk_ref, la_ref, gv_ref, gg_ref, a1_ref, a4_ref, a16_ref,
                    slab_ref, slab4_ref, w_ref, stage_ref, sem_ref, *, tm, layer):
    @pl.when((pl.program_id(0) == 0) & (pl.program_id(1) == 0))
    def _():
        _stage_w_in(wt_hbm, w_ref, stage_ref, sem_ref, layer)

    xn = _rms(h_ref[...], g_ref[...]).astype(BF16)

    def proj(lo, hi):
        return jnp.dot(xn, w_ref[:, lo:hi], preferred_element_type=F32)

    z = jnp.dot(proj(C_LR, C_END).astype(BF16), wa2_ref[...],
                preferred_element_type=F32) + ba_ref[...]
    la_ref[...] = (jnp.minimum(z, 0.0) - jnp.log1p(jnp.exp(-jnp.abs(z)))) * (1.0 / GLA_TAU)

    n_grp = DIL_W // LANES

    def project_dilated(c):
        y = proj(C_DQ + DIL_W * c, C_DQ + DIL_W * (c + 1))
        if c == 0:
            y = y * (LOG2E * DIL_HD ** -0.5)
        for s in range(n_grp):
            ys = y[:, LANES * s:LANES * (s + 1)]
            idx = c * n_grp + s
            a1_ref[0, idx] = ys.astype(BF16)
            slab_ref[idx] = ys

    def deinterleave(c):
        for idx in range(c * n_grp, (c + 1) * n_grp):
            for r in range(4):
                y4 = slab_ref[idx, pl.ds(r, tm // 4, stride=4), :]
                a4_ref[r, idx] = y4.astype(BF16)
                slab4_ref[idx, r] = y4
            for r in range(16):
                a16_ref[r, idx] = slab4_ref[idx, r % 4, pl.ds(r // 4, tm // 16, stride=4), :].astype(BF16)

    project_dilated(0)
    project_dilated(1)
    deinterleave(0)
    project_dilated(2)
    deinterleave(1)
    gq_ref[...] = (proj(C_GQ, C_GK) * (GLA_DK ** -0.5)).astype(BF16)
    gk_ref[...] = proj(C_GK, C_GV).astype(BF16)
    deinterleave(2)
    gv_ref[...] = proj(C_GV, C_GG).astype(BF16)
    gg_ref[...] = proj(C_GG, C_DQ).astype(BF16)


def _in_proj(h, g, w, wa2, ba, *, layer, batch, seq, tm):
    t = batch * seq
    nt = seq // tm
    row = lambda width: pl.BlockSpec((tm, width), lambda b, i: (b * nt + i, 0))
    const = lambda shape: pl.BlockSpec((None,) + shape[1:], lambda b, i: (layer, 0, 0))
    n_slab = 3 * DIL_W // LANES
    dil_spec = lambda d: pl.BlockSpec((None, d, n_slab, tm // d, LANES), lambda b, i: (b, 0, 0, i, 0))
    out_shape = (
        jax.ShapeDtypeStruct((t, GLA_QK), BF16), jax.ShapeDtypeStruct((t, GLA_QK), BF16),
        jax.ShapeDtypeStruct((t, GLA_QK), F32),
        jax.ShapeDtypeStruct((t, GLA_W), BF16), jax.ShapeDtypeStruct((t, GLA_W), BF16),
    ) + tuple(jax.ShapeDtypeStruct((batch, d, n_slab, seq // d, LANES), BF16) for d in DILATIONS)
    dm = h.shape[1]
    assert w.shape[1] - GLA_LOWRANK == C_LR
    return pl.pallas_call(
        functools.partial(_in_proj_kernel, tm=tm, layer=layer),
        grid=(batch, nt),
        in_specs=[row(dm), const(g.shape), pl.BlockSpec(memory_space=pl.ANY), const(wa2.shape),
                  const(ba.shape)],
        out_specs=(row(GLA_QK), row(GLA_QK), row(GLA_QK), row(GLA_W), row(GLA_W))
        + tuple(dil_spec(d) for d in DILATIONS),
        out_shape=out_shape,
        scratch_shapes=[pltpu.VMEM((n_slab, tm, LANES), F32),
                        pltpu.VMEM((n_slab, 4, tm // 4, LANES), F32),
                        pltpu.VMEM((dm, C_END), BF16),
                        pltpu.VMEM((CAST_SLOTS, 2 * CAST_ROWS, dm), F32),
                        pltpu.SemaphoreType.DMA((CAST_SLOTS,))],
        compiler_params=pltpu.CompilerParams(
            dimension_semantics=("arbitrary", "arbitrary"), vmem_limit_bytes=VMEM_LIMIT),
        name="in_proj",
    )(h, g, w, wa2, ba)


def _gla_constants():
    n = MACRO
    i = np.arange(n)[:, None]
    t = np.arange(n)[None, :]
    p = np.concatenate([t <= i, t > i], axis=0).astype(np.float32)
    j = np.arange(n)[None, :]
    lvl = np.where(j < i, np.floor(np.log2(np.maximum(i ^ j, 1))).astype(np.int32),
                   np.where(j == i, GLA_LEVELS, -1)).astype(np.int32)
    return p, np.concatenate([lvl, lvl], axis=0)


def _split_heads(x, lo_half):
    zero = jnp.zeros_like(x)
    return jnp.concatenate([jnp.where(lo_half, x, zero), jnp.where(lo_half, zero, x)], axis=0)


def _dot_nt(a, b):
    return lax.dot_general(a, b, (((1,), (1,)), ((), ())), preferred_element_type=F32)


def _dot_tn(a, b):
    return lax.dot_general(a, b, (((0,), (0,)), ((), ())), preferred_element_type=F32)


def _interleave(*gens):
    live = [iter(g) for g in gens]
    while live:
        for g in list(live):
            try:
                next(g)
            except StopIteration:
                live.remove(g)


def _gla_macro(rows, q_ref, k_ref, la_ref, v_ref, gg_ref, p_ref, lvl, gn_ref, o_ref, st_ref, lo_half):
    q = q_ref[rows, :].astype(F32)
    k = k_ref[rows, :].astype(F32)
    la = la_ref[rows, :]
    la_hi = la.astype(BF16)
    la2 = jnp.concatenate([la_hi, (la - la_hi.astype(F32)).astype(BF16)], axis=1)

    def range_sum(blk):
        r = jnp.dot(p_ref[blk * MACRO:(blk + 1) * MACRO, :], la2, preferred_element_type=F32)
        return r[:, :LANES] + r[:, LANES:]

    b = range_sum(0)
    x_start = jnp.exp(b)
    qg = q * x_start
    kg = k * jnp.exp(range_sum(1))
    a_last = x_start[MACRO - 1:MACRO, :]

    @functools.cache
    def row_bcast(r):
        return jnp.broadcast_to(b[r:r + 1, :], (SUBLANES, LANES))

    sub = lax.broadcasted_iota(jnp.int32, (SUBLANES, LANES), 0)

    def ref_rows(l):
        span = 2 << l
        pieces = []
        for g in range(MACRO // SUBLANES):
            group_rows = range(SUBLANES * g, SUBLANES * (g + 1))
            refs = sorted({(i // span) * span + span // 2 - 1 for i in group_rows})
            piece = row_bcast(refs[-1])
            for c in reversed(range(len(refs) - 1)):
                piece = jnp.where(sub < (c + 1) * span, row_bcast(refs[c]), piece)
            pieces.append(piece)
        return jnp.concatenate(pieces, axis=0)

    att = _dot_nt(_split_heads(q, lo_half).astype(BF16), k.astype(BF16))
    att = jnp.where(lvl == GLA_LEVELS, att, 0.0)
    yield
    for l in range(GLA_LEVELS):
        xl = jnp.exp(-jnp.abs(b - ref_rows(l)))
        a = _dot_nt(_split_heads(q * xl, lo_half).astype(BF16), (k * xl).astype(BF16))
        att = jnp.where(lvl == l, a, att)
        yield
    att = att.astype(BF16)
    st = st_ref[...]
    inter = _dot_nt(_split_heads(qg, lo_half).astype(BF16), st.astype(BF16))
    kgp = kg.astype(BF16)
    upd = []
    for e in range(2):
        hs = slice(GLA_DV * e, GLA_DV * (e + 1))
        vh = v_ref[rows, hs]
        o = (jnp.dot(att[e * MACRO:(e + 1) * MACRO], vh, preferred_element_type=F32)
             + inter[e * MACRO:(e + 1) * MACRO])
        gate = gg_ref[rows, hs].astype(F32)
        o = _rms(o, gn_ref[...]) * (gate / (1.0 + jnp.exp(-gate)))
        o_ref[rows, hs] = o.astype(o_ref.dtype)
        upd.append(_dot_tn(vh, kgp))
    st_ref[...] = st * a_last + jnp.where(lo_half, upd[0], upd[1])
    yield


def _gla_kernel(q_ref, k_ref, la_ref, v_ref, gg_ref, p_ref, lvl_ref, gn_ref, o_ref, st_ref,
                *, n_macro, group, unroll):
    st_ref[...] = jnp.zeros_like(st_ref)
    lo_half = lax.broadcasted_iota(jnp.int32, (MACRO, LANES), 1) < (LANES // 2)

    lvl = lvl_ref[...]
    n_pair = GLA_HEADS // 2

    def body(i, carry):
        streams = []
        for mm in range(group):
            rows = pl.ds(pl.multiple_of((i * group + mm) * MACRO, MACRO), MACRO)
            for pair in range(n_pair):
                qk = pl.ds(LANES * pair, LANES)
                vg = pl.ds(2 * GLA_DV * pair, 2 * GLA_DV)
                streams.append(_gla_macro(
                    rows, q_ref.at[:, qk], k_ref.at[:, qk], la_ref.at[:, qk], v_ref.at[:, vg],
                    gg_ref.at[:, vg], p_ref, lvl, gn_ref, o_ref.at[:, vg], st_ref.at[pair], lo_half))
        _interleave(*streams)
        return carry

    lax.fori_loop(0, n_macro // group, body, 0, unroll=unroll)


def _gla(gq, gk, la, gv, gg, gn, *, batch, seq, group=2, unroll=4):
    p_np, lvl_np = _gla_constants()
    pm = jnp.asarray(p_np, BF16)
    lvl = jnp.asarray(lvl_np)
    seq_spec = lambda width: pl.BlockSpec((None, seq, width), lambda b: (b, 0, 0))
    const = lambda shape: pl.BlockSpec(shape, lambda b: (0,) * len(shape))
    r3 = lambda a: a.reshape(batch, seq, a.shape[-1])
    return pl.pallas_call(
        functools.partial(_gla_kernel, n_macro=seq // MACRO, group=group, unroll=unroll),
        grid=(batch,),
        in_specs=[seq_spec(GLA_QK), seq_spec(GLA_QK), seq_spec(GLA_QK), seq_spec(GLA_W),
                  seq_spec(GLA_W), const(pm.shape), const(lvl.shape), const(gn.shape)],
        out_specs=seq_spec(GLA_W),
        out_shape=jax.ShapeDtypeStruct((batch, seq, GLA_W), BF16),
        scratch_shapes=[pltpu.VMEM((GLA_HEADS // 2, GLA_DV, LANES), F32)],
        compiler_params=pltpu.CompilerParams(
            dimension_semantics=("arbitrary",), vmem_limit_bytes=VMEM_LIMIT),
        name="gla",
    )(r3(gq), r3(gk), r3(la), r3(gv), r3(gg), pm, lvl, gn)


def _t5_bucket_np(dist):
    max_exact = REL_BUCKETS // 2
    d = np.maximum(dist, 1).astype(np.float32)
    large = max_exact + (np.log(d / np.float32(max_exact)) / np.float32(math.log(REL_MAX_DIST / max_exact))
                         * np.float32(REL_BUCKETS - max_exact)).astype(np.int32)
    large = np.minimum(large, REL_BUCKETS - 1)
    return np.where(dist < max_exact, dist, large).astype(np.int32)


def _bucket_table():
    qi = np.arange(BLK)[:, None]
    ki = np.arange(2 * BLK)[None, :]
    j = qi + BLK - ki
    valid = (j >= 0) & (j <= BAND)
    return np.stack([np.where(valid, _t5_bucket_np(np.maximum(j, 0) * d), -1) for d in DILATIONS]).astype(np.int32)


def _bias_kernel(rel_ref, bkt_ref, o_ref, *, buckets):
    for p, used in enumerate(buckets):
        bkt = bkt_ref[p]
        accs = [jnp.full(bkt.shape, NEG, F32) for _ in range(DIL_HEADS)]
        for u in used:
            hit = bkt == u
            accs = [jnp.where(hit, rel_ref[u, h] * LOG2E, a) for h, a in enumerate(accs)]
        for h, a in enumerate(accs):
            o_ref[p, h // 2, (h % 2) * BLK:(h % 2 + 1) * BLK, :] = a


def _bias_table(rel_bias):
    bkt_np = _bucket_table()
    buckets = tuple(tuple(int(u) for u in np.unique(b) if u >= 0) for b in bkt_np)
    n_pat = len(DILATIONS)
    return pl.pallas_call(
        functools.partial(_bias_kernel, buckets=buckets),
        in_specs=[pl.BlockSpec(memory_space=pltpu.SMEM), pl.BlockSpec(memory_space=pltpu.VMEM)],
        out_specs=pl.BlockSpec(memory_space=pltpu.VMEM),
        out_shape=jax.ShapeDtypeStruct((n_pat, DIL_HEADS // 2, 2 * BLK, 2 * BLK), F32),
        name="bias_table",
    )(rel_bias, jnp.asarray(bkt_np))


def _attn_kernel(q1, k1, v1, q4, k4, v4, q16, k16, v16, bias_ref, o_ref,
                 s_ref, mb_ref, u_ref, w_ref, m_ref, *, seq):
    lo_half = lax.broadcasted_iota(jnp.int32, (BLK, LANES), 1) < DIL_HD
    refs = ((q1, k1, v1), (q4, k4, v4), (q16, k16, v16))
    nblk = seq // BLK

    def rows_of(ref, p, g, with_prev=False):
        r, n = divmod(g, nblk // DILATIONS[p])
        return ref[r, pl.ds((n - 1) * BLK, 2 * BLK) if with_prev else pl.ds(n * BLK, BLK), :]

    one_trip = jnp.minimum(pl.program_id(0) + 1, 1)

    def region(fn):
        lax.fori_loop(0, one_trip, lambda i, c: (fn(), c)[1], 0)

    def natural_rows(p, g):
        d = DILATIONS[p]
        r, n = divmod(g, nblk // d)
        return pl.ds(n * (BLK * d) + r, BLK, stride=d) if d > 1 else pl.ds(g * BLK, BLK)

    def scores(p):
        q_ref, k_ref, _ = refs[p]
        nb = nblk // DILATIONS[p]
        for g in range(nblk):
            has_prev = g % nb > 0
            nk = 2 * BLK if has_prev else BLK
            bias = bias_ref[p] if has_prev else bias_ref[p, :, BLK:]
            q = rows_of(q_ref, p, g)
            s = _dot_nt(_split_heads(q, lo_half), rows_of(k_ref, p, g, has_prev)) + bias
            s_ref[p * nblk + g, :, :nk] = s
            mb = jnp.broadcast_to(jnp.max(s, axis=-1, keepdims=True), (2 * BLK, LANES))
            mb_ref[p * nblk + g] = mb
            m_ref[p, natural_rows(p, g), :] = jnp.where(lo_half, mb[:BLK], mb[BLK:])
            yield

    def outputs(p):
        _, _, v_ref = refs[p]
        nb = nblk // DILATIONS[p]
        for g in range(nblk):
            has_prev = g % nb > 0
            nk = 2 * BLK if has_prev else BLK
            idx = p * nblk + g
            e = jnp.concatenate(
                [jnp.exp2(s_ref[idx, pl.ds(h * BLK, BLK), c * LANES:(c + 1) * LANES]
                          - mb_ref[idx, pl.ds(h * BLK, BLK), :])
                 for h in range(2) for c in range(nk // LANES)], axis=1).astype(BF16)
            vv = rows_of(v_ref, p, g, has_prev)
            lo_k = lax.broadcasted_iota(jnp.int32, vv.shape, 1) < DIL_HD
            zero = jnp.zeros_like(vv)
            lane = lax.broadcasted_iota(jnp.int32, vv.shape, 1)
            sum_a = jnp.where(lane < DIL_HD, 1.0, 0.0).astype(BF16)
            sum_b = jnp.where(lane < DIL_HD, 0.0, 1.0).astype(BF16)
            rhs = jnp.concatenate(
                [jnp.concatenate([jnp.where(lo_k, vv, zero), sum_a], axis=1),
                 jnp.concatenate([jnp.where(lo_k, zero, vv), sum_b], axis=1)], axis=0)
            uw = jnp.dot(e, rhs, preferred_element_type=F32)
            dst = natural_rows(p, g)
            u_ref[p, dst, :] = uw[:, :LANES]
            w_ref[p, dst, :] = uw[:, LANES:]
            yield

    order = tuple(range(len(DILATIONS)))
    region(lambda: _interleave(scores(order[0])))
    for cur, nxt in zip(order, order[1:] + (None,)):
        def both(cur=cur, nxt=nxt):
            _interleave(outputs(cur), *([scores(nxt)] if nxt is not None else []))
        region(both)

    chunk = 2 * BLK

    def combine(i, carry):
        rows = pl.ds(pl.multiple_of(i * chunk, chunk), chunk)
        ms = [m_ref[p, rows, :] for p in range(len(DILATIONS))]
        mmax = functools.reduce(jnp.maximum, ms)
        cs = [jnp.exp2(mp - mmax) for mp in ms]
        num = sum(c * u_ref[p, rows, :] for p, c in enumerate(cs))
        den = sum(c * w_ref[p, rows, :] for p, c in enumerate(cs))
        o_ref[rows, :] = (num / den).astype(o_ref.dtype)
        return carry

    lax.fori_loop(0, seq // chunk, combine, 0, unroll=2)


def _dil_attn(a1, a4, a16, bias, *, batch, seq):
    n_pat = len(DILATIONS)
    n_pair = DIL_HEADS // 2
    arrs = (a1, a4, a16)

    def spec(d, part):
        return pl.BlockSpec((None, d, None, seq // d, LANES),
                            lambda h, b: (b, 0, part * n_pair + h, 0, 0))

    in_specs = [spec(d, part) for d in DILATIONS for part in range(3)]
    in_specs.append(pl.BlockSpec((n_pat, None, 2 * BLK, 2 * BLK), lambda h, b: (0, h, 0, 0)))
    args = [a for a in arrs for _ in range(3)] + [bias]
    return pl.pallas_call(
        functools.partial(_attn_kernel, seq=seq),
        grid=(n_pair, batch),
        in_specs=in_specs,
        out_specs=pl.BlockSpec((None, seq, LANES), lambda h, b: (b, 0, h)),
        out_shape=jax.ShapeDtypeStruct((batch, seq, DIL_W), BF16),
        scratch_shapes=[pltpu.VMEM((n_pat * seq // BLK, 2 * BLK, 2 * BLK), F32),
                        pltpu.VMEM((n_pat * seq // BLK, 2 * BLK, LANES), F32)]
        + [pltpu.VMEM((n_pat, seq, LANES), F32) for _ in range(3)],
        compiler_params=pltpu.CompilerParams(
            dimension_semantics=("arbitrary", "arbitrary"), vmem_limit_bytes=VMEM_LIMIT),
        name="dil_attn",
    )(*args)


def _stage_weights(pairs, stage_ref, sem_ref, layer):
    chunks = [(src, dst, r0) for src, dst in pairs for r0 in range(0, dst.shape[0], CAST_ROWS)]
    n_slot = stage_ref.shape[0]

    def copy(i):
        src, dst, r0 = chunks[i]
        return pltpu.make_async_copy(src.at[layer, pl.ds(r0, CAST_ROWS), :],
                                     stage_ref.at[i % n_slot, :, pl.ds(0, dst.shape[1])],
                                     sem_ref.at[i % n_slot])

    def consume(i):
        _, dst, r0 = chunks[i]
        dst[pl.ds(r0, CAST_ROWS), :] = stage_ref[i % n_slot, :, :dst.shape[1]].astype(BF16)

    _ring(len(chunks), n_slot, copy, consume)


def _post_kernel(h_ref, og_ref, od_ref, p_ref, wo_hbm, w1_hbm, w2_hbm, wg_hbm, wp_hbm,
                 g_mix_ref, g_pre_ref, g_post_ref, o_ref,
                 wo_ref, w1_ref, w2_ref, wg_ref, wp_ref, stage_ref, sem_ref, *, ff_chunk, layer):
    @pl.when(pl.program_id(0) == 0)
    def _():
        _stage_weights(((wo_hbm, wo_ref), (w1_hbm, w1_ref), (w2_hbm, w2_ref), (wg_hbm, wg_ref),
                        (wp_hbm, wp_ref)), stage_ref, sem_ref, layer)

    mix = (jnp.dot(og_ref[...], wo_ref[:GLA_W, :], preferred_element_type=F32)
           + jnp.dot(od_ref[...], wo_ref[GLA_W:, :], preferred_element_type=F32))
    emb = jnp.dot(p_ref[...].astype(BF16), wp_ref[...], preferred_element_type=F32)
    h1 = h_ref[...] + _rms(mix, g_mix_ref[...])
    xn = _rms(h1, g_pre_ref[...]).astype(BF16)
    f = jnp.zeros_like(h1)
    for c in range(w1_ref.shape[1] // ff_chunk):
        cols = slice(c * ff_chunk, (c + 1) * ff_chunk)
        a = jnp.maximum(jnp.dot(xn, w1_ref[:, cols], preferred_element_type=F32), 0.0)
        f = f + jnp.dot((a * a).astype(BF16), w2_ref[cols, :], preferred_element_type=F32)
    h2 = h1 + _rms(f, g_post_ref[...])
    gate = jnp.dot(h2.astype(BF16), wg_ref[...], preferred_element_type=F32)
    o_ref[...] = h2 + emb / (1.0 + jnp.exp(-gate))


def _post(h, og, od, p, wo, w1, w2, wg, wp, g_mix, g_pre, g_post, *, layer, tm, ff_chunk=1024):
    t, dm = h.shape
    row = lambda width: pl.BlockSpec((tm, width), lambda i: (i, 0))
    const = lambda a: pl.BlockSpec((None,) + a.shape[1:], lambda i: (layer, 0, 0),
                                   pipeline_mode=pl.Buffered(1))
    weights = (wo, w1, w2, wg, wp)
    gains = (g_mix, g_pre, g_post)
    widest = max(a.shape[2] for a in weights)
    return pl.pallas_call(
        functools.partial(_post_kernel, ff_chunk=ff_chunk, layer=layer),
        grid=(t // tm,),
        in_specs=[row(dm), row(GLA_W), row(DIL_W),
                  pl.BlockSpec((None, tm, p.shape[2]), lambda i: (layer, i, 0))]
        + [pl.BlockSpec(memory_space=pl.ANY) for _ in weights] + [const(a) for a in gains],
        out_specs=row(dm),
        out_shape=jax.ShapeDtypeStruct((t, dm), F32),
        scratch_shapes=[pltpu.VMEM(a.shape[1:], BF16) for a in weights]
        + [pltpu.VMEM((CAST_SLOTS, CAST_ROWS, widest), F32), pltpu.SemaphoreType.DMA((CAST_SLOTS,))],
        compiler_params=pltpu.CompilerParams(
            dimension_semantics=("arbitrary",), vmem_limit_bytes=VMEM_LIMIT),
        name="post",
    )(h, og, od, p, *weights, *gains)


def kernel(x, p, w_in, w_gla_a2, b_gla_a, gla_norm_g, w_out, rel_bias, pre_mix_g, post_mix_g,
           pre_mlp_g, post_mlp_g, w_mlp_in, w_mlp_out, w_ple_gate, w_ple_proj):
    batch, seq, dm = x.shape
    depth = w_in.shape[0]
    assert seq % (max(DILATIONS) * BLK) == 0 and seq % MACRO == 0
    t = batch * seq
    rows1 = lambda a: a.reshape(depth, 1, -1)
    bias = _bias_table(rel_bias)
    h = x.reshape(t, dm)
    w = jnp.swapaxes(w_in, 1, 2)
    wa2 =jnp.pad(w_gla_a2, ((0, 0), (0, C_END - C_LR - GLA_LOWRANK), (0, 0))).astype(BF16)
    p2 = p.reshape(depth, t, -1)
    for i in range(depth):
        gq, gk, la, gv, gg, a1, a4, a16 = _in_proj(
            h, rows1(pre_mix_g), w, wa2, rows1(b_gla_a), layer=i, batch=batch, seq=seq, tm=IN_PROJ_ROWS)
        og = _gla(gq, gk, la, gv, gg, gla_norm_g[i].reshape(1, -1), batch=batch, seq=seq)
        od = _dil_attn(a1, a4, a16, bias, batch=batch, seq=seq)
        h = _post(h, og.reshape(t, GLA_W), od.reshape(t, DIL_W), p2,
                  w_out, w_mlp_in, w_mlp_out, w_ple_gate, w_ple_proj,
                  rows1(post_mix_g), rows1(pre_mlp_g), rows1(post_mlp_g), layer=i, tm=POST_ROWS)
    return h.reshape(batch, seq, dm)
```
